```python
import jax, jax.numpy as jnp
from jax import lax
import numpy as np

D_MODEL = 1024
BATCH = 8
SEQ = 4096
DEPTH = 2
DEC_BATCH = 1
DEC_SEQ = 16384
PAST_LEN = 128

HEAD_DIM = 64
N_META = 16
GRID_W = 64
BLOCK = 128
RMS_EPS = 1e-6
A_HEADS = 12
A_KV_HEADS = 4
A_THETA = 10000.0
B_GROUPS = 4
B_DIM = 64
C_HEADS = 8
C_KV_HEADS = 2
WINDOW = 128
ROPE_THETA = 500000.0
ROPE_DIMS = HEAD_DIM // 4
D_HEADS = 8
D_WIDTH = D_HEADS * HEAD_DIM
DECAY_RANK = 64
ICLR_RANK = 64
GATE_RANK = 128
LNX_EPS = 64e-5
D_FF = ((8 * D_MODEL // 3 + 255) // 256) * 256
N_EVEN = (DEPTH + 1) // 2
N_ODD = DEPTH // 2
A_Q = A_HEADS * HEAD_DIM
A_KV = A_KV_HEADS * HEAD_DIM
B_W = B_GROUPS * B_DIM
EVEN_IN = A_Q + 2 * A_KV + B_W
EVEN_OUT = A_Q + B_W
C_Q = C_HEADS * HEAD_DIM
C_KV = C_KV_HEADS * HEAD_DIM
D_IN = 3 * D_WIDTH + 2 * DECAY_RANK + 2 * ICLR_RANK + GATE_RANK
ODD_IN = C_Q + 2 * C_KV + D_IN
ODD_OUT = C_Q + D_WIDTH
D_SPLITS = (D_WIDTH, 2 * D_WIDTH, 3 * D_WIDTH,
            3 * D_WIDTH + DECAY_RANK, 3 * D_WIDTH + 2 * DECAY_RANK,
            3 * D_WIDTH + 2 * DECAY_RANK + ICLR_RANK, 3 * D_WIDTH + 2 * DECAY_RANK + 2 * ICLR_RANK)

kernel_name = 'hybrid_axialgqa_fnet_swa_rwkv7_encoder'


def rms_norm(x, g):
    xf = x.astype(jnp.float32)
    return xf * lax.rsqrt(jnp.mean(xf * xf, axis=-1, keepdims=True) + RMS_EPS) * g.astype(jnp.float32)


def rotate_half(x, ang):
    c = jnp.cos(ang)[:, None, :]
    s = jnp.sin(ang)[:, None, :]
    h = x.shape[-1] // 2
    x1, x2 = x[..., :h], x[..., h:]
    return jnp.concatenate([x1 * c - x2 * s, x1 * s + x2 * c], axis=-1)


def axial_angles(n_tok):
    rows = n_tok // GRID_W
    row = jnp.repeat(jnp.arange(rows), GRID_W)
    col = jnp.arange(rows * GRID_W) % GRID_W
    meta = jnp.arange(N_META) - N_META
    row = jnp.concatenate([meta, row]).astype(jnp.float32)
    col = jnp.concatenate([meta, col]).astype(jnp.float32)
    half = HEAD_DIM // 2
    inv = A_THETA ** (-jnp.arange(0, half, 2, dtype=jnp.float32) / half)
    return row[:, None] * inv, col[:, None] * inv


def partial_angles(n_pos):
    inv = ROPE_THETA ** (-jnp.arange(0, ROPE_DIMS, 2, dtype=jnp.float32) / ROPE_DIMS)
    return jnp.arange(n_pos, dtype=jnp.float32)[:, None] * inv


def dense_attention(qb, k, v):
    s = jnp.einsum('bqhgd,bshd->bhgqs', qb, k)
    p = jax.nn.softmax(s, axis=-1)
    return jnp.einsum('bhgqs,bshd->bqhgd', p, v)


def mixer_a(qa, ka, va, ang_row, ang_col, q_gain, k_gain):
    B, L = qa.shape[:2]
    G = A_HEADS // A_KV_HEADS
    half = HEAD_DIM // 2
    q = rms_norm(qa, q_gain)
    k = rms_norm(ka, k_gain)
    q = jnp.concatenate([rotate_half(q[..., :half], ang_row), rotate_half(q[..., half:], ang_col)], axis=-1) * HEAD_DIM ** -0.5
    k = jnp.concatenate([rotate_half(k[..., :half], ang_row), rotate_half(k[..., half:], ang_col)], axis=-1)
    v = va.astype(jnp.float32)
    q = q.reshape(B, L, A_KV_HEADS, G, HEAD_DIM)
    out_meta = dense_attention(q[:, :N_META], k, v)
    nb = (L - N_META) // BLOCK
    q_blocks = jnp.moveaxis(q[:, N_META:].reshape(B, nb, BLOCK, A_KV_HEADS, G, HEAD_DIM), 1, 0)
    out_real = lax.map(lambda qb: dense_attention(qb, k, v), q_blocks)
    out_real = jnp.moveaxis(out_real, 0, 1).reshape(B, L - N_META, A_KV_HEADS, G, HEAD_DIM)
    return jnp.concatenate([out_meta, out_real], axis=1).reshape(B, L, A_Q)


def mixer_b(fb, norm_g, w_lin, b_lin):
    B, L = fb.shape[:2]
    f = rms_norm(fb, norm_g)
    spec = jnp.fft.fft2(f, axes=(1, 3), norm='ortho').real.astype(jnp.float32)
    y = jnp.einsum('blgc,gcd->blgd', spec, w_lin) + b_lin
    return y.reshape(B, L, B_W)


def partial_rope(x, ang):
    return jnp.concatenate([rotate_half(x[..., :ROPE_DIMS], ang), x[..., ROPE_DIMS:]], axis=-1)


def attend_with_sink(s, mask, sink):
    s = jnp.where(mask, s, -jnp.inf)
    sk = jnp.broadcast_to(sink[:, :, None, None], s.shape[:-1] + (1,))
    return jax.nn.softmax(jnp.concatenate([s, sk], axis=-1), axis=-1)[..., :-1]


def mixer_c(qc, kc, vc, ang, sink):
    B, L = qc.shape[:2]
    S = L - N_META
    nb = S // BLOCK
    G = C_HEADS // C_KV_HEADS
    q = partial_rope(qc.astype(jnp.float32), ang) * HEAD_DIM ** -0.5
    k = partial_rope(kc.astype(jnp.float32), ang)
    v = vc.astype(jnp.float32)
    q = q.reshape(B, L, C_KV_HEADS, G, HEAD_DIM)
    sink = sink.astype(jnp.float32).reshape(C_KV_HEADS, G)
    pos_m = jnp.arange(N_META)
    t0 = jnp.arange(BLOCK)
    mask_m = jnp.concatenate([jnp.ones((N_META, N_META), bool),
                              (t0[None, :] + N_META - pos_m[:, None]) <= WINDOW], axis=1)
    k_m = jnp.concatenate([k[:, :N_META], k[:, N_META:N_META + BLOCK]], axis=1)
    v_m = jnp.concatenate([v[:, :N_META], v[:, N_META:N_META + BLOCK]], axis=1)
    s_m = jnp.einsum('bqhgd,bshd->bhgqs', q[:, :N_META], k_m)
    p_m = attend_with_sink(s_m, mask_m, sink)
    out_meta = jnp.einsum('bhgqs,bshd->bqhgd', p_m, v_m).reshape(B, N_META, C_Q)

    def band(t):
        tp = jnp.pad(t[:, N_META:], ((0, 0), (BLOCK, BLOCK), (0, 0), (0, 0)))
        tp = tp.reshape(B, nb + 2, BLOCK, C_KV_HEADS, HEAD_DIM)
        win = jnp.concatenate([tp[:, :-2], tp[:, 1:-1], tp[:, 2:]], axis=2)
        lead = jnp.broadcast_to(t[:, None, :N_META], (B, nb, N_META, C_KV_HEADS, HEAD_DIM))
        return jnp.concatenate([lead, win], axis=2)

    k_b, v_b = band(k), band(v)
    c = jnp.arange(nb)[:, None, None]
    i = jnp.arange(BLOCK)[None, :, None]
    j = jnp.arange(3 * BLOCK)[None, None, :]
    tq = c * BLOCK + i
    tk = (c - 1) * BLOCK + j
    mask_band = (jnp.abs(tq - tk) <= WINDOW) & (tk >= 0) & (tk < S)
    mask_r = jnp.concatenate([jnp.ones((nb, BLOCK, N_META), bool), mask_band], axis=-1)
    q_r = q[:, N_META:].reshape(B, nb, BLOCK, C_KV_HEADS, G, HEAD_DIM)
    s_r = jnp.einsum('bnqhgd,bnshd->bnhgqs', q_r, k_b)
    p_r = attend_with_sink(s_r, mask_r[:, None, None], sink)
    out_real = jnp.einsum('bnhgqs,bnshd->bnqhgd', p_r, v_b).reshape(B, S, C_Q)
    return jnp.concatenate([out_meta, out_real], axis=1)


def wkv_scan(r, w, k, v, a, b, reverse):
    B, L, H, N = r.shape
    xs = tuple(jnp.moveaxis(t, 1, 0) for t in (r, w, k, v, a, b))

    def step(S, inp):
        rt, wt, kt, vt, at, bt = inp
        sa = jnp.einsum('bhij,bhj->bhi', S, at)
        S = S * wt[:, :, None, :] + sa[..., None] * bt[:, :, None, :] + vt[..., None] * kt[:, :, None, :]
        return S, jnp.einsum('bhij,bhj->bhi', S, rt)

    S0 = jnp.zeros((B, H, N, N), jnp.float32)
    _, o = lax.scan(step, S0, xs, reverse=reverse)
    return jnp.moveaxis(o, 0, 1)


def mixer_d(u, mu_prev, mu_next, w0, w_up, a0, a_up, g_up, k_k, k_a, r_k, ln_g, ln_b):
    B, L, _ = u.shape
    u = u.astype(jnp.float32)
    u_prev = jnp.pad(u, ((0, 0), (1, 0), (0, 0)))[:, :-1]
    u_next = jnp.pad(u, ((0, 0), (0, 1), (0, 0)))[:, 1:]
    u = u + mu_prev * (u_prev - u) + mu_next * (u_next - u)
    r, k, v, wfd, wbd, afd, abd, gd = jnp.split(u, D_SPLITS, axis=-1)
    heads = lambda t: t.reshape(B, L, D_HEADS, HEAD_DIM)
    g = jax.nn.sigmoid(gd) @ g_up
    kk = heads(k * k_k)
    kk = kk * lax.rsqrt(jnp.maximum(jnp.sum(kk * kk, axis=-1, keepdims=True), 1e-24))
    r, k, v = heads(r), heads(k), heads(v)
    k_a = k_a.reshape(D_HEADS, HEAD_DIM)
    outs = []
    bonuses = []
    for d, (wd, ad) in enumerate(((wfd, afd), (wbd, abd))):
        w_log = -jax.nn.softplus(-(w0[d] + jnp.tanh(wd) @ w_up[d])) - 0.5
        decay = heads(jnp.exp(-jnp.exp(w_log)))
        a = heads(jax.nn.sigmoid(a0[d] + ad @ a_up[d]))
        kd = k * (1.0 + (a - 1.0) * k_a)
        outs.append(wkv_scan(r, decay, kd, v, -kk, kk * a, reverse=(d == 1)))
        bonuses.append(jnp.sum(r * kd * r_k, axis=-1, keepdims=True) * v)
    o = outs[0] + outs[1]
    mean = jnp.mean(o, axis=-1, keepdims=True)
    var = jnp.mean(jnp.square(o - mean), axis=-1, keepdims=True)
    y = (o - mean) * lax.rsqrt(var + LNX_EPS) * ln_g.reshape(D_HEADS, HEAD_DIM) + ln_b.reshape(D_HEADS, HEAD_DIM)
    y = y + bonuses[0] + bonuses[1]
    return y.reshape(B, L, D_WIDTH) * g


def swiglu(h, w_gate, w_up, w_down):
    return (jax.nn.silu(h @ w_gate) * (h @ w_up)) @ w_down


def trunk(x, p):
    B, S, _ = x.shape
    L = S + N_META
    meta = jnp.broadcast_to(p['meta_tokens'].astype(x.dtype)[None], (B, N_META, D_MODEL))
    h = jnp.concatenate([meta, x], axis=1)
    ang_row, ang_col = axial_angles(S)
    ang_1d = partial_angles(L)
    for i in range(DEPTH):
        hn = rms_norm(h, p['pre_mix_g'][i])
        if i % 2 == 0:
            e = i // 2
            proj = hn @ p['even_w_in'][e]
            qa = proj[..., :A_Q].reshape(B, L, A_HEADS, HEAD_DIM)
            ka = proj[..., A_Q:A_Q + A_KV].reshape(B, L, A_KV_HEADS, HEAD_DIM)
            va = proj[..., A_Q + A_KV:A_Q + 2 * A_KV].reshape(B, L, A_KV_HEADS, HEAD_DIM)
            fb = proj[..., A_Q + 2 * A_KV:].reshape(B, L, B_GROUPS, B_DIM)
            ya = mixer_a(qa, ka, va, ang_row, ang_col, p['a_q_gain'][e], p['a_k_gain'][e])
            yb = mixer_b(fb, p['b_norm_g'][e], p['b_w'][e], p['b_b'][e])
            mix = jnp.concatenate([ya, yb], axis=-1) @ p['even_w_out'][e]
        else:
            o = i // 2
            proj = hn @ p['odd_w_in'][o]
            qc = proj[..., :C_Q].reshape(B, L, C_HEADS, HEAD_DIM)
            kc = proj[..., C_Q:C_Q + C_KV].reshape(B, L, C_KV_HEADS, HEAD_DIM)
            vc = proj[..., C_Q + C_KV:C_Q + 2 * C_KV].reshape(B, L, C_KV_HEADS, HEAD_DIM)
            yc = mixer_c(qc, kc, vc, ang_1d, p['c_sink'][o])
            yd = mixer_d(proj[..., C_Q + 2 * C_KV:], p['d_mu_prev'][o], p['d_mu_next'][o],
                         p['d_w0'][o], p['d_w_up'][o], p['d_a0'][o], p['d_a_up'][o], p['d_g_up'][o],
                         p['d_k_k'][o], p['d_k_a'][o], p['d_r_k'][o], p['d_ln_g'][o], p['d_ln_b'][o])
            mix = jnp.concatenate([yc, yd], axis=-1) @ p['odd_w_out'][o]
        h = h + rms_norm(mix, p['post_mix_g'][i]).astype(h.dtype)
        hn = rms_norm(h, p['pre_ffn_g'][i])
        f = swiglu(hn, p['ffn_w_gate'][i], p['ffn_w_up'][i], p['ffn_w_down'][i])
        h = h + rms_norm(f, p['post_ffn_g'][i]).astype(h.dtype)
    return h[:, N_META:]


def setup_inputs(seed: int = 0) -> dict:
    key = jax.random.key(seed)
    ks = jax.random.split(key, 32)

    def nrm(i, shape, scale):
        return jax.random.normal(ks[i], shape, jnp.float32) * scale

    return {
        'x_prompt': nrm(0, (BATCH, SEQ, D_MODEL), 1.0),
        'x_sample': nrm(1, (DEC_BATCH, DEC_SEQ, D_MODEL), 1.0),
        'meta_tokens': nrm(2, (N_META, D_MODEL), 1.0),
        'pre_mix_g': 1.0 + nrm(3, (DEPTH, D_MODEL), 0.05),
        'post_mix_g': 1.0 + nrm(4, (DEPTH, D_MODEL), 0.05),
        'pre_ffn_g': 1.0 + nrm(5, (DEPTH, D_MODEL), 0.05),
        'post_ffn_g': 1.0 + nrm(6, (DEPTH, D_MODEL), 0.05),
        'even_w_in': nrm(7, (N_EVEN, D_MODEL, EVEN_IN), D_MODEL ** -0.5),
        'even_w_out': nrm(8, (N_EVEN, EVEN_OUT, D_MODEL), EVEN_OUT ** -0.5),
        'a_q_gain': 1.0 + nrm(9, (N_EVEN, HEAD_DIM), 0.05),
        'a_k_gain': 1.0 + nrm(10, (N_EVEN, HEAD_DIM), 0.05),
        'b_norm_g': 1.0 + nrm(11, (N_EVEN, B_GROUPS, B_DIM), 0.05),
        'b_w': nrm(12, (N_EVEN, B_GROUPS, B_DIM, B_DIM), B_DIM ** -0.5),
        'b_b': nrm(13, (N_EVEN, B_GROUPS, B_DIM), 0.01),
        'odd_w_in': nrm(14, (N_ODD, D_MODEL, ODD_IN), D_MODEL ** -0.5),
        'odd_w_out': nrm(15, (N_ODD, ODD_OUT, D_MODEL), ODD_OUT ** -0.5),
        'c_sink': nrm(16, (N_ODD, C_HEADS), 0.5),
        'd_mu_prev': jax.random.uniform(ks[17], (N_ODD, D_IN), jnp.float32, 0.0, 0.5),
        'd_mu_next': jax.random.uniform(ks[18], (N_ODD, D_IN), jnp.float32, 0.0, 0.5),
        'd_w0': jnp.linspace(-6.5, -1.5, D_WIDTH, dtype=jnp.float32)[None, None] + nrm(19, (N_ODD, 2, D_WIDTH), 0.1),
        'd_w_up': nrm(20, (N_ODD, 2, DECAY_RANK, D_WIDTH), 0.1 * DECAY_RANK ** -0.5),
        'd_a0': nrm(21, (N_ODD, 2, D_WIDTH), 0.1),
        'd_a_up': nrm(22, (N_ODD, 2, ICLR_RANK, D_WIDTH), 0.1 * ICLR_RANK ** -0.5),
        'd_g_up': nrm(23, (N_ODD, GATE_RANK, D_WIDTH), GATE_RANK ** -0.5),
        'd_k_k': 0.85 + nrm(24, (N_ODD, D_WIDTH), 0.05),
        'd_k_a': 1.0 + nrm(25, (N_ODD, D_WIDTH), 0.05),
        'd_r_k': nrm(26, (N_ODD, D_HEADS, HEAD_DIM), 0.1),
        'd_ln_g': 1.0 + nrm(27, (N_ODD, D_WIDTH), 0.1),
        'd_ln_b': nrm(28, (N_ODD, D_WIDTH), 0.01),
        'ffn_w_gate': nrm(29, (DEPTH, D_MODEL, D_FF), D_MODEL ** -0.5),
        'ffn_w_up': nrm(30, (DEPTH, D_MODEL, D_FF), D_MODEL ** -0.5),
        'ffn_w_down': nrm(31, (DEPTH, D_FF, D_MODEL), D_FF ** -0.5),
    }


def reference(x_prompt, x_sample, meta_tokens, pre_mix_g, post_mix_g, pre_ffn_g, post_ffn_g,
              even_w_in, even_w_out, a_q_gain, a_k_gain, b_norm_g, b_w, b_b,
              odd_w_in, odd_w_out, c_sink, d_mu_prev, d_mu_next, d_w0, d_w_up, d_a0, d_a_up,
              d_g_up, d_k_k, d_k_a, d_r_k, d_ln_g, d_ln_b, ffn_w_gate, ffn_w_up, ffn_w_down):
    params = dict(meta_tokens=meta_tokens, pre_mix_g=pre_mix_g, post_mix_g=post_mix_g,
                  pre_ffn_g=pre_ffn_g, post_ffn_g=post_ffn_g,
                  even_w_in=even_w_in, even_w_out=even_w_out, a_q_gain=a_q_gain, a_k_gain=a_k_gain,
                  b_norm_g=b_norm_g, b_w=b_w, b_b=b_b,
                  odd_w_in=odd_w_in, odd_w_out=odd_w_out, c_sink=c_sink,
                  d_mu_prev=d_mu_prev, d_mu_next=d_mu_next, d_w0=d_w0, d_w_up=d_w_up,
                  d_a0=d_a0, d_a_up=d_a_up, d_g_up=d_g_up, d_k_k=d_k_k, d_k_a=d_k_a, d_r_k=d_r_k,
                  d_ln_g=d_ln_g, d_ln_b=d_ln_b,
                  ffn_w_gate=ffn_w_gate, ffn_w_up=ffn_w_up, ffn_w_down=ffn_w_down)
    y_prompt = trunk(x_prompt, params)
    y_sample = trunk(x_sample, params)
    return (y_prompt, y_sample)
```

```python
import functools
import math

import numpy as np
import jax
import jax.numpy as jnp
from jax import lax
from jax.experimental import pallas as pl
from jax.experimental.pallas import tpu as pltpu

F32 = jnp.float32
BF16 = jnp.bfloat16

D_MODEL = 1024
HEAD_DIM = 64
N_META = 16
GRID_W = 64
WINDOW = 128
RMS_EPS = 1e-6
A_HEADS, A_KV_HEADS, A_THETA = 12, 4, 10000.0
B_GROUPS, B_DIM = 4, 64
C_HEADS, C_KV_HEADS = 8, 2
ROPE_THETA = 500000.0
ROPE_DIMS = HEAD_DIM // 4
D_HEADS = 8
D_WIDTH = D_HEADS * HEAD_DIM
DECAY_RANK, ICLR_RANK, GATE_RANK = 64, 64, 128
LNX_EPS = 64e-5
D_FF = 2816
A_Q, A_KV, B_W = A_HEADS * HEAD_DIM, A_KV_HEADS * HEAD_DIM, B_GROUPS * B_DIM
EVEN_IN = A_Q + 2 * A_KV + B_W
C_Q, C_KV = C_HEADS * HEAD_DIM, C_KV_HEADS * HEAD_DIM
D_IN = 3 * D_WIDTH + 2 * DECAY_RANK + 2 * ICLR_RANK + GATE_RANK
ODD_IN = C_Q + 2 * C_KV + D_IN

LANES = 128
FRONT = 128
PAD = FRONT - N_META
NEG = -1e30
SCAN_CHUNK = 64
HGROUP = 256
FFT_L1 = 16
V_ROWS = 72
VMEM_LIMIT = 56 * 1024 * 1024


def _round_up(x, m):
    return (x + m - 1) // m * m


def _cparams(sem):
    return pltpu.CompilerParams(dimension_semantics=sem, vmem_limit_bytes=VMEM_LIMIT)


def _dot(a, b):
    return jnp.dot(a, b, preferred_element_type=F32)


def _dot_nt(a, b):
    return lax.dot_general(a, b, (((1,), (1,)), ((), ())), preferred_element_type=F32)


def _dot_tn(a, b):
    return lax.dot_general(a, b, (((0,), (0,)), ((), ())), preferred_element_type=F32)


def _split2(x):
    hi = x.astype(BF16)
    lo = (x - hi.astype(F32)).astype(BF16)
    return hi, lo


def _group_mean(x, bs):
    hi, lo = _split2(x)
    return _dot(hi, bs) + _dot(lo, bs)


def _rope(x, cos, sin_a, sin_b, shift):
    n = x.shape[1]
    return x * cos + pltpu.roll(x, n - shift, 1) * sin_a + pltpu.roll(x, shift, 1) * sin_b


def _wide(t, n):
    return t if n == LANES else jnp.concatenate([t] * (n // LANES), axis=1)


def _rms_rows(x, g):
    ms = jnp.mean(x * x, axis=-1, keepdims=True)
    return x * lax.rsqrt(ms + RMS_EPS) * g


def _in_even_kernel(h_ref, g_ref, w_ref, qg_ref, kg_ref, bg_ref, bs_ref, pq_ref,
                    cos_ref, sa_ref, sb_ref, q_ref, k_ref, v_ref, gf_ref):
    hn = _rms_rows(h_ref[...], g_ref[...]).astype(BF16)
    proj = _dot(hn, w_ref[...])
    bs = bs_ref[...]
    cos, sa, sb = (_wide(r[...], HGROUP) for r in (cos_ref, sa_ref, sb_ref))

    def norm_rope(x, gain):
        xn = x * lax.rsqrt(_group_mean(x * x, bs) + RMS_EPS) * gain
        return _rope(xn, cos, sa, sb, 16).astype(BF16)

    for s in range(A_Q // HGROUP):
        cs = slice(s * HGROUP, (s + 1) * HGROUP)
        q_ref[:, cs] = norm_rope(proj[:, cs], qg_ref[:, cs])
    k_ref[...] = norm_rope(proj[:, A_Q:A_Q + A_KV], kg_ref[...])
    v_ref[...] = proj[:, A_Q + A_KV:A_Q + 2 * A_KV].astype(BF16)
    f = proj[:, A_Q + 2 * A_KV:]
    fn = f * lax.rsqrt(_group_mean(f * f, bs) + RMS_EPS) * bg_ref[...]
    gf_ref[...] = _dot(fn.astype(BF16), pq_ref[...])


def _in_odd_kernel(h_ref, cos_ref, sa_ref, sb_ref, g_ref, w_ref, q_ref, k_ref, v_ref, u_ref):
    hn = _rms_rows(h_ref[...], g_ref[...]).astype(BF16)
    proj = _dot(hn, w_ref[...])
    cos, sa, sb = cos_ref[...], sa_ref[...], sb_ref[...]
    cos2, sa2, sb2 = (_wide(t, HGROUP) for t in (cos, sa, sb))
    for s in range(C_Q // HGROUP):
        cs = slice(s * HGROUP, (s + 1) * HGROUP)
        q_ref[:, cs] = (_rope(proj[:, cs], cos2, sa2, sb2, 8) * HEAD_DIM ** -0.5).astype(BF16)
    k_ref[...] = _rope(proj[:, C_Q:C_Q + C_KV], cos, sa, sb, 8).astype(BF16)
    v_ref[...] = proj[:, C_Q + C_KV:C_Q + 2 * C_KV].astype(BF16)
    u_ref[...] = proj[:, C_Q + 2 * C_KV:]


def _out_proj_kernel(h_ref, ya_ref, yb_ref, wa_ref, wb_ref, g_ref, o_ref):
    mix = _dot(ya_ref[...], wa_ref[...]) + _dot(yb_ref[...], wb_ref[...])
    o_ref[...] = h_ref[...] + _rms_rows(mix, g_ref[...])


def _ffn_kernel(h_ref, g1_ref, wg_ref, wu_ref, wd_ref, g2_ref, o_ref, acc_ref, *, chunk):
    h = h_ref[...]
    hn = _rms_rows(h, g1_ref[...]).astype(BF16)
    for c in range(D_FF // chunk):
        cs = slice(c * chunk, (c + 1) * chunk)
        gate = _dot(hn, wg_ref[:, cs])
        up = _dot(hn, wu_ref[:, cs])
        act = (gate * (1.0 / (1.0 + jnp.exp(-gate))) * up).astype(BF16)
        part = _dot(act, wd_ref[cs, :])
        if c == 0:
            acc_ref[...] = part
        else:
            acc_ref[...] += part
    o_ref[...] = h + _rms_rows(acc_ref[...], g2_ref[...])


def _row_call(kernel, n_rows, tm, row_ins, const_ins, outs, scratch=()):
    grid = (n_rows // tm,)
    in_specs = [pl.BlockSpec((tm, a.shape[1]), lambda i: (i, 0)) for a in row_ins]
    in_specs += [pl.BlockSpec(a.shape, lambda i, nd=a.ndim: (0,) * nd) for a in const_ins]
    out_specs = [pl.BlockSpec((tm, n), lambda i: (i, 0)) for n, _ in outs]
    out_shape = [jax.ShapeDtypeStruct((n_rows, n), dt) for n, dt in outs]
    res = pl.pallas_call(
        kernel, grid=grid, in_specs=in_specs, out_specs=out_specs, out_shape=out_shape,
        scratch_shapes=list(scratch), compiler_params=_cparams(("parallel",)),
    )(*row_ins, *const_ins)
    return res


def _attn_a_kernel(qT_ref, k_ref, vT_ref, o_ref, m_scr, acc_scr, s_scr, *, nchunk, nkb, tq):
    j = pl.program_id(2)

    @pl.when(j == 0)
    def _():
        m_scr[...] = jnp.full(m_scr.shape, NEG, F32)
        acc_scr[...] = jnp.zeros(acc_scr.shape, F32)

    group = A_HEADS // A_KV_HEADS
    key_row = lax.broadcasted_iota(jnp.int32, (LANES, tq), 0) + j * (nchunk * LANES)
    for h in range(A_HEADS):
        g = h // group
        qh = qT_ref[0, h * HEAD_DIM:(h + 1) * HEAD_DIM, :]

        def scores(c):
            kc = k_ref[0, g, pl.ds(pl.multiple_of(c * LANES, LANES), LANES), :]
            return _dot(kc, qh)

        def fold(s):
            return jnp.max(s.reshape(LANES // 8, 8, tq), axis=0)

        s0 = jnp.where(key_row >= PAD, scores(0), NEG)
        s_scr[0] = s0

        def pass1(c, mx):
            s = scores(c)
            s_scr[c] = s
            return jnp.maximum(mx, fold(s))

        mx = lax.fori_loop(1, nchunk, pass1, fold(s0))
        m_old = m_scr[h]
        m_new = jnp.maximum(m_old, jnp.max(mx, axis=0, keepdims=True))

        def pass2(c, acc):
            p = jnp.exp(s_scr[c] - m_new).astype(BF16)
            return acc + _dot(vT_ref[0, g, c], p)

        acc = lax.fori_loop(0, nchunk, pass2, jnp.zeros((V_ROWS, tq), F32))
        acc_scr[h] = acc_scr[h] * jnp.exp(m_old - m_new) + acc
        m_scr[h] = m_new

    @pl.when(j == nkb - 1)
    def _():
        for h in range(A_HEADS):
            a = acc_scr[h]
            o_ref[0, h * HEAD_DIM:(h + 1) * HEAD_DIM, :] = (
                a[:HEAD_DIM] / a[HEAD_DIM:HEAD_DIM + 1]).astype(o_ref.dtype)


def _attn_a(qT, k4, vT5):
    bsz, _, lp = qT.shape
    tq = 384 if lp % 384 == 0 else LANES
    nkb = 3 if lp % (3 * LANES) == 0 else 1
    tk = lp // nkb
    nchunk = tk // LANES
    kernel = functools.partial(_attn_a_kernel, nchunk=nchunk, nkb=nkb, tq=tq)
    return pl.pallas_call(
        kernel, grid=(bsz, lp // tq, nkb),
        in_specs=[
            pl.BlockSpec((1, A_Q, tq), lambda b, i, j: (b, 0, i)),
            pl.BlockSpec((1, A_KV_HEADS, tk, HEAD_DIM), lambda b, i, j: (b, 0, j, 0)),
            pl.BlockSpec((1, A_KV_HEADS, nchunk, V_ROWS, LANES), lambda b, i, j: (b, 0, j, 0, 0)),
        ],
        out_specs=pl.BlockSpec((1, A_Q, tq), lambda b, i, j: (b, 0, i)),
        out_shape=jax.ShapeDtypeStruct((bsz, A_Q, lp), BF16),
        scratch_shapes=[
            pltpu.VMEM((A_HEADS, 1, tq), F32),
            pltpu.VMEM((A_HEADS, V_ROWS, tq), F32),
            pltpu.VMEM((nchunk, LANES, tq), F32),
        ],
        compiler_params=_cparams(("parallel", "parallel", "arbitrary")),
    )(qT, k4, vT5)


def _attn_c_kernel(sink_ref, qT_ref, k0, k1, k2, k3, v0, v1, v2, v3, o_ref, *, lp):
    j = pl.program_id(1)
    k_refs, v_refs = (k0, k1, k2, k3), (v0, v1, v2, v3)
    row = lax.broadcasted_iota(jnp.int32, (LANES, LANES), 0)
    pq = lax.broadcasted_iota(jnp.int32, (LANES, LANES), 1) + j * LANES
    biases = []
    for slot in range(4):
        if slot == 0:
            bias = jnp.where(row >= PAD, 0.0, NEG)
        else:
            pk = row + (j + slot - 2) * LANES
            in_window = jnp.where(jnp.abs(pq - pk) <= WINDOW, 0.0, NEG)
            bias = jnp.where(pk >= FRONT, jnp.where(pk < lp, in_window, NEG), NEG)
        biases.append(bias.astype(F32))
    group = C_HEADS // C_KV_HEADS
    for h in range(C_HEADS):
        g = h // group
        qh = qT_ref[0, h * HEAD_DIM:(h + 1) * HEAD_DIM, :]
        sink = sink_ref[h]
        ss = [_dot(k_refs[t][0, g], qh) + biases[t] for t in range(4)]
        m = jnp.maximum(jnp.maximum(ss[0], ss[1]), jnp.maximum(ss[2], ss[3]))
        m = jnp.maximum(jnp.max(m, axis=0, keepdims=True), sink)
        acc = jnp.zeros((V_ROWS, LANES), F32)
        for t in range(4):
            acc = acc + _dot(v_refs[t][0, g], jnp.exp(ss[t] - m).astype(BF16))
        denom = acc[HEAD_DIM:HEAD_DIM + 1] + jnp.exp(sink - m)
        o_ref[0, h * HEAD_DIM:(h + 1) * HEAD_DIM, :] = (acc[:HEAD_DIM] / denom).astype(o_ref.dtype)


def _attn_c(sink, qT, k4, vT4):
    bsz, _, lp = qT.shape
    nb = lp // LANES
    kernel = functools.partial(_attn_c_kernel, lp=lp)

    def kspec(fn):
        return pl.BlockSpec((1, C_KV_HEADS, LANES, HEAD_DIM), lambda b, j: (b, 0, fn(j), 0))

    def vspec(fn):
        return pl.BlockSpec((1, C_KV_HEADS, V_ROWS, LANES), lambda b, j: (b, 0, 0, fn(j)))

    fns = (lambda j: 0, lambda j: jnp.maximum(j - 1, 0), lambda j: j, lambda j: jnp.minimum(j + 1, nb - 1))
    return pl.pallas_call(
        kernel, grid=(bsz, nb),
        in_specs=[pl.BlockSpec(memory_space=pltpu.SMEM),
                  pl.BlockSpec((1, C_Q, LANES), lambda b, j: (b, 0, j))]
                 + [kspec(f) for f in fns] + [vspec(f) for f in fns],
        out_specs=pl.BlockSpec((1, C_Q, LANES), lambda b, j: (b, 0, j)),
        out_shape=jax.ShapeDtypeStruct((bsz, C_Q, lp), BF16),
        compiler_params=_cparams(("parallel", "parallel")),
    )(sink, qT, k4, k4, k4, k4, vT4, vT4, vT4, vT4)


def _fft1_kernel(x_ref, c1_ref, s1_ref, o_ref):
    x = x_ref[0]
    tn = x.shape[1]
    lane = lax.broadcasted_iota(jnp.int32, x.shape, 1)
    is_re = (lane & (2 * B_W - 1)) < B_W
    xs = jnp.where(is_re, pltpu.roll(x, tn - B_W, 1), -pltpu.roll(x, B_W, 1))
    o_ref[0] = _dot(c1_ref[...], x.astype(BF16)) + _dot(s1_ref[...], xs.astype(BF16))


def _fft2_kernel(a_ref, tc_ref, ts_ref, c2_ref, s2_ref, bias_ref, o_ref):
    a = a_ref[0, 0]
    tc, ts = _wide(tc_ref[0], B_W), _wide(ts_ref[0], B_W)
    are, aim = a[:, :B_W], a[:, B_W:]
    bre = (are * tc + aim * ts).astype(BF16)
    bim = (aim * tc - are * ts).astype(BF16)
    o_ref[0] = _dot(c2_ref[...], bre) + _dot(s2_ref[...], bim) + bias_ref[...]


def _dft_tables(l2, l2p, l2o):
    l1 = FFT_L1
    length = l1 * l2
    n1 = np.arange(l1)
    ang1 = 2.0 * np.pi * ((n1[:, None] * n1[None, :]) % l1) / l1
    c1, s1 = np.cos(ang1), np.sin(ang1)
    n2 = np.arange(l2)
    angt = 2.0 * np.pi * (n1[:, None] * n2[None, :]) / length
    tc = np.zeros((l1, l2p, LANES), np.float32)
    ts = np.zeros((l1, l2p, LANES), np.float32)
    tc[:, :l2, :] = np.cos(angt)[:, :, None]
    ts[:, :l2, :] = np.sin(angt)[:, :, None]
    ang2 = 2.0 * np.pi * ((n2[:, None] * n2[None, :]) % l2) / l2
    scale = 1.0 / math.sqrt(B_DIM * length)
    c2 = np.zeros((l2o, l2p), np.float32)
    s2 = np.zeros((l2o, l2p), np.float32)
    c2[:l2, :l2] = np.cos(ang2) * scale
    s2[:l2, :l2] = np.sin(ang2) * scale
    as_bf = lambda t: jnp.asarray(t, F32).astype(BF16)
    return as_bf(c1), as_bf(s1), jnp.asarray(tc), jnp.asarray(ts), as_bf(c2), as_bf(s2)


def _mixer_b(gseq, bias):
    bsz, length, _ = gseq.shape
    l1 = FFT_L1
    l2 = length // l1
    l2p, l2o = _round_up(l2, LANES), _round_up(l2, 8)
    c1, s1, tc, ts, c2, s2 = _dft_tables(l2, l2p, l2o)
    x = jnp.pad(gseq.reshape(bsz, l1, l2, 2 * B_W), ((0, 0), (0, 0), (0, l2p - l2), (0, 0)))
    x = x.reshape(bsz, l1, l2p * 2 * B_W)
    ncol = l2p * 2 * B_W
    tn = 2 * B_W * 48 if l2p % 48 == 0 else 2 * B_W
    a = pl.pallas_call(
        _fft1_kernel, grid=(bsz, ncol // tn),
        in_specs=[pl.BlockSpec((1, l1, tn), lambda b, i: (b, 0, i)),
                  pl.BlockSpec((l1, l1), lambda b, i: (0, 0)),
                  pl.BlockSpec((l1, l1), lambda b, i: (0, 0))],
        out_specs=pl.BlockSpec((1, l1, tn), lambda b, i: (b, 0, i)),
        out_shape=jax.ShapeDtypeStruct((bsz, l1, ncol), F32),
        compiler_params=_cparams(("parallel", "parallel")),
    )(x, c1, s1)
    a = a.reshape(bsz, l1, l2p, 2 * B_W)
    y = pl.pallas_call(
        _fft2_kernel, grid=(bsz, l1),
        in_specs=[pl.BlockSpec((1, 1, l2p, 2 * B_W), lambda b, k: (b, k, 0, 0)),
                  pl.BlockSpec((1, l2p, LANES), lambda b, k: (k, 0, 0)),
                  pl.BlockSpec((1, l2p, LANES), lambda b, k: (k, 0, 0)),
                  pl.BlockSpec((l2o, l2p), lambda b, k: (0, 0)),
                  pl.BlockSpec((l2o, l2p), lambda b, k: (0, 0)),
                  pl.BlockSpec((1, B_W), lambda b, k: (0, 0))],
        out_specs=pl.BlockSpec((1, l2o, B_W), lambda b, k: (b, 0, k)),
        out_shape=jax.ShapeDtypeStruct((bsz, l2o, l1 * B_W), F32),
        compiler_params=_cparams(("parallel", "parallel")),
    )(a, tc, ts, c2, s2, bias)
    return y[:, :l2].reshape(bsz, length, B_W)


def _seq_position(rows, segments):
    pos = jnp.full(rows.shape, -1.0, F32)
    seqlen = jnp.full(rows.shape, 1.0, F32)
    for start, bsz, lp in segments:
        rel = rows - float(start)
        q = jnp.floor((rel + 0.5) * (1.0 / lp))
        inside = jnp.where(rel >= 0.0, jnp.where(rel < float(bsz * lp), 1.0, 0.0), 0.0) > 0.5
        pos = jnp.where(inside, rel - q * lp, pos)
        seqlen = jnp.where(inside, float(lp), seqlen)
    return pos, seqlen


def _sigmoid(x):
    return 1.0 / (1.0 + jnp.exp(-x))


def _d_prep_kernel(u_ref, up_ref, un_ref, mup_ref, mun_ref, w2_ref, a2_ref, gup_ref, w0_ref, a0_ref,
                   kk_ref, ka_ref, rk_ref, bs_ref,
                   r_ref, v_ref, a_ref, lwf_ref, lwb_ref, kdf_ref, kdb_ref, bf_ref, bb_ref,
                   bonus_ref, g_ref, *, tm, segments):
    i = pl.program_id(0)
    u = u_ref[...]
    rows = (lax.broadcasted_iota(jnp.int32, (tm, 1), 0) + i * tm).astype(F32)
    pos, seqlen = _seq_position(rows, segments)
    local = lax.broadcasted_iota(jnp.int32, (tm, 1), 0)
    u_prev = jnp.where(local == 0, up_ref[7:8, :], pltpu.roll(u, 1, 0))
    u_next = jnp.where(local == tm - 1, un_ref[0:1, :], pltpu.roll(u, tm - 1, 0))
    u_prev = jnp.where(pos == float(PAD), 0.0, u_prev)
    u_next = jnp.where(pos == seqlen - 1.0, 0.0, u_next)
    u = u + mup_ref[...] * (u_prev - u) + mun_ref[...] * (u_next - u)
    valid = jnp.where(pos >= float(PAD), 1.0, 0.0)

    w = D_WIDTH
    r, k, v = u[:, :w], u[:, w:2 * w], u[:, 2 * w:3 * w]
    c0 = 3 * w
    dec = _dot(jnp.tanh(u[:, c0:c0 + 2 * DECAY_RANK]).astype(BF16), w2_ref[...]) + w0_ref[...]
    c0 += 2 * DECAY_RANK
    icl = _dot(u[:, c0:c0 + 2 * ICLR_RANK].astype(BF16), a2_ref[...]) + a0_ref[...]
    c0 += 2 * ICLR_RANK
    g_ref[...] = _dot(_sigmoid(u[:, c0:c0 + GATE_RANK]).astype(BF16), gup_ref[...])

    bs = bs_ref[...]

    def head_sum(x):
        return jnp.concatenate(
            [_group_mean(x[:, s * HGROUP:(s + 1) * HGROUP], bs) for s in range(w // HGROUP)], axis=1
        ) * float(HEAD_DIM)

    kk = k * kk_ref[...]
    kk = kk * lax.rsqrt(jnp.maximum(head_sum(kk * kk), 1e-24))
    r_ref[...] = r
    v_ref[...] = v
    a_ref[...] = -kk * valid
    bonus = jnp.zeros_like(r)
    for d, (lw_ref, kd_ref, b_ref) in enumerate(((lwf_ref, kdf_ref, bf_ref), (lwb_ref, kdb_ref, bb_ref))):
        x = -dec[:, d * w:(d + 1) * w]
        softplus = jnp.maximum(x, 0.0) + jnp.log(1.0 + jnp.exp(-jnp.abs(x)))
        lw_ref[...] = -jnp.exp(-softplus - 0.5)
        gate = _sigmoid(icl[:, d * w:(d + 1) * w])
        kd = k * (1.0 + (gate - 1.0) * ka_ref[...])
        kd_ref[...] = kd * valid
        b_ref[...] = kk * gate * valid
        bonus = bonus + head_sum(r * kd * rk_ref[...]) * v
    bonus_ref[...] = bonus


def _scan_kernel(fwd_blk, bwd_blk, first,
                 rf, vf, af, lwf, kdf, bf_, rb, vb, ab, lwb, kdb, bb_,
                 of_ref, ob_ref, s_scr):
    del fwd_blk, bwd_blk
    step = pl.program_id(0)
    c = SCAN_CHUNK

    @pl.when(first[step] == 1)
    def _():
        s_scr[...] = jnp.zeros(s_scr.shape, F32)

    t_sq = lax.broadcasted_iota(jnp.int32, (c, c), 0)
    s_sq = lax.broadcasted_iota(jnp.int32, (c, c), 1)
    t_cat = lax.broadcasted_iota(jnp.int32, (c, HGROUP), 0)
    s_cat = lax.broadcasted_iota(jnp.int32, (c, HGROUP), 1) & (c - 1)
    bd_row = lax.broadcasted_iota(jnp.int32, (HGROUP, HGROUP), 0) >> 6
    bd_col = lax.broadcasted_iota(jnp.int32, (HGROUP, HGROUP), 1) >> 6
    bd_mask = bd_row == bd_col
    eye_cat = jnp.where(t_cat == s_cat, 1.0, 0.0).astype(F32)

    def block_diag(x):
        return jnp.where(bd_mask, jnp.concatenate([x] * (HGROUP // c), axis=0), 0.0)

    dirs = ((0, rf, vf, af, lwf, kdf, bf_, of_ref), (1, rb, vb, ab, lwb, kdb, bb_, ob_ref))
    for d, r_ref, v_ref, a_ref, lw_ref, kd_ref, b_ref, o_ref in dirs:
        rev = d == 1
        earlier_sq = (s_sq >= t_sq) if rev else (s_sq <= t_sq)
        tri = jnp.where(earlier_sq, 1.0, 0.0).astype(BF16)
        strict = (s_cat > t_cat) if rev else (s_cat < t_cat)
        incl = (s_cat >= t_cat) if rev else (s_cat <= t_cat)
        lw = lw_ref[...]
        h1 = lw.astype(BF16)
        r1 = lw - h1.astype(F32)
        h2 = r1.astype(BF16)
        h3 = (r1 - h2.astype(F32)).astype(BF16)
        cum = _dot(tri, h1) + _dot(tri, h2) + _dot(tri, h3)
        e_in = jnp.exp(cum)
        e_ex = jnp.exp(cum - lw)
        e_neg = jnp.exp(-cum)
        a_t = a_ref[...] * e_ex
        r_t = r_ref[...] * e_in
        b_t = b_ref[...] * e_neg
        k_t = kd_ref[...] * e_neg
        w_tot = e_in[0:1, :] if rev else e_in[c - 1:c, :]
        v_all = v_ref[...]
        for g in range(D_WIDTH // HGROUP):
            cs = slice(g * HGROUP, (g + 1) * HGROUP)
            at, rt, bt, kt, vc = a_t[:, cs], r_t[:, cs], b_t[:, cs], k_t[:, cs], v_all[:, cs]
            lhs = jnp.concatenate([at, rt], axis=0).astype(BF16)
            m_b = _dot_nt(lhs, block_diag(bt).astype(BF16))
            m_k = _dot_nt(lhs, block_diag(kt).astype(BF16))
            a_ab = jnp.where(strict, m_b[:c], 0.0)
            a_rb = jnp.where(incl, m_b[c:], 0.0)
            a_ak = jnp.where(strict, m_k[:c], 0.0)
            a_rk = jnp.where(incl, m_k[c:], 0.0)
            lvl = jnp.where((t_cat >> 1) == (s_cat >> 1), a_ab, 0.0)
            tinv = eye_cat + lvl
            m = 2
            while m < c:
                sh = int(math.log2(m))
                a_m = jnp.where((t_cat >> (sh + 1)) == (s_cat >> (sh + 1)),
                                jnp.where((t_cat >> sh) != (s_cat >> sh), a_ab, 0.0), 0.0).astype(BF16)
                x = _dot(a_m, block_diag(tinv).astype(BF16))
                tinv = tinv + _dot(tinv.astype(BF16), block_diag(x).astype(BF16))
                m *= 2
            state = s_scr[d, g]
            st_b = state.astype(BF16)
            vbd = block_diag(vc).astype(BF16)
            rhs = _dot_nt(at.astype(BF16), st_b) + _dot(a_ak.astype(BF16), vbd)
            u = _dot(tinv.astype(BF16), block_diag(rhs).astype(BF16))
            out = (_dot_nt(rt.astype(BF16), st_b) + _dot(a_rb.astype(BF16), block_diag(u).astype(BF16))
                   + _dot(a_rk.astype(BF16), vbd))
            o_ref[:, cs] = out
            upd = _dot_tn(jnp.concatenate([u, vc], axis=0).astype(BF16),
                          jnp.concatenate([bt, kt], axis=0).astype(BF16))
            s_scr[d, g] = jnp.where(bd_mask, (state + upd) * w_tot[:, cs], 0.0)


def _d_post_kernel(of_ref, ob_ref, bonus_ref, g_ref, lng_ref, lnb_ref, bs_ref, o_ref):
    o = of_ref[...] + ob_ref[...]
    bs = bs_ref[...]

    def head_mean(x):
        return jnp.concatenate(
            [_group_mean(x[:, s * HGROUP:(s + 1) * HGROUP], bs) for s in range(D_WIDTH // HGROUP)], axis=1)

    cen = o - head_mean(o)
    var = head_mean(cen * cen)
    y = cen * lax.rsqrt(var + LNX_EPS) * lng_ref[...] + lnb_ref[...] + bonus_ref[...]
    o_ref[...] = (y * g_ref[...]).astype(o_ref.dtype)


def _scan_tables(segments):
    fwd, bwd, first = [], [], []
    for start, bsz, lp in segments:
        nc = lp // SCAN_CHUNK
        for b in range(bsz):
            base = (start + b * lp) // SCAN_CHUNK
            for ci in range(nc):
                fwd.append(base + ci)
                bwd.append(base + nc - 1 - ci)
                first.append(1 if ci == 0 else 0)
    as_i32 = lambda t: jnp.asarray(np.asarray(t, np.int32))
    return as_i32(fwd), as_i32(bwd), as_i32(first)


def _scan(segments, n_rows, r, v, a, lwf, lwb, kdf, kdb, bf_, bb_):
    fwd, bwd, first = _scan_tables(segments)
    nsteps = fwd.shape[0]
    blk = (SCAN_CHUNK, D_WIDTH)
    fspec = pl.BlockSpec(blk, lambda s, fw, bw, fi: (fw[s], 0))
    bspec = pl.BlockSpec(blk, lambda s, fw, bw, fi: (bw[s], 0))
    grid_spec = pltpu.PrefetchScalarGridSpec(
        num_scalar_prefetch=3, grid=(nsteps,),
        in_specs=[fspec] * 6 + [bspec] * 6,
        out_specs=[fspec, bspec],
        scratch_shapes=[pltpu.VMEM((2, D_WIDTH // HGROUP, HGROUP, HGROUP), F32)],
    )
    return pl.pallas_call(
        _scan_kernel, grid_spec=grid_spec,
        out_shape=[jax.ShapeDtypeStruct((n_rows, D_WIDTH), F32)] * 2,
        compiler_params=_cparams(("arbitrary",)),
    )(fwd, bwd, first, r, v, a, lwf, kdf, bf_, r, v, a, lwb, kdb, bb_)


def _rope_tables(segments, n_rows, layer_kind):
    outs = []
    for start, bsz, lp in segments:
        p = jnp.arange(lp)
        d = jnp.arange(HEAD_DIM)
        if layer_kind == "axial":
            t = p - FRONT
            row = jnp.where(t >= 0, t // GRID_W, jnp.where(p >= PAD, t, 0)).astype(F32)
            col = jnp.where(t >= 0, t % GRID_W, jnp.where(p >= PAD, t, 0)).astype(F32)
            half = HEAD_DIM // 2
            inv = A_THETA ** (-jnp.arange(0, half, 2, dtype=F32) / half)
            ang = jnp.concatenate([row[:, None] * inv] * 2 + [col[:, None] * inv] * 2, axis=1)
            first = (d % 32) < 16
            cos = jnp.cos(ang)
            sa = jnp.where(first[None, :], -jnp.sin(ang), 0.0)
            sb = jnp.where(first[None, :], 0.0, jnp.sin(ang))
        else:
            pos = jnp.maximum(p - PAD, 0).astype(F32)
            inv = ROPE_THETA ** (-jnp.arange(0, ROPE_DIMS, 2, dtype=F32) / ROPE_DIMS)
            ang8 = pos[:, None] * inv
            ang = jnp.concatenate([ang8, ang8] + [jnp.zeros_like(ang8)] * 6, axis=1)
            cos = jnp.where((d < ROPE_DIMS)[None, :], jnp.cos(ang), 1.0)
            sa = jnp.where((d < 8)[None, :], -jnp.sin(ang), 0.0)
            sb = jnp.where(((d >= 8) & (d < 16))[None, :], jnp.sin(ang), 0.0)
        tabs = [jnp.tile(jnp.concatenate([t_, t_], axis=1), (bsz, 1)) for t_ in (cos, sa, sb)]
        outs.append(tabs)
    tail = n_rows - sum(b * lp for _, b, lp in segments)
    res = []
    for idx in range(3):
        parts = [o[idx] for o in outs] + [jnp.zeros((tail, LANES), F32)]
        res.append(jnp.concatenate(parts, axis=0))
    return res


def _seq_view(flat, seg, width):
    start, bsz, lp = seg
    return flat[start:start + bsz * lp].reshape(bsz, lp, width)


def _kv_layouts(q, k, v, seg, n_kv, chunked):
    bsz, lp = seg[1], seg[2]
    qT = jnp.transpose(_seq_view(q, seg, q.shape[1]), (0, 2, 1))
    k4 = jnp.transpose(_seq_view(k, seg, k.shape[1]).reshape(bsz, lp, n_kv, HEAD_DIM), (0, 2, 1, 3))
    vT = jnp.transpose(_seq_view(v, seg, v.shape[1]).reshape(bsz, lp, n_kv, HEAD_DIM), (0, 2, 3, 1))
    aug = jnp.concatenate([vT, jnp.ones((bsz, n_kv, 1, lp), BF16),
                           jnp.zeros((bsz, n_kv, V_ROWS - HEAD_DIM - 1, lp), BF16)], axis=2)
    if chunked:
        aug = jnp.transpose(aug.reshape(bsz, n_kv, V_ROWS, lp // LANES, LANES), (0, 1, 3, 2, 4))
    return qT, k4, aug


def _to_flat(parts, n_rows, width, dtype):
    used = sum(p.shape[0] for p in parts)
    return jnp.concatenate(parts + [jnp.zeros((n_rows - used, width), dtype)], axis=0)


def _block_avg(n, group):
    idx = np.arange(n) // group
    return jnp.asarray((idx[:, None] == idx[None, :]).astype(np.float32) / group).astype(BF16)


def _forward(xs, p):
    segments = []
    start = 0
    for x in xs:
        bsz, s, _ = x.shape
        lp = s + FRONT
        segments.append((start, bsz, lp))
        start += bsz * lp
    n_used = start
    tm = 512 if n_used >= 4096 else LANES
    n_rows = _round_up(n_used, tm)

    meta = p['meta_tokens'].astype(F32)
    parts = []
    for x in xs:
        bsz = x.shape[0]
        lead = jnp.concatenate([jnp.zeros((PAD, D_MODEL), F32), meta], axis=0)
        parts.append(jnp.concatenate([jnp.broadcast_to(lead[None], (bsz, FRONT, D_MODEL)), x], axis=1)
                     .reshape(-1, D_MODEL))
    h = _to_flat(parts, n_rows, D_MODEL, F32)

    bs = _block_avg(HGROUP, HEAD_DIM)
    row = lambda t: t.reshape(1, -1).astype(F32)

    depth = p['pre_mix_g'].shape[0]
    for i in range(depth):
        if i % 2 == 0:
            e = i // 2
            cos, sa, sb = _rope_tables(segments, n_rows, "axial")
            ch = np.arange(B_DIM)
            ang = 2.0 * np.pi * ((ch[:, None] * ch[None, :]) % B_DIM) / B_DIM
            cc, sc = jnp.asarray(np.cos(ang), F32), jnp.asarray(np.sin(ang), F32)
            wl = p['b_w'][e].astype(F32)
            hp = lax.Precision.HIGHEST
            pmat = jnp.einsum('cd,gde->gce', cc, wl, precision=hp)
            qmat = -jnp.einsum('cd,gde->gce', sc, wl, precision=hp)
            bd = lambda m: jax.scipy.linalg.block_diag(*[m[g] for g in range(B_GROUPS)])
            pq_base = jnp.concatenate([bd(pmat), bd(qmat)], axis=1)
            q, k, v, gf = _in_even_call(h, cos, sa, sb, p, i, e, bs, pq_base, n_rows, tm)
            ya_parts, yb_parts = [], []
            for seg in segments:
                _, bsz, lp = seg
                qT, k4, vT5 = _kv_layouts(q, k, v, seg, A_KV_HEADS, True)
                oT = _attn_a(qT, k4, vT5)
                ya_parts.append(jnp.transpose(oT, (0, 2, 1)).reshape(bsz * lp, A_Q))
                gseq = _seq_view(gf, seg, 2 * B_W)[:, PAD:]
                yb = _mixer_b(gseq, row(p['b_b'][e]))
                yb_parts.append(jnp.pad(yb, ((0, 0), (PAD, 0), (0, 0))).reshape(bsz * lp, B_W).astype(BF16))
            ya = _to_flat(ya_parts, n_rows, A_Q, BF16)
            yb = _to_flat(yb_parts, n_rows, B_W, BF16)
            w_out = p['even_w_out'][e].astype(BF16)
            wa, wb = w_out[:A_Q], w_out[A_Q:]
        else:
            o = i // 2
            cos, sa, sb = _rope_tables(segments, n_rows, "partial")
            q, k, v, u = _row_call(
                _in_odd_kernel, n_rows, tm, [h, cos, sa, sb],
                [row(p['pre_mix_g'][i]), p['odd_w_in'][o].astype(BF16)],
                [(C_Q, BF16), (C_KV, BF16), (C_KV, BF16), (D_IN, F32)])
            yc_parts = []
            for seg in segments:
                _, bsz, lp = seg
                qT, k4, vT4 = _kv_layouts(q, k, v, seg, C_KV_HEADS, False)
                oT = _attn_c(p['c_sink'][o].astype(F32), qT, k4, vT4)
                yc_parts.append(jnp.transpose(oT, (0, 2, 1)).reshape(bsz * lp, C_Q))
            ya = _to_flat(yc_parts, n_rows, C_Q, BF16)
            yb = _mixer_d(u, p, o, bs, segments, n_rows, tm)
            w_out = p['odd_w_out'][o].astype(BF16)
            wa, wb = w_out[:C_Q], w_out[C_Q:]
        (h,) = _row_call(_out_proj_kernel, n_rows, tm, [h, ya, yb],
                         [wa, wb, row(p['post_mix_g'][i])], [(D_MODEL, F32)])
        (h,) = _row_call(
            functools.partial(_ffn_kernel, chunk=256), n_rows, tm, [h],
            [row(p['pre_ffn_g'][i]), p['ffn_w_gate'][i].astype(BF16), p['ffn_w_up'][i].astype(BF16),
             p['ffn_w_down'][i].astype(BF16), row(p['post_ffn_g'][i])],
            [(D_MODEL, F32)], scratch=[pltpu.VMEM((tm, D_MODEL), F32)])

    outs = []
    for seg in segments:
        outs.append(_seq_view(h, seg, D_MODEL)[:, FRONT:])
    return tuple(outs)


def _in_even_call(h, cos, sa, sb, p, i, e, bs, pq_base, n_rows, tm):
    row = lambda t: t.reshape(1, -1).astype(F32)
    pq = pq_base.astype(BF16)
    grid = (n_rows // tm,)
    consts = [row(p['pre_mix_g'][i]), p['even_w_in'][e].astype(BF16),
              row(jnp.tile(p['a_q_gain'][e], A_HEADS) * HEAD_DIM ** -0.5),
              row(jnp.tile(p['a_k_gain'][e], A_KV_HEADS)), row(p['b_norm_g'][e]), bs, pq]
    in_specs = [pl.BlockSpec((tm, D_MODEL), lambda r: (r, 0))]
    in_specs += [pl.BlockSpec(a.shape, lambda r, nd=a.ndim: (0,) * nd) for a in consts]
    in_specs += [pl.BlockSpec((tm, LANES), lambda r: (r, 0))] * 3
    outs = [(A_Q, BF16), (A_KV, BF16), (A_KV, BF16), (2 * B_W, F32)]
    return pl.pallas_call(
        _in_even_kernel, grid=grid, in_specs=in_specs,
        out_specs=[pl.BlockSpec((tm, n), lambda r: (r, 0)) for n, _ in outs],
        out_shape=[jax.ShapeDtypeStruct((n_rows, n), dt) for n, dt in outs],
        compiler_params=_cparams(("parallel",)),
    )(h, *consts, cos, sa, sb)


def _mixer_d(u, p, o, bs, segments, n_rows, tm):
    row = lambda t: t.reshape(1, -1).astype(F32)
    w = D_WIDTH
    zeros = lambda r: jnp.zeros((r, w), F32)
    w2 = jnp.concatenate([jnp.concatenate([p['d_w_up'][o][0], zeros(DECAY_RANK)], axis=1),
                          jnp.concatenate([zeros(DECAY_RANK), p['d_w_up'][o][1]], axis=1)], axis=0)
    a2 = jnp.concatenate([jnp.concatenate([p['d_a_up'][o][0], zeros(ICLR_RANK)], axis=1),
                          jnp.concatenate([zeros(ICLR_RANK), p['d_a_up'][o][1]], axis=1)], axis=0)
    consts = [row(p['d_mu_prev'][o]), row(p['d_mu_next'][o]), w2.astype(BF16), a2.astype(BF16),
              p['d_g_up'][o].astype(BF16), row(p['d_w0'][o]), row(p['d_a0'][o]),
              row(p['d_k_k'][o]), row(p['d_k_a'][o]), row(p['d_r_k'][o]), bs]
    nb8 = n_rows // 8
    t8 = tm // 8
    in_specs = [pl.BlockSpec((tm, D_IN), lambda i: (i, 0)),
                pl.BlockSpec((8, D_IN), lambda i: (jnp.maximum(i * t8 - 1, 0), 0)),
                pl.BlockSpec((8, D_IN), lambda i: (jnp.minimum((i + 1) * t8, nb8 - 1), 0))]
    in_specs += [pl.BlockSpec(a.shape, lambda i, nd=a.ndim: (0,) * nd) for a in consts]
    n_out = 11
    prep = pl.pallas_call(
        functools.partial(_d_prep_kernel, tm=tm, segments=tuple(segments)),
        grid=(n_rows // tm,), in_specs=in_specs,
        out_specs=[pl.BlockSpec((tm, w), lambda i: (i, 0))] * n_out,
        out_shape=[jax.ShapeDtypeStruct((n_rows, w), F32)] * n_out,
        compiler_params=_cparams(("parallel",)),
    )(u, u, u, *consts)
    r, v, a, lwf, lwb, kdf, kdb, bf_, bb_, bonus, g = prep
    of, ob = _scan(segments, n_rows, r, v, a, lwf, lwb, kdf, kdb, bf_, bb_)
    (yd,) = _row_call(_d_post_kernel, n_rows, tm, [of, ob, bonus, g],
                      [row(p['d_ln_g'][o]), row(p['d_ln_b'][o]), bs], [(w, BF16)])
    return yd


def kernel(x_prompt, x_sample, meta_tokens, pre_mix_g, post_mix_g, pre_ffn_g, post_ffn_g, even_w_in, even_w_out, a_q_gain, a_k_gain, b_norm_g, b_w, b_b, odd_w_in, odd_w_out, c_sink, d_mu_prev, d_mu_next, d_w0, d_w_up, d_a0, d_a_up, d_g_up, d_k_k, d_k_a, d_r_k, d_ln_g, d_ln_b, ffn_w_gate, ffn_w_up, ffn_w_down):
    params = dict(meta_tokens=meta_tokens, pre_mix_g=pre_mix_g, post_mix_g=post_mix_g,
                  pre_ffn_g=pre_ffn_g, post_ffn_g=post_ffn_g,
                  even_w_in=even_w_in, even_w_out=even_w_out, a_q_gain=a_q_gain, a_k_gain=a_k_gain,
                  b_norm_g=b_norm_g, b_w=b_w, b_b=b_b,
                  odd_w_in=odd_w_in, odd_w_out=odd_w_out, c_sink=c_sink,
                  d_mu_prev=d_mu_prev, d_mu_next=d_mu_next, d_w0=d_w0, d_w_up=d_w_up,
                  d_a0=d_a0, d_a_up=d_a_up, d_g_up=d_g_up, d_k_k=d_k_k, d_k_a=d_k_a, d_r_k=d_r_k,
                  d_ln_g=d_ln_g, d_ln_b=d_ln_b,
                  ffn_w_gate=ffn_w_gate, ffn_w_up=ffn_w_up, ffn_w_down=ffn_w_down)
    return _forward([x_prompt, x_sample], params)
```

```python
import functools
import math

import numpy as np
import jax
import jax.numpy as jnp
from jax import lax
from jax.experimental import pallas as pl
from jax.experimental.pallas import tpu as pltpu

F32 = jnp.float32
BF16 = jnp.bfloat16

D_MODEL = 1024
HEAD_DIM = 64
N_META = 16
GRID_W = 64
WINDOW = 128
RMS_EPS = 1e-6
A_HEADS, A_KV_HEADS, A_THETA = 12, 4, 10000.0
B_GROUPS, B_DIM = 4, 64
C_HEADS, C_KV_HEADS = 8, 2
ROPE_THETA = 500000.0
ROPE_DIMS = HEAD_DIM // 4
D_HEADS = 8
D_WIDTH = D_HEADS * HEAD_DIM
DECAY_RANK, ICLR_RANK, GATE_RANK = 64, 64, 128
LNX_EPS = 64e-5
D_FF = 2816
A_Q, A_KV, B_W = A_HEADS * HEAD_DIM, A_KV_HEADS * HEAD_DIM, B_GROUPS * B_DIM
EVEN_IN = A_Q + 2 * A_KV + B_W
C_Q, C_KV = C_HEADS * HEAD_DIM, C_KV_HEADS * HEAD_DIM
D_IN = 3 * D_WIDTH + 2 * DECAY_RANK + 2 * ICLR_RANK + GATE_RANK
ODD_IN = C_Q + 2 * C_KV + D_IN

LANES = 128
FRONT = 128
PAD = FRONT - N_META
NEG = -1e30
SCAN_CHUNK = 64
HGROUP = 256
FFT_L1 = 16
V_ROWS = 72
VMEM_LIMIT = 56 * 1024 * 1024
ATTN_UNROLL = 4
LOG2_E = math.log2(math.e)


def _round_up(x, m):
    return (x + m - 1) // m * m


def _cparams(sem):
    return pltpu.CompilerParams(dimension_semantics=sem, vmem_limit_bytes=VMEM_LIMIT)


def _dot(a, b):
    return jnp.dot(a, b, preferred_element_type=F32)


def _dot_nt(a, b):
    return lax.dot_general(a, b, (((1,), (1,)), ((), ())), preferred_element_type=F32)


def _dot_tn(a, b):
    return lax.dot_general(a, b, (((0,), (0,)), ((), ())), preferred_element_type=F32)


def _split2(x):
    hi = x.astype(BF16)
    lo = (x - hi.astype(F32)).astype(BF16)
    return hi, lo


def _group_mean(x, bs):
    hi, lo = _split2(x)
    return _dot(hi, bs) + _dot(lo, bs)


def _rope(x, cos, sin_a, sin_b, shift):
    n = x.shape[1]
    return x * cos + pltpu.roll(x, n - shift, 1) * sin_a + pltpu.roll(x, shift, 1) * sin_b


def _wide(t, n):
    return t if n == LANES else jnp.concatenate([t] * (n // LANES), axis=1)


def _rms_rows(x, g):
    ms = jnp.mean(x * x, axis=-1, keepdims=True)
    return x * lax.rsqrt(ms + RMS_EPS) * g


def _in_even_kernel(h_ref, g_ref, w_ref, qg_ref, kg_ref, bg_ref, bs_ref, pq_ref,
                    cos_ref, sa_ref, sb_ref, q_ref, k_ref, v_ref, gf_ref):
    hn = _rms_rows(h_ref[...], g_ref[...]).astype(BF16)
    proj = _dot(hn, w_ref[...])
    bs = bs_ref[...]
    cos, sa, sb = (_wide(r[...], HGROUP) for r in (cos_ref, sa_ref, sb_ref))

    def norm_rope(x, gain):
        xn = x * lax.rsqrt(_group_mean(x * x, bs) + RMS_EPS) * gain
        return _rope(xn, cos, sa, sb, 16).astype(BF16)

    for s in range(A_Q // HGROUP):
        cs = slice(s * HGROUP, (s + 1) * HGROUP)
        q_ref[:, cs] = norm_rope(proj[:, cs], qg_ref[:, cs])
    k_ref[...] = norm_rope(proj[:, A_Q:A_Q + A_KV], kg_ref[...])
    v_ref[...] = proj[:, A_Q + A_KV:A_Q + 2 * A_KV].astype(BF16)
    f = proj[:, A_Q + 2 * A_KV:]
    fn = f * lax.rsqrt(_group_mean(f * f, bs) + RMS_EPS) * bg_ref[...]
    gf_ref[...] = _dot(fn.astype(BF16), pq_ref[...])


def _in_odd_kernel(h_ref, cos_ref, sa_ref, sb_ref, g_ref, w_ref, q_ref, k_ref, v_ref, u_ref):
    hn = _rms_rows(h_ref[...], g_ref[...]).astype(BF16)
    proj = _dot(hn, w_ref[...])
    cos, sa, sb = cos_ref[...], sa_ref[...], sb_ref[...]
    cos2, sa2, sb2 = (_wide(t, HGROUP) for t in (cos, sa, sb))
    for s in range(C_Q // HGROUP):
        cs = slice(s * HGROUP, (s + 1) * HGROUP)
        q_ref[:, cs] = (_rope(proj[:, cs], cos2, sa2, sb2, 8) * HEAD_DIM ** -0.5).astype(BF16)
    k_ref[...] = _rope(proj[:, C_Q:C_Q + C_KV], cos, sa, sb, 8).astype(BF16)
    v_ref[...] = proj[:, C_Q + C_KV:C_Q + 2 * C_KV].astype(BF16)
    u_ref[...] = proj[:, C_Q + 2 * C_KV:]


def _out_proj_kernel(h_ref, ya_ref, yb_ref, wa_ref, wb_ref, g_ref, o_ref):
    mix = _dot(ya_ref[...], wa_ref[...]) + _dot(yb_ref[...], wb_ref[...])
    o_ref[...] = h_ref[...] + _rms_rows(mix, g_ref[...])


def _ffn_kernel(h_ref, g1_ref, wg_ref, wu_ref, wd_ref, g2_ref, o_ref, acc_ref, *, chunk):
    h = h_ref[...]
    hn = _rms_rows(h, g1_ref[...]).astype(BF16)
    for c in range(D_FF // chunk):
        cs = slice(c * chunk, (c + 1) * chunk)
        gate = _dot(hn, wg_ref[:, cs])
        up = _dot(hn, wu_ref[:, cs])
        act = (gate * (1.0 / (1.0 + jnp.exp(-gate))) * up).astype(BF16)
        part = _dot(act, wd_ref[cs, :])
        if c == 0:
            acc_ref[...] = part
        else:
            acc_ref[...] += part
    o_ref[...] = h + _rms_rows(acc_ref[...], g2_ref[...])


def _row_call(name, kernel, n_rows, tm, row_ins, const_ins, outs, scratch=()):
    grid = (n_rows // tm,)
    in_specs = [pl.BlockSpec((tm, a.shape[1]), lambda i: (i, 0)) for a in row_ins]
    in_specs += [pl.BlockSpec(a.shape, lambda i, nd=a.ndim: (0,) * nd) for a in const_ins]
    out_specs = [pl.BlockSpec((tm, n), lambda i: (i, 0)) for n, _ in outs]
    out_shape = [jax.ShapeDtypeStruct((n_rows, n), dt) for n, dt in outs]
    res = pl.pallas_call(
        kernel, grid=grid, in_specs=in_specs, out_specs=out_specs, out_shape=out_shape,
        scratch_shapes=list(scratch), compiler_params=_cparams(("parallel",)), name=name,
    )(*row_ins, *const_ins)
    return res


def _attn_a_kernel(qT_ref, k_ref, vT_ref, o_ref, m_scr, acc_scr, s_scr, *, nchunk, nkb, tq):
    j = pl.program_id(2)

    @pl.when(j == 0)
    def _():
        m_scr[...] = jnp.full(m_scr.shape, NEG, F32)
        acc_scr[...] = jnp.zeros(acc_scr.shape, F32)

    group = A_HEADS // A_KV_HEADS
    key_row = lax.broadcasted_iota(jnp.int32, (LANES, tq), 0) + j * (nchunk * LANES)

    def scores(h, c):
        kc = k_ref[0, h // group, pl.ds(pl.multiple_of(c * LANES, LANES), LANES), :]
        return _dot(kc, qT_ref[0, h * HEAD_DIM:(h + 1) * HEAD_DIM, :])

    def fold(s):
        return jnp.max(s.reshape(LANES // 8, 8, tq), axis=0)

    def first_scores(h):
        s0 = jnp.where(key_row >= PAD, scores(h, 0), NEG)
        s_scr[h % 2, 0] = s0
        return fold(s0)

    def next_scores(h, c, mx):
        s = scores(h, c)
        s_scr[h % 2, c] = s
        return jnp.maximum(mx, fold(s))

    def weighted_values(h, c, m_new):
        p = jnp.exp2(s_scr[h % 2, c] - m_new).astype(BF16)
        return _dot(vT_ref[0, h // group, c], p)

    mx = lax.fori_loop(1, nchunk, functools.partial(next_scores, 0), first_scores(0), unroll=ATTN_UNROLL)
    for h in range(A_HEADS):
        m_old = m_scr[h]
        m_new = jnp.maximum(m_old, jnp.max(mx, axis=0, keepdims=True))
        if h + 1 < A_HEADS:
            def body(c, carry, h=h, m_new=m_new):
                acc, mxn = carry
                return acc + weighted_values(h, c, m_new), next_scores(h + 1, c, mxn)

            mx0 = first_scores(h + 1)
            acc, mx = lax.fori_loop(1, nchunk, body, (weighted_values(h, 0, m_new), mx0),
                                    unroll=ATTN_UNROLL)
        else:
            acc = lax.fori_loop(0, nchunk, lambda c, acc: acc + weighted_values(h, c, m_new),
                                jnp.zeros((V_ROWS, tq), F32), unroll=ATTN_UNROLL)
        acc_scr[h] = acc_scr[h] * jnp.exp2(m_old - m_new) + acc
        m_scr[h] = m_new

    @pl.when(j == nkb - 1)
    def _():
        for h in range(A_HEADS):
            a = acc_scr[h]
            o_ref[0, h * HEAD_DIM:(h + 1) * HEAD_DIM, :] = (
                a[:HEAD_DIM] / a[HEAD_DIM:HEAD_DIM + 1]).astype(o_ref.dtype)


def _attn_a(qT, k4, vT5):
    bsz, _, lp = qT.shape
    tq = 384 if lp % 384 == 0 else LANES
    nkb = 3 if lp % (3 * LANES) == 0 else 1
    tk = lp // nkb
    nchunk = tk // LANES
    kernel = functools.partial(_attn_a_kernel, nchunk=nchunk, nkb=nkb, tq=tq)
    return pl.pallas_call(
        kernel, grid=(bsz, lp // tq, nkb),
        in_specs=[
            pl.BlockSpec((1, A_Q, tq), lambda b, i, j: (b, 0, i)),
            pl.BlockSpec((1, A_KV_HEADS, tk, HEAD_DIM), lambda b, i, j: (b, 0, j, 0)),
            pl.BlockSpec((1, A_KV_HEADS, nchunk, V_ROWS, LANES), lambda b, i, j: (b, 0, j, 0, 0)),
        ],
        out_specs=pl.BlockSpec((1, A_Q, tq), lambda b, i, j: (b, 0, i)),
        out_shape=jax.ShapeDtypeStruct((bsz, A_Q, lp), BF16),
        scratch_shapes=[
            pltpu.VMEM((A_HEADS, 1, tq), F32),
            pltpu.VMEM((A_HEADS, V_ROWS, tq), F32),
            pltpu.VMEM((2, nchunk, LANES, tq), F32),
        ],
        compiler_params=_cparams(("parallel", "parallel", "arbitrary")), name="attn_a",
    )(qT, k4, vT5)


def _attn_c_kernel(sink_ref, qT_ref, k0, k1, k2, k3, v0, v1, v2, v3, o_ref, *, lp):
    j = pl.program_id(1)
    k_refs, v_refs = (k0, k1, k2, k3), (v0, v1, v2, v3)
    row = lax.broadcasted_iota(jnp.int32, (LANES, LANES), 0)
    pq = lax.broadcasted_iota(jnp.int32, (LANES, LANES), 1) + j * LANES
    biases = []
    for slot in range(4):
        if slot == 0:
            bias = jnp.where(row >= PAD, 0.0, NEG)
        else:
            pk = row + (j + slot - 2) * LANES
            in_window = jnp.where(jnp.abs(pq - pk) <= WINDOW, 0.0, NEG)
            bias = jnp.where(pk >= FRONT, jnp.where(pk < lp, in_window, NEG), NEG)
        biases.append(bias.astype(F32))
    group = C_HEADS // C_KV_HEADS
    for h in range(C_HEADS):
        g = h // group
        qh = qT_ref[0, h * HEAD_DIM:(h + 1) * HEAD_DIM, :]
        sink = sink_ref[h]
        ss = [_dot(k_refs[t][0, g], qh) + biases[t] for t in range(4)]
        m = jnp.maximum(jnp.maximum(ss[0], ss[1]), jnp.maximum(ss[2], ss[3]))
        m = jnp.maximum(jnp.max(m, axis=0, keepdims=True), sink)
        acc = jnp.zeros((V_ROWS, LANES), F32)
        for t in range(4):
            acc = acc + _dot(v_refs[t][0, g], jnp.exp(ss[t] - m).astype(BF16))
        denom = acc[HEAD_DIM:HEAD_DIM + 1] + jnp.exp(sink - m)
        o_ref[0, h * HEAD_DIM:(h + 1) * HEAD_DIM, :] = (acc[:HEAD_DIM] / denom).astype(o_ref.dtype)


def _attn_c(sink, qT, k4, vT4):
    bsz, _, lp = qT.shape
    nb = lp // LANES
    kernel = functools.partial(_attn_c_kernel, lp=lp)

    def kspec(fn):
        return pl.BlockSpec((1, C_KV_HEADS, LANES, HEAD_DIM), lambda b, j: (b, 0, fn(j), 0))

    def vspec(fn):
        return pl.BlockSpec((1, C_KV_HEADS, V_ROWS, LANES), lambda b, j: (b, 0, 0, fn(j)))

    fns = (lambda j: 0, lambda j: jnp.maximum(j - 1, 0), lambda j: j, lambda j: jnp.minimum(j + 1, nb - 1))
    return pl.pallas_call(
        kernel, grid=(bsz, nb),
        in_specs=[pl.BlockSpec(memory_space=pltpu.SMEM),
                  pl.BlockSpec((1, C_Q, LANES), lambda b, j: (b, 0, j))]
                 + [kspec(f) for f in fns] + [vspec(f) for f in fns],
        out_specs=pl.BlockSpec((1, C_Q, LANES), lambda b, j: (b, 0, j)),
        out_shape=jax.ShapeDtypeStruct((bsz, C_Q, lp), BF16),
        compiler_params=_cparams(("parallel", "parallel")), name="attn_c",
    )(sink, qT, k4, k4, k4, k4, vT4, vT4, vT4, vT4)


def _fft1_kernel(x_ref, c1_ref, s1_ref, o_ref):
    x = x_ref[0]
    tn = x.shape[1]
    lane = lax.broadcasted_iota(jnp.int32, x.shape, 1)
    is_re = (lane & (2 * B_W - 1)) < B_W
    xs = jnp.where(is_re, pltpu.roll(x, tn - B_W, 1), -pltpu.roll(x, B_W, 1))
    o_ref[0] = _dot(c1_ref[...], x.astype(BF16)) + _dot(s1_ref[...], xs.astype(BF16))


def _fft2_kernel(a_ref, tc_ref, ts_ref, c2_ref, s2_ref, bias_ref, o_ref):
    a = a_ref[0, 0]
    tc, ts = _wide(tc_ref[0], B_W), _wide(ts_ref[0], B_W)
    are, aim = a[:, :B_W], a[:, B_W:]
    bre = (are * tc + aim * ts).astype(BF16)
    bim = (aim * tc - are * ts).astype(BF16)
    o_ref[0] = _dot(c2_ref[...], bre) + _dot(s2_ref[...], bim) + bias_ref[...]


def _dft_tables(l2, l2p, l2o):
    l1 = FFT_L1
    length = l1 * l2
    n1 = np.arange(l1)
    ang1 = 2.0 * np.pi * ((n1[:, None] * n1[None, :]) % l1) / l1
    c1, s1 = np.cos(ang1), np.sin(ang1)
    n2 = np.arange(l2)
    angt = 2.0 * np.pi * (n1[:, None] * n2[None, :]) / length
    tc = np.zeros((l1, l2p, LANES), np.float32)
    ts = np.zeros((l1, l2p, LANES), np.float32)
    tc[:, :l2, :] = np.cos(angt)[:, :, None]
    ts[:, :l2, :] = np.sin(angt)[:, :, None]
    ang2 = 2.0 * np.pi * ((n2[:, None] * n2[None, :]) % l2) / l2
    scale = 1.0 / math.sqrt(B_DIM * length)
    c2 = np.zeros((l2o, l2p), np.float32)
    s2 = np.zeros((l2o, l2p), np.float32)
    c2[:l2, :l2] = np.cos(ang2) * scale
    s2[:l2, :l2] = np.sin(ang2) * scale
    as_bf = lambda t: jnp.asarray(t, F32).astype(BF16)
    return as_bf(c1), as_bf(s1), jnp.asarray(tc), jnp.asarray(ts), as_bf(c2), as_bf(s2)


def _mixer_b(gseq, bias):
    bsz, length, _ = gseq.shape
    l1 = FFT_L1
    l2 = length // l1
    l2p, l2o = _round_up(l2, LANES), _round_up(l2, 8)
    c1, s1, tc, ts, c2, s2 = _dft_tables(l2, l2p, l2o)
    x = jnp.pad(gseq.reshape(bsz, l1, l2, 2 * B_W), ((0, 0), (0, 0), (0, l2p - l2), (0, 0)))
    x = x.reshape(bsz, l1, l2p * 2 * B_W)
    ncol = l2p * 2 * B_W
    tn = 2 * B_W * 48 if l2p % 48 == 0 else 2 * B_W
    a = pl.pallas_call(
        _fft1_kernel, grid=(bsz, ncol // tn),
        in_specs=[pl.BlockSpec((1, l1, tn), lambda b, i: (b, 0, i)),
                  pl.BlockSpec((l1, l1), lambda b, i: (0, 0)),
                  pl.BlockSpec((l1, l1), lambda b, i: (0, 0))],
        out_specs=pl.BlockSpec((1, l1, tn), lambda b, i: (b, 0, i)),
        out_shape=jax.ShapeDtypeStruct((bsz, l1, ncol), F32),
        compiler_params=_cparams(("parallel", "parallel")), name="fft1",
    )(x, c1, s1)
    a = a.reshape(bsz, l1, l2p, 2 * B_W)
    y = pl.pallas_call(
        _fft2_kernel, grid=(bsz, l1),
        in_specs=[pl.BlockSpec((1, 1, l2p, 2 * B_W), lambda b, k: (b, k, 0, 0)),
                  pl.BlockSpec((1, l2p, LANES), lambda b, k: (k, 0, 0)),
                  pl.BlockSpec((1, l2p, LANES), lambda b, k: (k, 0, 0)),
                  pl.BlockSpec((l2o, l2p), lambda b, k: (0, 0)),
                  pl.BlockSpec((l2o, l2p), lambda b, k: (0, 0)),
                  pl.BlockSpec((1, B_W), lambda b, k: (0, 0))],
        out_specs=pl.BlockSpec((1, l2o, B_W), lambda b, k: (b, 0, k)),
        out_shape=jax.ShapeDtypeStruct((bsz, l2o, l1 * B_W), F32),
        compiler_params=_cparams(("parallel", "parallel")), name="fft2",
    )(a, tc, ts, c2, s2, bias)
    return y[:, :l2].reshape(bsz, length, B_W)


def _seq_position(rows, segments):
    pos = jnp.full(rows.shape, -1.0, F32)
    seqlen = jnp.full(rows.shape, 1.0, F32)
    for start, bsz, lp in segments:
        rel = rows - float(start)
        q = jnp.floor((rel + 0.5) * (1.0 / lp))
        inside = jnp.where(rel >= 0.0, jnp.where(rel < float(bsz * lp), 1.0, 0.0), 0.0) > 0.5
        pos = jnp.where(inside, rel - q * lp, pos)
        seqlen = jnp.where(inside, float(lp), seqlen)
    return pos, seqlen


def _sigmoid(x):
    return 1.0 / (1.0 + jnp.exp(-x))


def _d_prep_kernel(u_ref, up_ref, un_ref, mup_ref, mun_ref, w2_ref, a2_ref, gup_ref, w0_ref, a0_ref,
                   kk_ref, ka_ref, rk_ref, bs_ref,
                   r_ref, v_ref, a_ref, lwf_ref, lwb_ref, kdf_ref, kdb_ref, bf_ref, bb_ref,
                   bonus_ref, g_ref, *, tm, segments):
    i = pl.program_id(0)
    u = u_ref[...]
    rows = (lax.broadcasted_iota(jnp.int32, (tm, 1), 0) + i * tm).astype(F32)
    pos, seqlen = _seq_position(rows, segments)
    local = lax.broadcasted_iota(jnp.int32, (tm, 1), 0)
    u_prev = jnp.where(local == 0, up_ref[7:8, :], pltpu.roll(u, 1, 0))
    u_next = jnp.where(local == tm - 1, un_ref[0:1, :], pltpu.roll(u, tm - 1, 0))
    u_prev = jnp.where(pos == float(PAD), 0.0, u_prev)
    u_next = jnp.where(pos == seqlen - 1.0, 0.0, u_next)
    u = u + mup_ref[...] * (u_prev - u) + mun_ref[...] * (u_next - u)
    valid = jnp.where(pos >= float(PAD), 1.0, 0.0)

    w = D_WIDTH
    r, k, v = u[:, :w], u[:, w:2 * w], u[:, 2 * w:3 * w]
    c0 = 3 * w
    dec = _dot(jnp.tanh(u[:, c0:c0 + 2 * DECAY_RANK]).astype(BF16), w2_ref[...]) + w0_ref[...]
    c0 += 2 * DECAY_RANK
    icl = _dot(u[:, c0:c0 + 2 * ICLR_RANK].astype(BF16), a2_ref[...]) + a0_ref[...]
    c0 += 2 * ICLR_RANK
    g_ref[...] = _dot(_sigmoid(u[:, c0:c0 + GATE_RANK]).astype(BF16), gup_ref[...])

    bs = bs_ref[...]

    def head_sum(x):
        return jnp.concatenate(
            [_group_mean(x[:, s * HGROUP:(s + 1) * HGROUP], bs) for s in range(w // HGROUP)], axis=1
        ) * float(HEAD_DIM)

    kk = k * kk_ref[...]
    kk = kk * lax.rsqrt(jnp.maximum(head_sum(kk * kk), 1e-24))
    r_ref[...] = r
    v_ref[...] = v
    a_ref[...] = -kk * valid
    bonus = jnp.zeros_like(r)
    for d, (lw_ref, kd_ref, b_ref) in enumerate(((lwf_ref, kdf_ref, bf_ref), (lwb_ref, kdb_ref, bb_ref))):
        x = -dec[:, d * w:(d + 1) * w]
        softplus = jnp.maximum(x, 0.0) + jnp.log(1.0 + jnp.exp(-jnp.abs(x)))
        lw_ref[...] = -jnp.exp(-softplus - 0.5)
        gate = _sigmoid(icl[:, d * w:(d + 1) * w])
        kd = k * (1.0 + (gate - 1.0) * ka_ref[...])
        kd_ref[...] = kd * valid
        b_ref[...] = kk * gate * valid
        bonus = bonus + head_sum(r * kd * rk_ref[...]) * v
    bonus_ref[...] = bonus


def _scan_kernel(fwd_blk, bwd_blk, first,
                 rf, vf, af, lwf, kdf, bf_, rb, vb, ab, lwb, kdb, bb_,
                 of_ref, ob_ref, s_scr):
    del fwd_blk, bwd_blk
    step = pl.program_id(0)
    c = SCAN_CHUNK

    @pl.when(first[step] == 1)
    def _():
        s_scr[...] = jnp.zeros(s_scr.shape, F32)

    t_sq = lax.broadcasted_iota(jnp.int32, (c, c), 0)
    s_sq = lax.broadcasted_iota(jnp.int32, (c, c), 1)
    t_cat = lax.broadcasted_iota(jnp.int32, (c, HGROUP), 0)
    s_cat = lax.broadcasted_iota(jnp.int32, (c, HGROUP), 1) & (c - 1)
    bd_row = lax.broadcasted_iota(jnp.int32, (HGROUP, HGROUP), 0) >> 6
    bd_col = lax.broadcasted_iota(jnp.int32, (HGROUP, HGROUP), 1) >> 6
    bd_mask = bd_row == bd_col
    eye_cat = jnp.where(t_cat == s_cat, 1.0, 0.0).astype(F32)

    def block_diag(x):
        return jnp.where(bd_mask, jnp.concatenate([x] * (HGROUP // c), axis=0), 0.0)

    dirs = ((0, rf, vf, af, lwf, kdf, bf_, of_ref), (1, rb, vb, ab, lwb, kdb, bb_, ob_ref))
    for d, r_ref, v_ref, a_ref, lw_ref, kd_ref, b_ref, o_ref in dirs:
        rev = d == 1
        earlier_sq = (s_sq >= t_sq) if rev else (s_sq <= t_sq)
        tri = jnp.where(earlier_sq, 1.0, 0.0).astype(BF16)
        strict = (s_cat > t_cat) if rev else (s_cat < t_cat)
        incl = (s_cat >= t_cat) if rev else (s_cat <= t_cat)
        lw = lw_ref[...]
        h1 = lw.astype(BF16)
        r1 = lw - h1.astype(F32)
        h2 = r1.astype(BF16)
        h3 = (r1 - h2.astype(F32)).astype(BF16)
        cum = _dot(tri, h1) + _dot(tri, h2) + _dot(tri, h3)
        e_in = jnp.exp(cum)
        e_ex = jnp.exp(cum - lw)
        e_neg = jnp.exp(-cum)
        a_t = a_ref[...] * e_ex
        r_t = r_ref[...] * e_in
        b_t = b_ref[...] * e_neg
        k_t = kd_ref[...] * e_neg
        w_tot = e_in[0:1, :] if rev else e_in[c - 1:c, :]
        v_all = v_ref[...]
        for g in range(D_WIDTH // HGROUP):
            cs = slice(g * HGROUP, (g + 1) * HGROUP)
            at, rt, bt, kt, vc = a_t[:, cs], r_t[:, cs], b_t[:, cs], k_t[:, cs], v_all[:, cs]
            lhs = jnp.concatenate([at, rt], axis=0).astype(BF16)
            m_b = _dot_nt(lhs, block_diag(bt).astype(BF16))
            m_k = _dot_nt(lhs, block_diag(kt).astype(BF16))
            a_ab = jnp.where(strict, m_b[:c], 0.0)
            a_rb = jnp.where(incl, m_b[c:], 0.0)
            a_ak = jnp.where(strict, m_k[:c], 0.0)
            a_rk = jnp.where(incl, m_k[c:], 0.0)
            lvl = jnp.where((t_cat >> 1) == (s_cat >> 1), a_ab, 0.0)
            tinv = eye_cat + lvl
            m = 2
            while m < c:
                sh = int(math.log2(m))
                a_m = jnp.where((t_cat >> (sh + 1)) == (s_cat >> (sh + 1)),
                                jnp.where((t_cat >> sh) != (s_cat >> sh), a_ab, 0.0), 0.0).astype(BF16)
                x = _dot(a_m, block_diag(tinv).astype(BF16))
                tinv = tinv + _dot(tinv.astype(BF16), block_diag(x).astype(BF16))
                m *= 2
            state = s_scr[d, g]
            st_b = state.astype(BF16)
            vbd = block_diag(vc).astype(BF16)
            rhs = _dot_nt(at.astype(BF16), st_b) + _dot(a_ak.astype(BF16), vbd)
            u = _dot(tinv.astype(BF16), block_diag(rhs).astype(BF16))
            out = (_dot_nt(rt.astype(BF16), st_b) + _dot(a_rb.astype(BF16), block_diag(u).astype(BF16))
                   + _dot(a_rk.astype(BF16), vbd))
            o_ref[:, cs] = out
            upd = _dot_tn(jnp.concatenate([u, vc], axis=0).astype(BF16),
                          jnp.concatenate([bt, kt], axis=0).astype(BF16))
            s_scr[d, g] = jnp.where(bd_mask, (state + upd) * w_tot[:, cs], 0.0)


def _d_post_kernel(of_ref, ob_ref, bonus_ref, g_ref, lng_ref, lnb_ref, bs_ref, o_ref):
    o = of_ref[...] + ob_ref[...]
    bs = bs_ref[...]

    def head_mean(x):
        return jnp.concatenate(
            [_group_mean(x[:, s * HGROUP:(s + 1) * HGROUP], bs) for s in range(D_WIDTH // HGROUP)], axis=1)

    cen = o - head_mean(o)
    var = head_mean(cen * cen)
    y = cen * lax.rsqrt(var + LNX_EPS) * lng_ref[...] + lnb_ref[...] + bonus_ref[...]
    o_ref[...] = (y * g_ref[...]).astype(o_ref.dtype)


def _scan_tables(segments):
    fwd, bwd, first = [], [], []
    for start, bsz, lp in segments:
        nc = lp // SCAN_CHUNK
        for b in range(bsz):
            base = (start + b * lp) // SCAN_CHUNK
            for ci in range(nc):
                fwd.append(base + ci)
                bwd.append(base + nc - 1 - ci)
                first.append(1 if ci == 0 else 0)
    as_i32 = lambda t: jnp.asarray(np.asarray(t, np.int32))
    return as_i32(fwd), as_i32(bwd), as_i32(first)


def _scan(segments, n_rows, r, v, a, lwf, lwb, kdf, kdb, bf_, bb_):
    fwd, bwd, first = _scan_tables(segments)
    nsteps = fwd.shape[0]
    blk = (SCAN_CHUNK, D_WIDTH)
    fspec = pl.BlockSpec(blk, lambda s, fw, bw, fi: (fw[s], 0))
    bspec = pl.BlockSpec(blk, lambda s, fw, bw, fi: (bw[s], 0))
    grid_spec = pltpu.PrefetchScalarGridSpec(
        num_scalar_prefetch=3, grid=(nsteps,),
        in_specs=[fspec] * 6 + [bspec] * 6,
        out_specs=[fspec, bspec],
        scratch_shapes=[pltpu.VMEM((2, D_WIDTH // HGROUP, HGROUP, HGROUP), F32)],
    )
    return pl.pallas_call(
        _scan_kernel, grid_spec=grid_spec,
        out_shape=[jax.ShapeDtypeStruct((n_rows, D_WIDTH), F32)] * 2,
        compiler_params=_cparams(("arbitrary",)), name="wkv_scan",
    )(fwd, bwd, first, r, v, a, lwf, kdf, bf_, r, v, a, lwb, kdb, bb_)


def _rope_tables(segments, n_rows, layer_kind):
    outs = []
    for start, bsz, lp in segments:
        p = jnp.arange(lp)
        d = jnp.arange(HEAD_DIM)
        if layer_kind == "axial":
            t = p - FRONT
            row = jnp.where(t >= 0, t // GRID_W, jnp.where(p >= PAD, t, 0)).astype(F32)
            col = jnp.where(t >= 0, t % GRID_W, jnp.where(p >= PAD, t, 0)).astype(F32)
            half = HEAD_DIM // 2
            inv = A_THETA ** (-jnp.arange(0, half, 2, dtype=F32) / half)
            ang = jnp.concatenate([row[:, None] * inv] * 2 + [col[:, None] * inv] * 2, axis=1)
            first = (d % 32) < 16
            cos = jnp.cos(ang)
            sa = jnp.where(first[None, :], -jnp.sin(ang), 0.0)
            sb = jnp.where(first[None, :], 0.0, jnp.sin(ang))
        else:
            pos = jnp.maximum(p - PAD, 0).astype(F32)
            inv = ROPE_THETA ** (-jnp.arange(0, ROPE_DIMS, 2, dtype=F32) / ROPE_DIMS)
            ang8 = pos[:, None] * inv
            ang = jnp.concatenate([ang8, ang8] + [jnp.zeros_like(ang8)] * 6, axis=1)
            cos = jnp.where((d < ROPE_DIMS)[None, :], jnp.cos(ang), 1.0)
            sa = jnp.where((d < 8)[None, :], -jnp.sin(ang), 0.0)
            sb = jnp.where(((d >= 8) & (d < 16))[None, :], jnp.sin(ang), 0.0)
        tabs = [jnp.tile(jnp.concatenate([t_, t_], axis=1), (bsz, 1)) for t_ in (cos, sa, sb)]
        outs.append(tabs)
    tail = n_rows - sum(b * lp for _, b, lp in segments)
    res = []
    for idx in range(3):
        parts = [o[idx] for o in outs] + [jnp.zeros((tail, LANES), F32)]
        res.append(jnp.concatenate(parts, axis=0))
    return res


def _seq_view(flat, seg, width):
    start, bsz, lp = seg
    return flat[start:start + bsz * lp].reshape(bsz, lp, width)


def _kv_layouts(q, k, v, seg, n_kv, chunked):
    bsz, lp = seg[1], seg[2]
    qT = jnp.transpose(_seq_view(q, seg, q.shape[1]), (0, 2, 1))
    k4 = jnp.transpose(_seq_view(k, seg, k.shape[1]).reshape(bsz, lp, n_kv, HEAD_DIM), (0, 2, 1, 3))
    vT = jnp.transpose(_seq_view(v, seg, v.shape[1]).reshape(bsz, lp, n_kv, HEAD_DIM), (0, 2, 3, 1))
    aug = jnp.concatenate([vT, jnp.ones((bsz, n_kv, 1, lp), BF16),
                           jnp.zeros((bsz, n_kv, V_ROWS - HEAD_DIM - 1, lp), BF16)], axis=2)
    if chunked:
        aug = jnp.transpose(aug.reshape(bsz, n_kv, V_ROWS, lp // LANES, LANES), (0, 1, 3, 2, 4))
    return qT, k4, aug


def _to_flat(parts, n_rows, width, dtype):
    used = sum(p.shape[0] for p in parts)
    return jnp.concatenate(parts + [jnp.zeros((n_rows - used, width), dtype)], axis=0)


def _block_avg(n, group):
    idx = np.arange(n) // group
    return jnp.asarray((idx[:, None] == idx[None, :]).astype(np.float32) / group).astype(BF16)


def _forward(xs, p):
    segments = []
    start = 0
    for x in xs:
        bsz, s, _ = x.shape
        lp = s + FRONT
        segments.append((start, bsz, lp))
        start += bsz * lp
    n_used = start
    tm = 512 if n_used >= 4096 else LANES
    n_rows = _round_up(n_used, tm)

    meta = p['meta_tokens'].astype(F32)
    parts = []
    for x in xs:
        bsz = x.shape[0]
        lead = jnp.concatenate([jnp.zeros((PAD, D_MODEL), F32), meta], axis=0)
        parts.append(jnp.concatenate([jnp.broadcast_to(lead[None], (bsz, FRONT, D_MODEL)), x], axis=1)
                     .reshape(-1, D_MODEL))
    h = _to_flat(parts, n_rows, D_MODEL, F32)

    bs = _block_avg(HGROUP, HEAD_DIM)
    row = lambda t: t.reshape(1, -1).astype(F32)

    depth = p['pre_mix_g'].shape[0]
    for i in range(depth):
        if i % 2 == 0:
            e = i // 2
            cos, sa, sb = _rope_tables(segments, n_rows, "axial")
            ch = np.arange(B_DIM)
            ang = 2.0 * np.pi * ((ch[:, None] * ch[None, :]) % B_DIM) / B_DIM
            cc, sc = jnp.asarray(np.cos(ang), F32), jnp.asarray(np.sin(ang), F32)
            wl = p['b_w'][e].astype(F32)
            hp = lax.Precision.HIGHEST
            pmat = jnp.einsum('cd,gde->gce', cc, wl, precision=hp)
            qmat = -jnp.einsum('cd,gde->gce', sc, wl, precision=hp)
            bd = lambda m: jax.scipy.linalg.block_diag(*[m[g] for g in range(B_GROUPS)])
            pq_base = jnp.concatenate([bd(pmat), bd(qmat)], axis=1)
            q, k, v, gf = _in_even_call(h, cos, sa, sb, p, i, e, bs, pq_base, n_rows, tm)
            ya_parts, yb_parts = [], []
            for seg in segments:
                _, bsz, lp = seg
                qT, k4, vT5 = _kv_layouts(q, k, v, seg, A_KV_HEADS, True)
                oT = _attn_a(qT, k4, vT5)
                ya_parts.append(jnp.transpose(oT, (0, 2, 1)).reshape(bsz * lp, A_Q))
                gseq = _seq_view(gf, seg, 2 * B_W)[:, PAD:]
                yb = _mixer_b(gseq, row(p['b_b'][e]))
                yb_parts.append(jnp.pad(yb, ((0, 0), (PAD, 0), (0, 0))).reshape(bsz * lp, B_W).astype(BF16))
            ya = _to_flat(ya_parts, n_rows, A_Q, BF16)
            yb = _to_flat(yb_parts, n_rows, B_W, BF16)
            w_out = p['even_w_out'][e].astype(BF16)
            wa, wb = w_out[:A_Q], w_out[A_Q:]
        else:
            o = i // 2
            cos, sa, sb = _rope_tables(segments, n_rows, "partial")
            q, k, v, u = _row_call(
                "in_odd", _in_odd_kernel, n_rows, tm, [h, cos, sa, sb],
                [row(p['pre_mix_g'][i]), p['odd_w_in'][o].astype(BF16)],
                [(C_Q, BF16), (C_KV, BF16), (C_KV, BF16), (D_IN, F32)])
            yc_parts = []
            for seg in segments:
                _, bsz, lp = seg
                qT, k4, vT4 = _kv_layouts(q, k, v, seg, C_KV_HEADS, False)
                oT = _attn_c(p['c_sink'][o].astype(F32), qT, k4, vT4)
                yc_parts.append(jnp.transpose(oT, (0, 2, 1)).reshape(bsz * lp, C_Q))
            ya = _to_flat(yc_parts, n_rows, C_Q, BF16)
            yb = _mixer_d(u, p, o, bs, segments, n_rows, tm)
            w_out = p['odd_w_out'][o].astype(BF16)
            wa, wb = w_out[:C_Q], w_out[C_Q:]
        (h,) = _row_call("out_proj", _out_proj_kernel, n_rows, tm, [h, ya, yb],
                         [wa, wb, row(p['post_mix_g'][i])], [(D_MODEL, F32)])
        (h,) = _row_call(
            "ffn", functools.partial(_ffn_kernel, chunk=256), n_rows, tm, [h],
            [row(p['pre_ffn_g'][i]), p['ffn_w_gate'][i].astype(BF16), p['ffn_w_up'][i].astype(BF16),
             p['ffn_w_down'][i].astype(BF16), row(p['post_ffn_g'][i])],
            [(D_MODEL, F32)], scratch=[pltpu.VMEM((tm, D_MODEL), F32)])

    outs = []
    for seg in segments:
        outs.append(_seq_view(h, seg, D_MODEL)[:, FRONT:])
    return tuple(outs)


def _in_even_call(h, cos, sa, sb, p, i, e, bs, pq_base, n_rows, tm):
    row = lambda t: t.reshape(1, -1).astype(F32)
    pq = pq_base.astype(BF16)
    grid = (n_rows // tm,)
    consts = [row(p['pre_mix_g'][i]), p['even_w_in'][e].astype(BF16),
              row(jnp.tile(p['a_q_gain'][e], A_HEADS) * (HEAD_DIM ** -0.5 * LOG2_E)),
              row(jnp.tile(p['a_k_gain'][e], A_KV_HEADS)), row(p['b_norm_g'][e]), bs, pq]
    in_specs = [pl.BlockSpec((tm, D_MODEL), lambda r: (r, 0))]
    in_specs += [pl.BlockSpec(a.shape, lambda r, nd=a.ndim: (0,) * nd) for a in consts]
    in_specs += [pl.BlockSpec((tm, LANES), lambda r: (r, 0))] * 3
    outs = [(A_Q, BF16), (A_KV, BF16), (A_KV, BF16), (2 * B_W, F32)]
    return pl.pallas_call(
        _in_even_kernel, grid=grid, in_specs=in_specs,
        out_specs=[pl.BlockSpec((tm, n), lambda r: (r, 0)) for n, _ in outs],
        out_shape=[jax.ShapeDtypeStruct((n_rows, n), dt) for n, dt in outs],
        compiler_params=_cparams(("parallel",)), name="in_even",
    )(h, *consts, cos, sa, sb)


def _mixer_d(u, p, o, bs, segments, n_rows, tm):
    row = lambda t: t.reshape(1, -1).astype(F32)
    w = D_WIDTH
    zeros = lambda r: jnp.zeros((r, w), F32)
    w2 = jnp.concatenate([jnp.concatenate([p['d_w_up'][o][0], zeros(DECAY_RANK)], axis=1),
                          jnp.concatenate([zeros(DECAY_RANK), p['d_w_up'][o][1]], axis=1)], axis=0)
    a2 = jnp.concatenate([jnp.concatenate([p['d_a_up'][o][0], zeros(ICLR_RANK)], axis=1),
                          jnp.concatenate([zeros(ICLR_RANK), p['d_a_up'][o][1]], axis=1)], axis=0)
    consts = [row(p['d_mu_prev'][o]), row(p['d_mu_next'][o]), w2.astype(BF16), a2.astype(BF16),
              p['d_g_up'][o].astype(BF16), row(p['d_w0'][o]), row(p['d_a0'][o]),
              row(p['d_k_k'][o]), row(p['d_k_a'][o]), row(p['d_r_k'][o]), bs]
    nb8 = n_rows // 8
    t8 = tm // 8
    in_specs = [pl.BlockSpec((tm, D_IN), lambda i: (i, 0)),
                pl.BlockSpec((8, D_IN), lambda i: (jnp.maximum(i * t8 - 1, 0), 0)),
                pl.BlockSpec((8, D_IN), lambda i: (jnp.minimum((i + 1) * t8, nb8 - 1), 0))]
    in_specs += [pl.BlockSpec(a.shape, lambda i, nd=a.ndim: (0,) * nd) for a in consts]
    n_out = 11
    prep = pl.pallas_call(
        functools.partial(_d_prep_kernel, tm=tm, segments=tuple(segments)),
        grid=(n_rows // tm,), in_specs=in_specs,
        out_specs=[pl.BlockSpec((tm, w), lambda i: (i, 0))] * n_out,
        out_shape=[jax.ShapeDtypeStruct((n_rows, w), F32)] * n_out,
        compiler_params=_cparams(("parallel",)), name="d_prep",
    )(u, u, u, *consts)
    r, v, a, lwf, lwb, kdf, kdb, bf_, bb_, bonus, g = prep
    of, ob = _scan(segments, n_rows, r, v, a, lwf, lwb, kdf, kdb, bf_, bb_)
    (yd,) = _row_call("d_post", _d_post_kernel, n_rows, tm, [of, ob, bonus, g],
                      [row(p['d_ln_g'][o]), row(p['d_ln_b'][o]), bs], [(w, BF16)])
    return yd


def kernel(x_prompt, x_sample, meta_tokens, pre_mix_g, post_mix_g, pre_ffn_g, post_ffn_g, even_w_in, even_w_out, a_q_gain, a_k_gain, b_norm_g, b_w, b_b, odd_w_in, odd_w_out, c_sink, d_mu_prev, d_mu_next, d_w0, d_w_up, d_a0, d_a_up, d_g_up, d_k_k, d_k_a, d_r_k, d_ln_g, d_ln_b, ffn_w_gate, ffn_w_up, ffn_w_down):
    params = dict(meta_tokens=meta_tokens, pre_mix_g=pre_mix_g, post_mix_g=post_mix_g,
                  pre_ffn_g=pre_ffn_g, post_ffn_g=post_ffn_g,
                  even_w_in=even_w_in, even_w_out=even_w_out, a_q_gain=a_q_gain, a_k_gain=a_k_gain,
                  b_norm_g=b_norm_g, b_w=b_w, b_b=b_b,
                  odd_w_in=odd_w_in, odd_w_out=odd_w_out, c_sink=c_sink,
                  d_mu_prev=d_mu_prev, d_mu_next=d_mu_next, d_w0=d_w0, d_w_up=d_w_up,
                  d_a0=d_a0, d_a_up=d_a_up, d_g_up=d_g_up, d_k_k=d_k_k, d_k_a=d_k_a, d_r_k=d_r_k,
                  d_ln_g=d_ln_g, d_ln_b=d_ln_b,
                  ffn_w_gate=ffn_w_gate, ffn_w_up=ffn_w_up, ffn_w_down=ffn_w_down)
    return _forward([x_prompt, x_sample], params)
```

```python
import functools
import math

import numpy as np
import jax
import jax.numpy as jnp
from jax import lax
from jax.experimental import pallas as pl
from jax.experimental.pallas import tpu as pltpu

F32 = jnp.float32
BF16 = jnp.bfloat16

D_MODEL = 1024
HEAD_DIM = 64
N_META = 16
GRID_W = 64
WINDOW = 128
RMS_EPS = 1e-6
A_HEADS, A_KV_HEADS, A_THETA = 12, 4, 10000.0
B_GROUPS, B_DIM = 4, 64
C_HEADS, C_KV_HEADS = 8, 2
ROPE_THETA = 500000.0
ROPE_DIMS = HEAD_DIM // 4
D_HEADS = 8
D_WIDTH = D_HEADS * HEAD_DIM
DECAY_RANK, ICLR_RANK, GATE_RANK = 64, 64, 128
LNX_EPS = 64e-5
D_FF = 2816
A_Q, A_KV, B_W = A_HEADS * HEAD_DIM, A_KV_HEADS * HEAD_DIM, B_GROUPS * B_DIM
EVEN_IN = A_Q + 2 * A_KV + B_W
C_Q, C_KV = C_HEADS * HEAD_DIM, C_KV_HEADS * HEAD_DIM
D_IN = 3 * D_WIDTH + 2 * DECAY_RANK + 2 * ICLR_RANK + GATE_RANK
ODD_IN = C_Q + 2 * C_KV + D_IN

LANES = 128
FRONT = 256
PAD = FRONT - N_META
NEG = -1e30
SCAN_CHUNK = 64
SCAN_CHUNKS_PER_STEP = 4
HGROUP = 256
FFT_L1 = 16
V_ROWS = 72
VMEM_LIMIT = 56 * 1024 * 1024
ATTN_CHUNK = 256
ATTN_MAX_CHUNKS = 20
ATTN_UNROLL = 8
LOG2_E = math.log2(math.e)


def _round_up(x, m):
    return (x + m - 1) // m * m


def _cparams(sem):
    return pltpu.CompilerParams(dimension_semantics=sem, vmem_limit_bytes=VMEM_LIMIT)


def _dot(a, b):
    return jnp.dot(a, b, preferred_element_type=F32)


def _dot_nt(a, b):
    return lax.dot_general(a, b, (((1,), (1,)), ((), ())), preferred_element_type=F32)


def _dot_tn(a, b):
    return lax.dot_general(a, b, (((0,), (0,)), ((), ())), preferred_element_type=F32)


def _split2(x):
    hi = x.astype(BF16)
    lo = (x - hi.astype(F32)).astype(BF16)
    return hi, lo


def _group_mean(x, bs):
    hi, lo = _split2(x)
    return _dot(hi, bs) + _dot(lo, bs)


def _rope(x, cos, sin_a, sin_b, shift):
    n = x.shape[1]
    return x * cos + pltpu.roll(x, n - shift, 1) * sin_a + pltpu.roll(x, shift, 1) * sin_b


def _wide(t, n):
    return t if n == LANES else jnp.concatenate([t] * (n // LANES), axis=1)


def _rms_rows(x, g):
    ms = jnp.mean(x * x, axis=-1, keepdims=True)
    return x * lax.rsqrt(ms + RMS_EPS) * g


def _in_even_kernel(h_ref, g_ref, w_ref, qg_ref, kg_ref, bg_ref, bs_ref, pq_ref,
                    cos_ref, sa_ref, sb_ref, q_ref, k_ref, v_ref, gf_ref):
    hn = _rms_rows(h_ref[...], g_ref[...]).astype(BF16)
    proj = _dot(hn, w_ref[...])
    bs = bs_ref[...]
    cos, sa, sb = (_wide(r[...], HGROUP) for r in (cos_ref, sa_ref, sb_ref))

    def norm_rope(x, gain):
        xn = x * lax.rsqrt(_group_mean(x * x, bs) + RMS_EPS) * gain
        return _rope(xn, cos, sa, sb, 16).astype(BF16)

    for s in range(A_Q // HGROUP):
        cs = slice(s * HGROUP, (s + 1) * HGROUP)
        q_ref[:, cs] = norm_rope(proj[:, cs], qg_ref[:, cs])
    k_ref[...] = norm_rope(proj[:, A_Q:A_Q + A_KV], kg_ref[...])
    v_ref[...] = proj[:, A_Q + A_KV:A_Q + 2 * A_KV].astype(BF16)
    f = proj[:, A_Q + 2 * A_KV:]
    fn = f * lax.rsqrt(_group_mean(f * f, bs) + RMS_EPS) * bg_ref[...]
    gf_ref[...] = _dot(fn.astype(BF16), pq_ref[...])


def _in_odd_kernel(h_ref, cos_ref, sa_ref, sb_ref, g_ref, w_ref, q_ref, k_ref, v_ref, u_ref):
    hn = _rms_rows(h_ref[...], g_ref[...]).astype(BF16)
    proj = _dot(hn, w_ref[...])
    cos, sa, sb = cos_ref[...], sa_ref[...], sb_ref[...]
    cos2, sa2, sb2 = (_wide(t, HGROUP) for t in (cos, sa, sb))
    for s in range(C_Q // HGROUP):
        cs = slice(s * HGROUP, (s + 1) * HGROUP)
        q_ref[:, cs] = (_rope(proj[:, cs], cos2, sa2, sb2, 8) * HEAD_DIM ** -0.5).astype(BF16)
    k_ref[...] = _rope(proj[:, C_Q:C_Q + C_KV], cos, sa, sb, 8).astype(BF16)
    v_ref[...] = proj[:, C_Q + C_KV:C_Q + 2 * C_KV].astype(BF16)
    u_ref[...] = proj[:, C_Q + 2 * C_KV:]


def _out_proj_kernel(h_ref, ya_ref, yb_ref, wa_ref, wb_ref, g_ref, o_ref):
    mix = _dot(ya_ref[...], wa_ref[...]) + _dot(yb_ref[...], wb_ref[...])
    o_ref[...] = h_ref[...] + _rms_rows(mix, g_ref[...])


def _ffn_kernel(h_ref, g1_ref, wg_ref, wu_ref, wd_ref, g2_ref, o_ref, acc_ref, *, chunk):
    h = h_ref[...]
    hn = _rms_rows(h, g1_ref[...]).astype(BF16)
    for c in range(D_FF // chunk):
        cs = slice(c * chunk, (c + 1) * chunk)
        gate = _dot(hn, wg_ref[:, cs])
        up = _dot(hn, wu_ref[:, cs])
        act = (gate * (1.0 / (1.0 + jnp.exp(-gate))) * up).astype(BF16)
        part = _dot(act, wd_ref[cs, :])
        if c == 0:
            acc_ref[...] = part
        else:
            acc_ref[...] += part
    o_ref[...] = h + _rms_rows(acc_ref[...], g2_ref[...])


def _row_call(name, kernel, n_rows, tm, row_ins, const_ins, outs, scratch=()):
    grid = (n_rows // tm,)
    in_specs = [pl.BlockSpec((tm, a.shape[1]), lambda i: (i, 0)) for a in row_ins]
    in_specs += [pl.BlockSpec(a.shape, lambda i, nd=a.ndim: (0,) * nd) for a in const_ins]
    out_specs = [pl.BlockSpec((tm, n), lambda i: (i, 0)) for n, _ in outs]
    out_shape = [jax.ShapeDtypeStruct((n_rows, n), dt) for n, dt in outs]
    res = pl.pallas_call(
        kernel, grid=grid, in_specs=in_specs, out_specs=out_specs, out_shape=out_shape,
        scratch_shapes=list(scratch), compiler_params=_cparams(("parallel",)), name=name,
    )(*row_ins, *const_ins)
    return res


def _attn_a_kernel(qT_ref, k_ref, vT_ref, o_ref, m_scr, acc_scr, s_scr, *, nchunk, nkb, tq):
    j = pl.program_id(2)

    @pl.when(j == 0)
    def _():
        m_scr[...] = jnp.full(m_scr.shape, NEG, F32)
        acc_scr[...] = jnp.zeros(acc_scr.shape, F32)

    group = A_HEADS // A_KV_HEADS
    ck = ATTN_CHUNK
    key_row = lax.broadcasted_iota(jnp.int32, (ck, tq), 0) + j * (nchunk * ck)

    def scores(h, c):
        kc = k_ref[0, h // group, pl.ds(pl.multiple_of(c * ck, ck), ck), :]
        return _dot(kc, qT_ref[0, h * HEAD_DIM:(h + 1) * HEAD_DIM, :])

    def fold(s):
        return jnp.max(s.reshape(ck // 8, 8, tq), axis=0)

    def first_scores(h):
        s0 = jnp.where(key_row >= PAD, scores(h, 0), NEG)
        s_scr[h % 2, 0] = s0
        return fold(s0)

    def next_scores(h, c, mx):
        s = scores(h, c)
        s_scr[h % 2, c] = s
        return jnp.maximum(mx, fold(s))

    def weighted_values(h, c, m_new):
        p = jnp.exp2(s_scr[h % 2, c] - m_new).astype(BF16)
        return _dot(vT_ref[0, h // group, c], p)

    unroll = max(u for u in range(1, ATTN_UNROLL + 1) if max(nchunk - 1, 1) % u == 0)
    mx = lax.fori_loop(1, nchunk, functools.partial(next_scores, 0), first_scores(0), unroll=unroll)
    for h in range(A_HEADS):
        m_old = m_scr[h]
        m_new = jnp.maximum(m_old, jnp.max(mx, axis=0, keepdims=True))
        if h + 1 < A_HEADS:
            def body(c, carry, h=h, m_new=m_new):
                acc, mxn = carry
                return acc + weighted_values(h, c, m_new), next_scores(h + 1, c, mxn)

            mx0 = first_scores(h + 1)
            acc, mx = lax.fori_loop(1, nchunk, body, (weighted_values(h, 0, m_new), mx0), unroll=unroll)
        else:
            acc = lax.fori_loop(1, nchunk, lambda c, acc: acc + weighted_values(h, c, m_new),
                                weighted_values(h, 0, m_new), unroll=unroll)
        acc_scr[h] = acc_scr[h] * jnp.exp2(m_old - m_new) + acc
        m_scr[h] = m_new

    @pl.when(j == nkb - 1)
    def _():
        for h in range(A_HEADS):
            a = acc_scr[h]
            o_ref[0, h * HEAD_DIM:(h + 1) * HEAD_DIM, :] = (
                a[:HEAD_DIM] / a[HEAD_DIM:HEAD_DIM + 1]).astype(o_ref.dtype)


def _attn_a(qT, k4, vT5):
    bsz, _, lp = qT.shape
    tq = ATTN_CHUNK
    total = lp // ATTN_CHUNK
    nkb = min(n for n in range(1, total + 1) if total % n == 0 and total // n <= ATTN_MAX_CHUNKS)
    tk = lp // nkb
    nchunk = tk // ATTN_CHUNK
    kernel = functools.partial(_attn_a_kernel, nchunk=nchunk, nkb=nkb, tq=tq)
    return pl.pallas_call(
        kernel, grid=(bsz, lp // tq, nkb),
        in_specs=[
            pl.BlockSpec((1, A_Q, tq), lambda b, i, j: (b, 0, i)),
            pl.BlockSpec((1, A_KV_HEADS, tk, HEAD_DIM), lambda b, i, j: (b, 0, j, 0)),
            pl.BlockSpec((1, A_KV_HEADS, nchunk, V_ROWS, ATTN_CHUNK), lambda b, i, j: (b, 0, j, 0, 0)),
        ],
        out_specs=pl.BlockSpec((1, A_Q, tq), lambda b, i, j: (b, 0, i)),
        out_shape=jax.ShapeDtypeStruct((bsz, A_Q, lp), BF16),
        scratch_shapes=[
            pltpu.VMEM((A_HEADS, 1, tq), F32),
            pltpu.VMEM((A_HEADS, V_ROWS, tq), F32),
            pltpu.VMEM((2, nchunk, ATTN_CHUNK, tq), F32),
        ],
        compiler_params=_cparams(("parallel", "parallel", "arbitrary")), name="attn_a",
    )(qT, k4, vT5)


def _attn_c_kernel(sink_ref, qT_ref, k0, k1, k2, k3, v0, v1, v2, v3, o_ref, *, lp):
    j = pl.program_id(1)
    k_refs, v_refs = (k0, k1, k2, k3), (v0, v1, v2, v3)
    row = lax.broadcasted_iota(jnp.int32, (LANES, LANES), 0)
    pq = lax.broadcasted_iota(jnp.int32, (LANES, LANES), 1) + j * LANES
    biases = []
    for slot in range(4):
        if slot == 0:
            bias = jnp.where(row >= PAD % LANES, 0.0, NEG)
        else:
            pk = row + (j + slot - 2) * LANES
            in_window = jnp.where(jnp.abs(pq - pk) <= WINDOW, 0.0, NEG)
            bias = jnp.where(pk >= FRONT, jnp.where(pk < lp, in_window, NEG), NEG)
        biases.append(bias.astype(F32))
    group = C_HEADS // C_KV_HEADS
    for h in range(C_HEADS):
        g = h // group
        qh = qT_ref[0, h * HEAD_DIM:(h + 1) * HEAD_DIM, :]
        sink = sink_ref[h]
        ss = [_dot(k_refs[t][0, g], qh) + biases[t] for t in range(4)]
        m = jnp.maximum(jnp.maximum(ss[0], ss[1]), jnp.maximum(ss[2], ss[3]))
        m = jnp.maximum(jnp.max(m, axis=0, keepdims=True), sink)
        acc = jnp.zeros((V_ROWS, LANES), F32)
        for t in range(4):
            acc = acc + _dot(v_refs[t][0, g], jnp.exp(ss[t] - m).astype(BF16))
        denom = acc[HEAD_DIM:HEAD_DIM + 1] + jnp.exp(sink - m)
        o_ref[0, h * HEAD_DIM:(h + 1) * HEAD_DIM, :] = (acc[:HEAD_DIM] / denom).astype(o_ref.dtype)


def _attn_c(sink, qT, k4, vT4):
    bsz, _, lp = qT.shape
    nb = lp // LANES
    kernel = functools.partial(_attn_c_kernel, lp=lp)

    def kspec(fn):
        return pl.BlockSpec((1, C_KV_HEADS, LANES, HEAD_DIM), lambda b, j: (b, 0, fn(j), 0))

    def vspec(fn):
        return pl.BlockSpec((1, C_KV_HEADS, V_ROWS, LANES), lambda b, j: (b, 0, 0, fn(j)))

    fns = (lambda j: PAD // LANES, lambda j: jnp.maximum(j - 1, 0), lambda j: j,
           lambda j: jnp.minimum(j + 1, nb - 1))
    return pl.pallas_call(
        kernel, grid=(bsz, nb),
        in_specs=[pl.BlockSpec(memory_space=pltpu.SMEM),
                  pl.BlockSpec((1, C_Q, LANES), lambda b, j: (b, 0, j))]
                 + [kspec(f) for f in fns] + [vspec(f) for f in fns],
        out_specs=pl.BlockSpec((1, C_Q, LANES), lambda b, j: (b, 0, j)),
        out_shape=jax.ShapeDtypeStruct((bsz, C_Q, lp), BF16),
        compiler_params=_cparams(("parallel", "parallel")), name="attn_c",
    )(sink, qT, k4, k4, k4, k4, vT4, vT4, vT4, vT4)


def _fft1_kernel(x_ref, c1_ref, s1_ref, o_ref):
    x = x_ref[0]
    tn = x.shape[1]
    lane = lax.broadcasted_iota(jnp.int32, x.shape, 1)
    is_re = (lane & (2 * B_W - 1)) < B_W
    xs = jnp.where(is_re, pltpu.roll(x, tn - B_W, 1), -pltpu.roll(x, B_W, 1))
    o_ref[0] = _dot(c1_ref[...], x.astype(BF16)) + _dot(s1_ref[...], xs.astype(BF16))


def _fft2_kernel(a_ref, tc_ref, ts_ref, c2_ref, s2_ref, bias_ref, o_ref):
    a = a_ref[0, 0]
    tc, ts = _wide(tc_ref[0], B_W), _wide(ts_ref[0], B_W)
    are, aim = a[:, :B_W], a[:, B_W:]
    bre = (are * tc + aim * ts).astype(BF16)
    bim = (aim * tc - are * ts).astype(BF16)
    o_ref[0] = _dot(c2_ref[...], bre) + _dot(s2_ref[...], bim) + bias_ref[...]


def _dft_tables(l2, l2p, l2o):
    l1 = FFT_L1
    length = l1 * l2
    n1 = np.arange(l1)
    ang1 = 2.0 * np.pi * ((n1[:, None] * n1[None, :]) % l1) / l1
    c1, s1 = np.cos(ang1), np.sin(ang1)
    n2 = np.arange(l2)
    angt = 2.0 * np.pi * (n1[:, None] * n2[None, :]) / length
    tc = np.zeros((l1, l2p, LANES), np.float32)
    ts = np.zeros((l1, l2p, LANES), np.float32)
    tc[:, :l2, :] = np.cos(angt)[:, :, None]
    ts[:, :l2, :] = np.sin(angt)[:, :, None]
    ang2 = 2.0 * np.pi * ((n2[:, None] * n2[None, :]) % l2) / l2
    scale = 1.0 / math.sqrt(B_DIM * length)
    c2 = np.zeros((l2o, l2p), np.float32)
    s2 = np.zeros((l2o, l2p), np.float32)
    c2[:l2, :l2] = np.cos(ang2) * scale
    s2[:l2, :l2] = np.sin(ang2) * scale
    as_bf = lambda t: jnp.asarray(t, F32).astype(BF16)
    return as_bf(c1), as_bf(s1), jnp.asarray(tc), jnp.asarray(ts), as_bf(c2), as_bf(s2)


def _mixer_b(gseq, bias):
    bsz, length, _ = gseq.shape
    l1 = FFT_L1
    l2 = length // l1
    l2p, l2o = _round_up(l2, LANES), _round_up(l2, 8)
    c1, s1, tc, ts, c2, s2 = _dft_tables(l2, l2p, l2o)
    x = jnp.pad(gseq.reshape(bsz, l1, l2, 2 * B_W), ((0, 0), (0, 0), (0, l2p - l2), (0, 0)))
    x = x.reshape(bsz, l1, l2p * 2 * B_W)
    ncol = l2p * 2 * B_W
    tn = 2 * B_W * 48 if l2p % 48 == 0 else 2 * B_W
    a = pl.pallas_call(
        _fft1_kernel, grid=(bsz, ncol // tn),
        in_specs=[pl.BlockSpec((1, l1, tn), lambda b, i: (b, 0, i)),
                  pl.BlockSpec((l1, l1), lambda b, i: (0, 0)),
                  pl.BlockSpec((l1, l1), lambda b, i: (0, 0))],
        out_specs=pl.BlockSpec((1, l1, tn), lambda b, i: (b, 0, i)),
        out_shape=jax.ShapeDtypeStruct((bsz, l1, ncol), F32),
        compiler_params=_cparams(("parallel", "parallel")), name="fft1",
    )(x, c1, s1)
    a = a.reshape(bsz, l1, l2p, 2 * B_W)
    y = pl.pallas_call(
        _fft2_kernel, grid=(bsz, l1),
        in_specs=[pl.BlockSpec((1, 1, l2p, 2 * B_W), lambda b, k: (b, k, 0, 0)),
                  pl.BlockSpec((1, l2p, LANES), lambda b, k: (k, 0, 0)),
                  pl.BlockSpec((1, l2p, LANES), lambda b, k: (k, 0, 0)),
                  pl.BlockSpec((l2o, l2p), lambda b, k: (0, 0)),
                  pl.BlockSpec((l2o, l2p), lambda b, k: (0, 0)),
                  pl.BlockSpec((1, B_W), lambda b, k: (0, 0))],
        out_specs=pl.BlockSpec((1, l2o, B_W), lambda b, k: (b, 0, k)),
        out_shape=jax.ShapeDtypeStruct((bsz, l2o, l1 * B_W), F32),
        compiler_params=_cparams(("parallel", "parallel")), name="fft2",
    )(a, tc, ts, c2, s2, bias)
    return y[:, :l2].reshape(bsz, length, B_W)


def _seq_position(rows, segments):
    pos = jnp.full(rows.shape, -1.0, F32)
    seqlen = jnp.full(rows.shape, 1.0, F32)
    for start, bsz, lp in segments:
        rel = rows - float(start)
        q = jnp.floor((rel + 0.5) * (1.0 / lp))
        inside = jnp.where(rel >= 0.0, jnp.where(rel < float(bsz * lp), 1.0, 0.0), 0.0) > 0.5
        pos = jnp.where(inside, rel - q * lp, pos)
        seqlen = jnp.where(inside, float(lp), seqlen)
    return pos, seqlen


def _sigmoid(x):
    return 1.0 / (1.0 + jnp.exp(-x))


def _d_prep_kernel(u_ref, up_ref, un_ref, mup_ref, mun_ref, w2_ref, a2_ref, gup_ref, w0_ref, a0_ref,
                   kk_ref, ka_ref, rk_ref, bs_ref,
                   r_ref, v_ref, a_ref, lwf_ref, lwb_ref, kdf_ref, kdb_ref, bf_ref, bb_ref,
                   bonus_ref, g_ref, *, tm, segments):
    i = pl.program_id(0)
    u = u_ref[...]
    rows = (lax.broadcasted_iota(jnp.int32, (tm, 1), 0) + i * tm).astype(F32)
    pos, seqlen = _seq_position(rows, segments)
    local = lax.broadcasted_iota(jnp.int32, (tm, 1), 0)
    u_prev = jnp.where(local == 0, up_ref[7:8, :], pltpu.roll(u, 1, 0))
    u_next = jnp.where(local == tm - 1, un_ref[0:1, :], pltpu.roll(u, tm - 1, 0))
    u_prev = jnp.where(pos == float(PAD), 0.0, u_prev)
    u_next = jnp.where(pos == seqlen - 1.0, 0.0, u_next)
    u = u + mup_ref[...] * (u_prev - u) + mun_ref[...] * (u_next - u)
    valid = jnp.where(pos >= float(PAD), 1.0, 0.0)

    w = D_WIDTH
    r, k, v = u[:, :w], u[:, w:2 * w], u[:, 2 * w:3 * w]
    c0 = 3 * w
    dec = _dot(jnp.tanh(u[:, c0:c0 + 2 * DECAY_RANK]).astype(BF16), w2_ref[...]) + w0_ref[...]
    c0 += 2 * DECAY_RANK
    icl = _dot(u[:, c0:c0 + 2 * ICLR_RANK].astype(BF16), a2_ref[...]) + a0_ref[...]
    c0 += 2 * ICLR_RANK
    g_ref[...] = _dot(_sigmoid(u[:, c0:c0 + GATE_RANK]).astype(BF16), gup_ref[...])

    bs = bs_ref[...]

    def head_sum(x):
        return jnp.concatenate(
            [_group_mean(x[:, s * HGROUP:(s + 1) * HGROUP], bs) for s in range(w // HGROUP)], axis=1
        ) * float(HEAD_DIM)

    kk = k * kk_ref[...]
    kk = kk * lax.rsqrt(jnp.maximum(head_sum(kk * kk), 1e-24))
    r_ref[...] = r
    v_ref[...] = v
    a_ref[...] = -kk * valid
    bonus = jnp.zeros_like(r)
    for d, (lw_ref, kd_ref, b_ref) in enumerate(((lwf_ref, kdf_ref, bf_ref), (lwb_ref, kdb_ref, bb_ref))):
        x = -dec[:, d * w:(d + 1) * w]
        softplus = jnp.maximum(x, 0.0) + jnp.log(1.0 + jnp.exp(-jnp.abs(x)))
        lw_ref[...] = -jnp.exp(-softplus - 0.5)
        gate = _sigmoid(icl[:, d * w:(d + 1) * w])
        kd = k * (1.0 + (gate - 1.0) * ka_ref[...])
        kd_ref[...] = kd * valid
        b_ref[...] = kk * gate * valid
        bonus = bonus + head_sum(r * kd * rk_ref[...]) * v
    bonus_ref[...] = bonus


def _scan_kernel(fwd_blk, bwd_blk, first,
                 rf, vf, af, lwf, kdf, bf_, rb, vb, ab, lwb, kdb, bb_,
                 of_ref, ob_ref, s_scr, *, nk):
    del fwd_blk, bwd_blk
    step = pl.program_id(0)
    c = SCAN_CHUNK

    @pl.when(first[step] == 1)
    def _():
        s_scr[...] = jnp.zeros(s_scr.shape, F32)

    t_sq = lax.broadcasted_iota(jnp.int32, (nk * c, nk * c), 0)
    s_sq = lax.broadcasted_iota(jnp.int32, (nk * c, nk * c), 1)
    t_cat = lax.broadcasted_iota(jnp.int32, (c, HGROUP), 0)
    s_cat = lax.broadcasted_iota(jnp.int32, (c, HGROUP), 1) & (c - 1)
    bd_row = lax.broadcasted_iota(jnp.int32, (HGROUP, HGROUP), 0) >> 6
    bd_col = lax.broadcasted_iota(jnp.int32, (HGROUP, HGROUP), 1) >> 6
    bd_mask = bd_row == bd_col
    eye_cat = jnp.where(t_cat == s_cat, 1.0, 0.0).astype(F32)

    def block_diag(x):
        return jnp.where(bd_mask, jnp.concatenate([x] * (HGROUP // c), axis=0), 0.0).astype(BF16)

    chains = []
    dirs = ((0, rf, vf, af, lwf, kdf, bf_), (1, rb, vb, ab, lwb, kdb, bb_))
    for d, r_ref, v_ref, a_ref, lw_ref, kd_ref, b_ref in dirs:
        rev = d == 1
        earlier = (s_sq >= t_sq) if rev else (s_sq <= t_sq)
        tri = jnp.where((t_sq >> 6) == (s_sq >> 6), jnp.where(earlier, 1.0, 0.0), 0.0).astype(BF16)
        lw = lw_ref[...]
        h1 = lw.astype(BF16)
        r1 = lw - h1.astype(F32)
        h2 = r1.astype(BF16)
        h3 = (r1 - h2.astype(F32)).astype(BF16)
        cum = _dot(tri, h1) + _dot(tri, h2) + _dot(tri, h3)
        e_in = jnp.exp(cum)
        e_neg = jnp.exp(-cum)
        a_t = a_ref[...] * jnp.exp(cum - lw)
        r_t = r_ref[...] * e_in
        b_t = b_ref[...] * e_neg
        k_t = kd_ref[...] * e_neg
        v_all = v_ref[...]
        for ck in range(nk):
            rows = slice(ck * c, (ck + 1) * c)
            w_row = ck * c if rev else (ck + 1) * c - 1
            for g in range(D_WIDTH // HGROUP):
                cs = slice(g * HGROUP, (g + 1) * HGROUP)
                chains.append(dict(d=d, ck=ck, g=g, rev=rev, at=a_t[rows, cs], rt=r_t[rows, cs],
                                   bt=b_t[rows, cs], kt=k_t[rows, cs], vc=v_all[rows, cs],
                                   w=e_in[w_row:w_row + 1, cs]))

    def strict(ch):
        return (s_cat > t_cat) if ch['rev'] else (s_cat < t_cat)

    def incl(ch):
        return (s_cat >= t_cat) if ch['rev'] else (s_cat <= t_cat)

    for ch in chains:
        ch['lhs'] = jnp.concatenate([ch['at'], ch['rt']], axis=0).astype(BF16)
        ch['bd_v'] = block_diag(ch['vc'])
    m_b = [_dot_nt(ch['lhs'], block_diag(ch['bt'])) for ch in chains]
    m_k = [_dot_nt(ch['lhs'], block_diag(ch['kt'])) for ch in chains]
    for ch, mb, mk in zip(chains, m_b, m_k):
        ch['a_ab'] = jnp.where(strict(ch), mb[:c], 0.0)
        ch['a_rb'] = jnp.where(incl(ch), mb[c:], 0.0).astype(BF16)
        ch['a_ak'] = jnp.where(strict(ch), mk[:c], 0.0).astype(BF16)
        ch['a_rk'] = jnp.where(incl(ch), mk[c:], 0.0).astype(BF16)
        ch['t'] = eye_cat + jnp.where((t_cat >> 1) == (s_cat >> 1), ch['a_ab'], 0.0)
    m = 2
    while m < c:
        sh = int(math.log2(m))
        xs = []
        for ch in chains:
            a_m = jnp.where((t_cat >> (sh + 1)) == (s_cat >> (sh + 1)),
                            jnp.where((t_cat >> sh) != (s_cat >> sh), ch['a_ab'], 0.0), 0.0)
            xs.append(_dot(a_m.astype(BF16), block_diag(ch['t'])))
        for ch, x in zip(chains, xs):
            ch['t'] = ch['t'] + _dot(ch['t'].astype(BF16), block_diag(x))
        m *= 2
    akv = [_dot(ch['a_ak'], ch['bd_v']) for ch in chains]
    for ch in chains:
        ch['t'] = ch['t'].astype(BF16)
    a_hat = [_dot(ch['t'], block_diag(ch['at'])) for ch in chains]
    u_hat = [_dot(ch['t'], block_diag(x)) for ch, x in zip(chains, akv)]
    r_hat = [(ch['rt'] + _dot(ch['a_rb'], block_diag(x))).astype(BF16) for ch, x in zip(chains, a_hat)]
    o_hat = [_dot(ch['a_rb'], block_diag(x)) + _dot(ch['a_rk'], ch['bd_v']) for ch, x in zip(chains, u_hat)]
    m_c = [jnp.where(bd_mask, _dot_tn(x.astype(BF16), ch['bt'].astype(BF16)), 0.0).astype(BF16)
           for ch, x in zip(chains, a_hat)]
    n_c = [jnp.where(bd_mask, _dot_tn(jnp.concatenate([x, ch['vc']], axis=0).astype(BF16),
                                      jnp.concatenate([ch['bt'], ch['kt']], axis=0).astype(BF16)), 0.0)
           for ch, x in zip(chains, u_hat)]
    pre = {(ch['d'], ch['ck'], ch['g']): (r_hat[i], o_hat[i], m_c[i], n_c[i], ch['w'])
           for i, ch in enumerate(chains)}

    groups = [(d, g) for d in range(2) for g in range(D_WIDTH // HGROUP)]
    states = {dg: s_scr[dg[0], dg[1]] for dg in groups}
    for i in range(nk):
        st_b = {dg: states[dg].astype(BF16) for dg in groups}
        for d, g in groups:
            ck = nk - 1 - i if d == 1 else i
            rh, oh, mc, nc_, w = pre[(d, ck, g)]
            o_ref = ob_ref if d == 1 else of_ref
            o_ref[ck * c:(ck + 1) * c, g * HGROUP:(g + 1) * HGROUP] = _dot_nt(rh, st_b[(d, g)]) + oh
            states[(d, g)] = (states[(d, g)] + _dot(st_b[(d, g)], mc) + nc_) * w
    for d, g in groups:
        s_scr[d, g] = states[(d, g)]


def _d_post_kernel(of_ref, ob_ref, bonus_ref, g_ref, lng_ref, lnb_ref, bs_ref, o_ref):
    o = of_ref[...] + ob_ref[...]
    bs = bs_ref[...]

    def head_mean(x):
        return jnp.concatenate(
            [_group_mean(x[:, s * HGROUP:(s + 1) * HGROUP], bs) for s in range(D_WIDTH // HGROUP)], axis=1)

    cen = o - head_mean(o)
    var = head_mean(cen * cen)
    y = cen * lax.rsqrt(var + LNX_EPS) * lng_ref[...] + lnb_ref[...] + bonus_ref[...]
    o_ref[...] = (y * g_ref[...]).astype(o_ref.dtype)


def _scan_tables(segments, rows):
    fwd, bwd, first = [], [], []
    for start, bsz, lp in segments:
        nc = lp // rows
        for b in range(bsz):
            base = (start + b * lp) // rows
            for ci in range(nc):
                fwd.append(base + ci)
                bwd.append(base + nc - 1 - ci)
                first.append(1 if ci == 0 else 0)
    as_i32 = lambda t: jnp.asarray(np.asarray(t, np.int32))
    return as_i32(fwd), as_i32(bwd), as_i32(first)


def _scan(segments, n_rows, r, v, a, lwf, lwb, kdf, kdb, bf_, bb_):
    nk = SCAN_CHUNKS_PER_STEP
    rows = nk * SCAN_CHUNK
    if any(start % rows or lp % rows for start, _, lp in segments):
        nk, rows = 1, SCAN_CHUNK
    fwd, bwd, first = _scan_tables(segments, rows)
    nsteps = fwd.shape[0]
    blk = (rows, D_WIDTH)
    fspec = pl.BlockSpec(blk, lambda s, fw, bw, fi: (fw[s], 0))
    bspec = pl.BlockSpec(blk, lambda s, fw, bw, fi: (bw[s], 0))
    grid_spec = pltpu.PrefetchScalarGridSpec(
        num_scalar_prefetch=3, grid=(nsteps,),
        in_specs=[fspec] * 6 + [bspec] * 6,
        out_specs=[fspec, bspec],
        scratch_shapes=[pltpu.VMEM((2, D_WIDTH // HGROUP, HGROUP, HGROUP), F32)],
    )
    return pl.pallas_call(
        functools.partial(_scan_kernel, nk=nk), grid_spec=grid_spec,
        out_shape=[jax.ShapeDtypeStruct((n_rows, D_WIDTH), F32)] * 2,
        compiler_params=_cparams(("arbitrary",)), name="wkv_scan",
    )(fwd, bwd, first, r, v, a, lwf, kdf, bf_, r, v, a, lwb, kdb, bb_)


def _rope_tables(segments, n_rows, layer_kind):
    outs = []
    for start, bsz, lp in segments:
        p = jnp.arange(lp)
        d = jnp.arange(HEAD_DIM)
        if layer_kind == "axial":
            t = p - FRONT
            row = jnp.where(t >= 0, t // GRID_W, jnp.where(p >= PAD, t, 0)).astype(F32)
            col = jnp.where(t >= 0, t % GRID_W, jnp.where(p >= PAD, t, 0)).astype(F32)
            half = HEAD_DIM // 2
            inv = A_THETA ** (-jnp.arange(0, half, 2, dtype=F32) / half)
            ang = jnp.concatenate([row[:, None] * inv] * 2 + [col[:, None] * inv] * 2, axis=1)
            first = (d % 32) < 16
            cos = jnp.cos(ang)
            sa = jnp.where(first[None, :], -jnp.sin(ang), 0.0)
            sb = jnp.where(first[None, :], 0.0, jnp.sin(ang))
        else:
            pos = jnp.maximum(p - PAD, 0).astype(F32)
            inv = ROPE_THETA ** (-jnp.arange(0, ROPE_DIMS, 2, dtype=F32) / ROPE_DIMS)
            ang8 = pos[:, None] * inv
            ang = jnp.concatenate([ang8, ang8] + [jnp.zeros_like(ang8)] * 6, axis=1)
            cos = jnp.where((d < ROPE_DIMS)[None, :], jnp.cos(ang), 1.0)
            sa = jnp.where((d < 8)[None, :], -jnp.sin(ang), 0.0)
            sb = jnp.where(((d >= 8) & (d < 16))[None, :], jnp.sin(ang), 0.0)
        tabs = [jnp.tile(jnp.concatenate([t_, t_], axis=1), (bsz, 1)) for t_ in (cos, sa, sb)]
        outs.append(tabs)
    tail = n_rows - sum(b * lp for _, b, lp in segments)
    res = []
    for idx in range(3):
        parts = [o[idx] for o in outs] + [jnp.zeros((tail, LANES), F32)]
        res.append(jnp.concatenate(parts, axis=0))
    return res


def _seq_view(flat, seg, width):
    start, bsz, lp = seg
    return flat[start:start + bsz * lp].reshape(bsz, lp, width)


def _kv_layouts(q, k, v, seg, n_kv, chunked):
    bsz, lp = seg[1], seg[2]
    qT = jnp.transpose(_seq_view(q, seg, q.shape[1]), (0, 2, 1))
    k4 = jnp.transpose(_seq_view(k, seg, k.shape[1]).reshape(bsz, lp, n_kv, HEAD_DIM), (0, 2, 1, 3))
    vT = jnp.transpose(_seq_view(v, seg, v.shape[1]).reshape(bsz, lp, n_kv, HEAD_DIM), (0, 2, 3, 1))
    aug = jnp.concatenate([vT, jnp.ones((bsz, n_kv, 1, lp), BF16),
                           jnp.zeros((bsz, n_kv, V_ROWS - HEAD_DIM - 1, lp), BF16)], axis=2)
    if chunked:
        aug = jnp.transpose(aug.reshape(bsz, n_kv, V_ROWS, lp // ATTN_CHUNK, ATTN_CHUNK), (0, 1, 3, 2, 4))
    return qT, k4, aug


def _to_flat(parts, n_rows, width, dtype):
    used = sum(p.shape[0] for p in parts)
    return jnp.concatenate(parts + [jnp.zeros((n_rows - used, width), dtype)], axis=0)


def _block_avg(n, group):
    idx = np.arange(n) // group
    return jnp.asarray((idx[:, None] == idx[None, :]).astype(np.float32) / group).astype(BF16)


def _forward(xs, p):
    segments = []
    start = 0
    for x in xs:
        bsz, s, _ = x.shape
        lp = s + FRONT
        segments.append((start, bsz, lp))
        start += bsz * lp
    n_used = start
    tm = 512 if n_used >= 4096 else LANES
    n_rows = _round_up(n_used, tm)

    meta = p['meta_tokens'].astype(F32)
    parts = []
    for x in xs:
        bsz = x.shape[0]
        lead = jnp.concatenate([jnp.zeros((PAD, D_MODEL), F32), meta], axis=0)
        parts.append(jnp.concatenate([jnp.broadcast_to(lead[None], (bsz, FRONT, D_MODEL)), x], axis=1)
                     .reshape(-1, D_MODEL))
    h = _to_flat(parts, n_rows, D_MODEL, F32)

    bs = _block_avg(HGROUP, HEAD_DIM)
    row = lambda t: t.reshape(1, -1).astype(F32)

    depth = p['pre_mix_g'].shape[0]
    for i in range(depth):
        if i % 2 == 0:
            e = i // 2
            cos, sa, sb = _rope_tables(segments, n_rows, "axial")
            ch = np.arange(B_DIM)
            ang = 2.0 * np.pi * ((ch[:, None] * ch[None, :]) % B_DIM) / B_DIM
            cc, sc = jnp.asarray(np.cos(ang), F32), jnp.asarray(np.sin(ang), F32)
            wl = p['b_w'][e].astype(F32)
            hp = lax.Precision.HIGHEST
            pmat = jnp.einsum('cd,gde->gce', cc, wl, precision=hp)
            qmat = -jnp.einsum('cd,gde->gce', sc, wl, precision=hp)
            bd = lambda m: jax.scipy.linalg.block_diag(*[m[g] for g in range(B_GROUPS)])
            pq_base = jnp.concatenate([bd(pmat), bd(qmat)], axis=1)
            q, k, v, gf = _in_even_call(h, cos, sa, sb, p, i, e, bs, pq_base, n_rows, tm)
            ya_parts, yb_parts = [], []
            for seg in segments:
                _, bsz, lp = seg
                qT, k4, vT5 = _kv_layouts(q, k, v, seg, A_KV_HEADS, True)
                oT = _attn_a(qT, k4, vT5)
                ya_parts.append(jnp.transpose(oT, (0, 2, 1)).reshape(bsz * lp, A_Q))
                gseq = _seq_view(gf, seg, 2 * B_W)[:, PAD:]
                yb = _mixer_b(gseq, row(p['b_b'][e]))
                yb_parts.append(jnp.pad(yb, ((0, 0), (PAD, 0), (0, 0))).reshape(bsz * lp, B_W).astype(BF16))
            ya = _to_flat(ya_parts, n_rows, A_Q, BF16)
            yb = _to_flat(yb_parts, n_rows, B_W, BF16)
            w_out = p['even_w_out'][e].astype(BF16)
            wa, wb = w_out[:A_Q], w_out[A_Q:]
        else:
            o = i // 2
            cos, sa, sb = _rope_tables(segments, n_rows, "partial")
            q, k, v, u = _row_call(
                "in_odd", _in_odd_kernel, n_rows, tm, [h, cos, sa, sb],
                [row(p['pre_mix_g'][i]), p['odd_w_in'][o].astype(BF16)],
                [(C_Q, BF16), (C_KV, BF16), (C_KV, BF16), (D_IN, F32)])
            yc_parts = []
            for seg in segments:
                _, bsz, lp = seg
                qT, k4, vT4 = _kv_layouts(q, k, v, seg, C_KV_HEADS, False)
                oT = _attn_c(p['c_sink'][o].astype(F32), qT, k4, vT4)
                yc_parts.append(jnp.transpose(oT, (0, 2, 1)).reshape(bsz * lp, C_Q))
            ya = _to_flat(yc_parts, n_rows, C_Q, BF16)
            yb = _mixer_d(u, p, o, bs, segments, n_rows, tm)
            w_out = p['odd_w_out'][o].astype(BF16)
            wa, wb = w_out[:C_Q], w_out[C_Q:]
        (h,) = _row_call("out_proj", _out_proj_kernel, n_rows, tm, [h, ya, yb],
                         [wa, wb, row(p['post_mix_g'][i])], [(D_MODEL, F32)])
        (h,) = _row_call(
            "ffn", functools.partial(_ffn_kernel, chunk=256), n_rows, tm, [h],
            [row(p['pre_ffn_g'][i]), p['ffn_w_gate'][i].astype(BF16), p['ffn_w_up'][i].astype(BF16),
             p['ffn_w_down'][i].astype(BF16), row(p['post_ffn_g'][i])],
            [(D_MODEL, F32)], scratch=[pltpu.VMEM((tm, D_MODEL), F32)])

    outs = []
    for seg in segments:
        outs.append(_seq_view(h, seg, D_MODEL)[:, FRONT:])
    return tuple(outs)


def _in_even_call(h, cos, sa, sb, p, i, e, bs, pq_base, n_rows, tm):
    row = lambda t: t.reshape(1, -1).astype(F32)
    pq = pq_base.astype(BF16)
    grid = (n_rows // tm,)
    consts = [row(p['pre_mix_g'][i]), p['even_w_in'][e].astype(BF16),
              row(jnp.tile(p['a_q_gain'][e], A_HEADS) * (HEAD_DIM ** -0.5 * LOG2_E)),
              row(jnp.tile(p['a_k_gain'][e], A_KV_HEADS)), row(p['b_norm_g'][e]), bs, pq]
    in_specs = [pl.BlockSpec((tm, D_MODEL), lambda r: (r, 0))]
    in_specs += [pl.BlockSpec(a.shape, lambda r, nd=a.ndim: (0,) * nd) for a in consts]
    in_specs += [pl.BlockSpec((tm, LANES), lambda r: (r, 0))] * 3
    outs = [(A_Q, BF16), (A_KV, BF16), (A_KV, BF16), (2 * B_W, F32)]
    return pl.pallas_call(
        _in_even_kernel, grid=grid, in_specs=in_specs,
        out_specs=[pl.BlockSpec((tm, n), lambda r: (r, 0)) for n, _ in outs],
        out_shape=[jax.ShapeDtypeStruct((n_rows, n), dt) for n, dt in outs],
        compiler_params=_cparams(("parallel",)), name="in_even",
    )(h, *consts, cos, sa, sb)


def _mixer_d(u, p, o, bs, segments, n_rows, tm):
    row = lambda t: t.reshape(1, -1).astype(F32)
    w = D_WIDTH
    zeros = lambda r: jnp.zeros((r, w), F32)
    w2 = jnp.concatenate([jnp.concatenate([p['d_w_up'][o][0], zeros(DECAY_RANK)], axis=1),
                          jnp.concatenate([zeros(DECAY_RANK), p['d_w_up'][o][1]], axis=1)], axis=0)
    a2 = jnp.concatenate([jnp.concatenate([p['d_a_up'][o][0], zeros(ICLR_RANK)], axis=1),
                          jnp.concatenate([zeros(ICLR_RANK), p['d_a_up'][o][1]], axis=1)], axis=0)
    consts = [row(p['d_mu_prev'][o]), row(p['d_mu_next'][o]), w2.astype(BF16), a2.astype(BF16),
              p['d_g_up'][o].astype(BF16), row(p['d_w0'][o]), row(p['d_a0'][o]),
              row(p['d_k_k'][o]), row(p['d_k_a'][o]), row(p['d_r_k'][o]), bs]
    nb8 = n_rows // 8
    t8 = tm // 8
    in_specs = [pl.BlockSpec((tm, D_IN), lambda i: (i, 0)),
                pl.BlockSpec((8, D_IN), lambda i: (jnp.maximum(i * t8 - 1, 0), 0)),
                pl.BlockSpec((8, D_IN), lambda i: (jnp.minimum((i + 1) * t8, nb8 - 1), 0))]
    in_specs += [pl.BlockSpec(a.shape, lambda i, nd=a.ndim: (0,) * nd) for a in consts]
    n_out = 11
    prep = pl.pallas_call(
        functools.partial(_d_prep_kernel, tm=tm, segments=tuple(segments)),
        grid=(n_rows // tm,), in_specs=in_specs,
        out_specs=[pl.BlockSpec((tm, w), lambda i: (i, 0))] * n_out,
        out_shape=[jax.ShapeDtypeStruct((n_rows, w), F32)] * n_out,
        compiler_params=_cparams(("parallel",)), name="d_prep",
    )(u, u, u, *consts)
    r, v, a, lwf, lwb, kdf, kdb, bf_, bb_, bonus, g = prep
    of, ob = _scan(segments, n_rows, r, v, a, lwf, lwb, kdf, kdb, bf_, bb_)
    (yd,) = _row_call("d_post", _d_post_kernel, n_rows, tm, [of, ob, bonus, g],
                      [row(p['d_ln_g'][o]), row(p['d_ln_b'][o]), bs], [(w, BF16)])
    return yd


def kernel(x_prompt, x_sample, meta_tokens, pre_mix_g, post_mix_g, pre_ffn_g, post_ffn_g, even_w_in, even_w_out, a_q_gain, a_k_gain, b_norm_g, b_w, b_b, odd_w_in, odd_w_out, c_sink, d_mu_prev, d_mu_next, d_w0, d_w_up, d_a0, d_a_up, d_g_up, d_k_k, d_k_a, d_r_k, d_ln_g, d_ln_b, ffn_w_gate, ffn_w_up, ffn_w_down):
    params = dict(meta_tokens=meta_tokens, pre_mix_g=pre_mix_g, post_mix_g=post_mix_g,
                  pre_ffn_g=pre_ffn_g, post_ffn_g=post_ffn_g,
                  even_w_in=even_w_in, even_w_out=even_w_out, a_q_gain=a_q_gain, a_k_gain=a_k_gain,
                  b_norm_g=b_norm_g, b_w=b_w, b_b=b_b,
                  odd_w_in=odd_w_in, odd_w_out=odd_w_out, c_sink=c_sink,
                  d_mu_prev=d_mu_prev, d_mu_next=d_mu_next, d_w0=d_w0, d_w_up=d_w_up,
                  d_a0=d_a0, d_a_up=d_a_up, d_g_up=d_g_up, d_k_k=d_k_k, d_k_a=d_k_a, d_r_k=d_r_k,
                  d_ln_g=d_ln_g, d_ln_b=d_ln_b,
                  ffn_w_gate=ffn_w_gate, ffn_w_up=ffn_w_up, ffn_w_down=ffn_w_down)
    return _forward([x_prompt, x_sample], params)
```

```python
import functools
import math

import numpy as np
import jax
import jax.numpy as jnp
from jax import lax
from jax.experimental import pallas as pl
from jax.experimental.pallas import tpu as pltpu

F32 = jnp.float32
BF16 = jnp.bfloat16

D_MODEL = 1024
HEAD_DIM = 64
N_META = 16
GRID_W = 64
WINDOW = 128
RMS_EPS = 1e-6
A_HEADS, A_KV_HEADS, A_THETA = 12, 4, 10000.0
B_GROUPS, B_DIM = 4, 64
C_HEADS, C_KV_HEADS = 8, 2
ROPE_THETA = 500000.0
ROPE_DIMS = HEAD_DIM // 4
D_HEADS = 8
D_WIDTH = D_HEADS * HEAD_DIM
DECAY_RANK, ICLR_RANK, GATE_RANK = 64, 64, 128
LNX_EPS = 64e-5
D_FF = 2816
A_Q, A_KV, B_W = A_HEADS * HEAD_DIM, A_KV_HEADS * HEAD_DIM, B_GROUPS * B_DIM
EVEN_IN = A_Q + 2 * A_KV + B_W
C_Q, C_KV = C_HEADS * HEAD_DIM, C_KV_HEADS * HEAD_DIM
D_IN = 3 * D_WIDTH + 2 * DECAY_RANK + 2 * ICLR_RANK + GATE_RANK
ODD_IN = C_Q + 2 * C_KV + D_IN

LANES = 128
FRONT = 256
PAD = FRONT - N_META
NEG = -1e30
SCAN_CHUNK = 64
SCAN_CHUNKS_PER_STEP = 4
HGROUP = 256
FFT_L1 = 16
VMEM_LIMIT = 56 * 1024 * 1024
ATTN_CHUNK = 256
ATTN_MAX_CHUNKS = 20
ATTN_UNROLL = 8
LOG2_E = math.log2(math.e)


def _round_up(x, m):
    return (x + m - 1) // m * m


def _cparams(sem):
    return pltpu.CompilerParams(dimension_semantics=sem, vmem_limit_bytes=VMEM_LIMIT)


def _dot(a, b):
    return jnp.dot(a, b, preferred_element_type=F32)


def _dot_nt(a, b):
    return lax.dot_general(a, b, (((1,), (1,)), ((), ())), preferred_element_type=F32)


def _dot_tn(a, b):
    return lax.dot_general(a, b, (((0,), (0,)), ((), ())), preferred_element_type=F32)


def _split2(x):
    hi = x.astype(BF16)
    lo = (x - hi.astype(F32)).astype(BF16)
    return hi, lo


def _group_mean(x, bs):
    hi, lo = _split2(x)
    return _dot(hi, bs) + _dot(lo, bs)


def _rope(x, cos, sin_a, sin_b, shift):
    n = x.shape[1]
    return x * cos + pltpu.roll(x, n - shift, 1) * sin_a + pltpu.roll(x, shift, 1) * sin_b


def _wide(t, n):
    return t if n == LANES else jnp.concatenate([t] * (n // LANES), axis=1)


def _rms_rows(x, g):
    ms = jnp.mean(x * x, axis=-1, keepdims=True)
    return x * lax.rsqrt(ms + RMS_EPS) * g


def _store_vt(v_ref, v):
    n = v.shape[1]
    for r in range(v.shape[0] // n):
        v_ref[r] = v[r * n:(r + 1) * n, :].T.astype(BF16)


def _in_even_kernel(h_ref, cos_ref, sa_ref, sb_ref, g_ref, w_ref, qg_ref, kg_ref, bg_ref, bs_ref, pq_ref,
                    q_ref, k_ref, v_ref, gf_ref):
    hn = _rms_rows(h_ref[...], g_ref[...]).astype(BF16)
    proj = _dot(hn, w_ref[...])
    bs = bs_ref[...]
    cos, sa, sb = (_wide(r[...], HGROUP) for r in (cos_ref, sa_ref, sb_ref))

    def norm_rope(x, gain):
        xn = x * lax.rsqrt(_group_mean(x * x, bs) + RMS_EPS) * gain
        return _rope(xn, cos, sa, sb, 16)

    for s in range(A_Q // HGROUP):
        cs = slice(s * HGROUP, (s + 1) * HGROUP)
        q_ref[cs, :] = norm_rope(proj[:, cs], qg_ref[:, cs]).T.astype(BF16)
    k_ref[...] = norm_rope(proj[:, A_Q:A_Q + A_KV], kg_ref[...]).astype(BF16)
    _store_vt(v_ref, proj[:, A_Q + A_KV:A_Q + 2 * A_KV])
    f = proj[:, A_Q + 2 * A_KV:]
    fn = f * lax.rsqrt(_group_mean(f * f, bs) + RMS_EPS) * bg_ref[...]
    gf_ref[...] = _dot(fn.astype(BF16), pq_ref[...])


def _in_odd_kernel(h_ref, cos_ref, sa_ref, sb_ref, g_ref, w_ref, q_ref, k_ref, v_ref, u_ref):
    hn = _rms_rows(h_ref[...], g_ref[...]).astype(BF16)
    proj = _dot(hn, w_ref[...])
    cos, sa, sb = cos_ref[...], sa_ref[...], sb_ref[...]
    cos2, sa2, sb2 = (_wide(t, HGROUP) for t in (cos, sa, sb))
    for s in range(C_Q // HGROUP):
        cs = slice(s * HGROUP, (s + 1) * HGROUP)
        q_ref[cs, :] = (_rope(proj[:, cs], cos2, sa2, sb2, 8) * (HEAD_DIM ** -0.5 * LOG2_E)).T.astype(BF16)
    k_ref[...] = _rope(proj[:, C_Q:C_Q + C_KV], cos, sa, sb, 8).astype(BF16)
    _store_vt(v_ref, proj[:, C_Q + C_KV:C_Q + 2 * C_KV])
    u_ref[...] = proj[:, C_Q + 2 * C_KV:]


def _out_proj_kernel(h_ref, yb_ref, wa_ref, wb_ref, g_ref, yat_ref, o_ref):
    mix = _dot_tn(yat_ref[...], wa_ref[...]) + _dot(yb_ref[...], wb_ref[...])
    o_ref[...] = h_ref[...] + _rms_rows(mix, g_ref[...])


def _ffn_kernel(h_ref, g1_ref, wg_ref, wu_ref, wd_ref, g2_ref, o_ref, acc_ref, *, chunk):
    h = h_ref[...]
    hn = _rms_rows(h, g1_ref[...]).astype(BF16)
    for c in range(D_FF // chunk):
        cs = slice(c * chunk, (c + 1) * chunk)
        gate = _dot(hn, wg_ref[:, cs])
        up = _dot(hn, wu_ref[:, cs])
        act = (gate * (1.0 / (1.0 + jnp.exp(-gate))) * up).astype(BF16)
        part = _dot(act, wd_ref[cs, :])
        if c == 0:
            acc_ref[...] = part
        else:
            acc_ref[...] += part
    o_ref[...] = h + _rms_rows(acc_ref[...], g2_ref[...])


def _row_call(name, kernel, n_rows, tm, row_ins, const_ins, outs, scratch=(), col_ins=()):
    grid = (n_rows // tm,)
    in_specs = [pl.BlockSpec((tm, a.shape[1]), lambda i: (i, 0)) for a in row_ins]
    in_specs += [pl.BlockSpec(a.shape, lambda i, nd=a.ndim: (0,) * nd) for a in const_ins]
    in_specs += [pl.BlockSpec((a.shape[0], tm), lambda i: (0, i)) for a in col_ins]
    out_specs, out_shape = [], []
    for kind, n, dt in outs:
        if kind == "rows":
            out_specs.append(pl.BlockSpec((tm, n), lambda i: (i, 0)))
            out_shape.append(jax.ShapeDtypeStruct((n_rows, n), dt))
        elif kind == "cols":
            out_specs.append(pl.BlockSpec((n, tm), lambda i: (0, i)))
            out_shape.append(jax.ShapeDtypeStruct((n, n_rows), dt))
        else:
            out_specs.append(pl.BlockSpec((tm // n, n, n), lambda i: (i, 0, 0)))
            out_shape.append(jax.ShapeDtypeStruct((n_rows // n, n, n), dt))
    return pl.pallas_call(
        kernel, grid=grid, in_specs=in_specs, out_specs=out_specs, out_shape=out_shape,
        scratch_shapes=list(scratch), compiler_params=_cparams(("parallel",)), name=name,
    )(*row_ins, *const_ins, *col_ins)


def _padded_q(q_ref, h, kv):
    qh = q_ref[h * HEAD_DIM:(h + 1) * HEAD_DIM, :]
    zero = jnp.zeros_like(qh)
    return jnp.concatenate([qh, zero] if kv % 2 == 0 else [zero, qh], axis=0)


def _attn_a_kernel(qT_ref, k_ref, vT_ref, o_ref, m_scr, l_scr, acc_scr, s_scr, *, nchunk, nkb, tq):
    j = pl.program_id(2)

    @pl.when(j == 0)
    def _():
        m_scr[...] = jnp.full(m_scr.shape, NEG, F32)
        l_scr[...] = jnp.zeros(l_scr.shape, F32)
        acc_scr[...] = jnp.zeros(acc_scr.shape, F32)

    group = A_HEADS // A_KV_HEADS
    ck = ATTN_CHUNK
    key_row = lax.broadcasted_iota(jnp.int32, (ck, tq), 0) + j * (nchunk * ck)

    def scores(h, qp, c):
        pair = (h // group) // 2
        kc = k_ref[0, pl.ds(pl.multiple_of(c * ck, ck), ck), pair * LANES:(pair + 1) * LANES]
        return _dot(kc, qp)

    def fold(s, op):
        return op(s.reshape(ck // 8, 8, tq), axis=0)

    def first_scores(h, qp):
        s0 = jnp.where(key_row >= PAD, scores(h, qp, 0), NEG)
        s_scr[h % 2, 0] = s0
        return fold(s0, jnp.max)

    def next_scores(h, qp, c, mx):
        s = scores(h, qp, c)
        s_scr[h % 2, c] = s
        return jnp.maximum(mx, fold(s, jnp.max))

    def weighted_values(h, c, m_new):
        p = jnp.exp2(s_scr[h % 2, c] - m_new)
        kv = h // group
        pv = _dot(vT_ref[0, c, kv * HEAD_DIM:(kv + 1) * HEAD_DIM, :], p.astype(BF16))
        return pv, fold(p, jnp.sum)

    unroll = max(u for u in range(1, ATTN_UNROLL + 1) if max(nchunk - 1, 1) % u == 0)
    qp = _padded_q(qT_ref, 0, 0)
    mx = lax.fori_loop(1, nchunk, functools.partial(next_scores, 0, qp), first_scores(0, qp), unroll=unroll)
    for h in range(A_HEADS):
        m_old = m_scr[h]
        m_new = jnp.maximum(m_old, jnp.max(mx, axis=0, keepdims=True))
        if h + 1 < A_HEADS:
            qp = _padded_q(qT_ref, h + 1, (h + 1) // group)

            def body(c, carry, h=h, m_new=m_new, qp=qp):
                acc, lsum, mxn = carry
                mxn = next_scores(h + 1, qp, c, mxn)
                pv, ps = weighted_values(h, c, m_new)
                return acc + pv, lsum + ps, mxn

            mx0 = first_scores(h + 1, qp)
            acc, lsum, mx = lax.fori_loop(1, nchunk, body, (*weighted_values(h, 0, m_new), mx0),
                                          unroll=unroll)
        else:
            def tail(c, carry, h=h, m_new=m_new):
                pv, ps = weighted_values(h, c, m_new)
                return carry[0] + pv, carry[1] + ps

            acc, lsum = lax.fori_loop(1, nchunk, tail, weighted_values(h, 0, m_new), unroll=unroll)
        alpha = jnp.exp2(m_old - m_new)
        acc_scr[h] = acc_scr[h] * alpha + acc
        l_scr[h] = l_scr[h] * alpha + jnp.sum(lsum, axis=0, keepdims=True)
        m_scr[h] = m_new

    @pl.when(j == nkb - 1)
    def _():
        for h in range(A_HEADS):
            o_ref[h * HEAD_DIM:(h + 1) * HEAD_DIM, :] = (acc_scr[h] / l_scr[h]).astype(o_ref.dtype)


def _attn_a(qT, k, vT, bsz):
    lp = k.shape[1]
    tq = ATTN_CHUNK
    total = lp // ATTN_CHUNK
    nkb = min(n for n in range(1, total + 1) if total % n == 0 and total // n <= ATTN_MAX_CHUNKS)
    tk = lp // nkb
    nchunk = tk // ATTN_CHUNK
    nq = lp // tq
    kernel = functools.partial(_attn_a_kernel, nchunk=nchunk, nkb=nkb, tq=tq)
    return pl.pallas_call(
        kernel, grid=(bsz, nq, nkb),
        in_specs=[
            pl.BlockSpec((A_Q, tq), lambda b, i, j: (0, b * nq + i)),
            pl.BlockSpec((1, tk, A_KV), lambda b, i, j: (b, j, 0)),
            pl.BlockSpec((1, nchunk, A_KV, ATTN_CHUNK), lambda b, i, j: (b, j, 0, 0)),
        ],
        out_specs=pl.BlockSpec((A_Q, tq), lambda b, i, j: (0, b * nq + i)),
        out_shape=jax.ShapeDtypeStruct((A_Q, bsz * lp), BF16),
        scratch_shapes=[
            pltpu.VMEM((A_HEADS, 1, tq), F32),
            pltpu.VMEM((A_HEADS, 1, tq), F32),
            pltpu.VMEM((A_HEADS, HEAD_DIM, tq), F32),
            pltpu.VMEM((2, nchunk, ATTN_CHUNK, tq), F32),
        ],
        compiler_params=_cparams(("parallel", "parallel", "arbitrary")), name="attn_a",
    )(qT, k, vT)


def _attn_c_kernel(sink_ref, qT_ref, k0, k1, k2, k3, v0, v1, v2, v3, o_ref, *, lp):
    j = pl.program_id(1)
    k_refs, v_refs = (k0, k1, k2, k3), (v0, v1, v2, v3)
    row = lax.broadcasted_iota(jnp.int32, (LANES, LANES), 0)
    pq = lax.broadcasted_iota(jnp.int32, (LANES, LANES), 1) + j * LANES
    biases = []
    for slot in range(4):
        if slot == 0:
            bias = jnp.where(row >= PAD % LANES, 0.0, NEG)
        else:
            pk = row + (j + slot - 2) * LANES
            in_window = jnp.where(jnp.abs(pq - pk) <= WINDOW, 0.0, NEG)
            bias = jnp.where(pk >= FRONT, jnp.where(pk < lp, in_window, NEG), NEG)
        biases.append(bias.astype(F32))
    group = C_HEADS // C_KV_HEADS
    for h in range(C_HEADS):
        kv = h // group
        qp = _padded_q(qT_ref, h, kv)
        sink = sink_ref[h] * LOG2_E
        ss = [_dot(k_refs[t][0], qp) + biases[t] for t in range(4)]
        m = jnp.maximum(jnp.maximum(ss[0], ss[1]), jnp.maximum(ss[2], ss[3]))
        m = jnp.maximum(jnp.max(m, axis=0, keepdims=True), sink)
        acc = jnp.zeros((HEAD_DIM, LANES), F32)
        denom = jnp.exp2(sink - m)
        for t in range(4):
            p = jnp.exp2(ss[t] - m)
            acc = acc + _dot(v_refs[t][0, 0, kv * HEAD_DIM:(kv + 1) * HEAD_DIM, :], p.astype(BF16))
            denom = denom + jnp.sum(p, axis=0, keepdims=True)
        o_ref[h * HEAD_DIM:(h + 1) * HEAD_DIM, :] = (acc / denom).astype(o_ref.dtype)


def _attn_c(sink, qT, k, vT, bsz):
    lp = k.shape[1]
    nb = lp // LANES
    kernel = functools.partial(_attn_c_kernel, lp=lp)

    def kspec(fn):
        return pl.BlockSpec((1, LANES, C_KV), lambda b, j: (b, fn(j), 0))

    def vspec(fn):
        return pl.BlockSpec((1, 1, C_KV, LANES), lambda b, j: (b, fn(j), 0, 0))

    fns = (lambda j: PAD // LANES, lambda j: jnp.maximum(j - 1, 0), lambda j: j,
           lambda j: jnp.minimum(j + 1, nb - 1))
    return pl.pallas_call(
        kernel, grid=(bsz, nb),
        in_specs=[pl.BlockSpec(memory_space=pltpu.SMEM),
                  pl.BlockSpec((C_Q, LANES), lambda b, j: (0, b * nb + j))]
                 + [kspec(f) for f in fns] + [vspec(f) for f in fns],
        out_specs=pl.BlockSpec((C_Q, LANES), lambda b, j: (0, b * nb + j)),
        out_shape=jax.ShapeDtypeStruct((C_Q, bsz * lp), BF16),
        compiler_params=_cparams(("parallel", "parallel")), name="attn_c",
    )(sink, qT, k, k, k, k, vT, vT, vT, vT)


def _fft1_kernel(x_ref, c1_ref, s1_ref, o_ref):
    x = x_ref[0]
    tn = x.shape[1]
    lane = lax.broadcasted_iota(jnp.int32, x.shape, 1)
    is_re = (lane & (2 * B_W - 1)) < B_W
    xs = jnp.where(is_re, pltpu.roll(x, tn - B_W, 1), -pltpu.roll(x, B_W, 1))
    o_ref[0] = _dot(c1_ref[...], x.astype(BF16)) + _dot(s1_ref[...], xs.astype(BF16))


def _fft2_kernel(a_ref, tc_ref, ts_ref, c2_ref, s2_ref, bias_ref, o_ref):
    a = a_ref[0, 0]
    tc, ts = _wide(tc_ref[0], B_W), _wide(ts_ref[0], B_W)
    are, aim = a[:, :B_W], a[:, B_W:]
    bre = (are * tc + aim * ts).astype(BF16)
    bim = (aim * tc - are * ts).astype(BF16)
    o_ref[0] = _dot(c2_ref[...], bre) + _dot(s2_ref[...], bim) + bias_ref[...]


def _dft_tables(l2, l2p, l2o):
    l1 = FFT_L1
    length = l1 * l2
    n1 = np.arange(l1)
    ang1 = 2.0 * np.pi * ((n1[:, None] * n1[None, :]) % l1) / l1
    c1, s1 = np.cos(ang1), np.sin(ang1)
    n2 = np.arange(l2)
    angt = 2.0 * np.pi * (n1[:, None] * n2[None, :]) / length
    tc = np.zeros((l1, l2p, LANES), np.float32)
    ts = np.zeros((l1, l2p, LANES), np.float32)
    tc[:, :l2, :] = np.cos(angt)[:, :, None]
    ts[:, :l2, :] = np.sin(angt)[:, :, None]
    ang2 = 2.0 * np.pi * ((n2[:, None] * n2[None, :]) % l2) / l2
    scale = 1.0 / math.sqrt(B_DIM * length)
    c2 = np.zeros((l2o, l2p), np.float32)
    s2 = np.zeros((l2o, l2p), np.float32)
    c2[:l2, :l2] = np.cos(ang2) * scale
    s2[:l2, :l2] = np.sin(ang2) * scale
    as_bf = lambda t: jnp.asarray(t, F32).astype(BF16)
    return as_bf(c1), as_bf(s1), jnp.asarray(tc), jnp.asarray(ts), as_bf(c2), as_bf(s2)


def _mixer_b(gseq, bias):
    bsz, length, _ = gseq.shape
    l1 = FFT_L1
    l2 = length // l1
    l2p, l2o = _round_up(l2, LANES), _round_up(l2, 8)
    c1, s1, tc, ts, c2, s2 = _dft_tables(l2, l2p, l2o)
    x = jnp.pad(gseq.reshape(bsz, l1, l2, 2 * B_W), ((0, 0), (0, 0), (0, l2p - l2), (0, 0)))
    x = x.reshape(bsz, l1, l2p * 2 * B_W)
    ncol = l2p * 2 * B_W
    tn = 2 * B_W * 48 if l2p % 48 == 0 else 2 * B_W
    a = pl.pallas_call(
        _fft1_kernel, grid=(bsz, ncol // tn),
        in_specs=[pl.BlockSpec((1, l1, tn), lambda b, i: (b, 0, i)),
                  pl.BlockSpec((l1, l1), lambda b, i: (0, 0)),
                  pl.BlockSpec((l1, l1), lambda b, i: (0, 0))],
        out_specs=pl.BlockSpec((1, l1, tn), lambda b, i: (b, 0, i)),
        out_shape=jax.ShapeDtypeStruct((bsz, l1, ncol), F32),
        compiler_params=_cparams(("parallel", "parallel")), name="fft1",
    )(x, c1, s1)
    a = a.reshape(bsz, l1, l2p, 2 * B_W)
    y = pl.pallas_call(
        _fft2_kernel, grid=(bsz, l1),
        in_specs=[pl.BlockSpec((1, 1, l2p, 2 * B_W), lambda b, k: (b, k, 0, 0)),
                  pl.BlockSpec((1, l2p, LANES), lambda b, k: (k, 0, 0)),
                  pl.BlockSpec((1, l2p, LANES), lambda b, k: (k, 0, 0)),
                  pl.BlockSpec((l2o, l2p), lambda b, k: (0, 0)),
                  pl.BlockSpec((l2o, l2p), lambda b, k: (0, 0)),
                  pl.BlockSpec((1, B_W), lambda b, k: (0, 0))],
        out_specs=pl.BlockSpec((1, l2o, B_W), lambda b, k: (b, 0, k)),
        out_shape=jax.ShapeDtypeStruct((bsz, l2o, l1 * B_W), F32),
        compiler_params=_cparams(("parallel", "parallel")), name="fft2",
    )(a, tc, ts, c2, s2, bias)
    return y[:, :l2].reshape(bsz, length, B_W)


def _seq_position(rows, segments):
    pos = jnp.full(rows.shape, -1.0, F32)
    seqlen = jnp.full(rows.shape, 1.0, F32)
    for start, bsz, lp in segments:
        rel = rows - float(start)
        q = jnp.floor((rel + 0.5) * (1.0 / lp))
        inside = jnp.where(rel >= 0.0, jnp.where(rel < float(bsz * lp), 1.0, 0.0), 0.0) > 0.5
        pos = jnp.where(inside, rel - q * lp, pos)
        seqlen = jnp.where(inside, float(lp), seqlen)
    return pos, seqlen


def _sigmoid(x):
    return 1.0 / (1.0 + jnp.exp(-x))


def _d_prep_kernel(u_ref, up_ref, un_ref, mup_ref, mun_ref, w2_ref, a2_ref, gup_ref, w0_ref, a0_ref,
                   kk_ref, ka_ref, rk_ref, bs_ref,
                   r_ref, v_ref, a_ref, lwf_ref, lwb_ref, kdf_ref, kdb_ref, bf_ref, bb_ref,
                   bonus_ref, g_ref, *, tm, segments):
    i = pl.program_id(0)
    u = u_ref[...]
    rows = (lax.broadcasted_iota(jnp.int32, (tm, 1), 0) + i * tm).astype(F32)
    pos, seqlen = _seq_position(rows, segments)
    local = lax.broadcasted_iota(jnp.int32, (tm, 1), 0)
    u_prev = jnp.where(local == 0, up_ref[7:8, :], pltpu.roll(u, 1, 0))
    u_next = jnp.where(local == tm - 1, un_ref[0:1, :], pltpu.roll(u, tm - 1, 0))
    u_prev = jnp.where(pos == float(PAD), 0.0, u_prev)
    u_next = jnp.where(pos == seqlen - 1.0, 0.0, u_next)
    u = u + mup_ref[...] * (u_prev - u) + mun_ref[...] * (u_next - u)
    valid = jnp.where(pos >= float(PAD), 1.0, 0.0)

    w = D_WIDTH
    r, k, v = u[:, :w], u[:, w:2 * w], u[:, 2 * w:3 * w]
    c0 = 3 * w
    dec = _dot(jnp.tanh(u[:, c0:c0 + 2 * DECAY_RANK]).astype(BF16), w2_ref[...]) + w0_ref[...]
    c0 += 2 * DECAY_RANK
    icl = _dot(u[:, c0:c0 + 2 * ICLR_RANK].astype(BF16), a2_ref[...]) + a0_ref[...]
    c0 += 2 * ICLR_RANK
    g_ref[...] = _dot(_sigmoid(u[:, c0:c0 + GATE_RANK]).astype(BF16), gup_ref[...])

    bs = bs_ref[...]

    def head_sum(x):
        return jnp.concatenate(
            [_group_mean(x[:, s * HGROUP:(s + 1) * HGROUP], bs) for s in range(w // HGROUP)], axis=1
        ) * float(HEAD_DIM)

    kk = k * kk_ref[...]
    kk = kk * lax.rsqrt(jnp.maximum(head_sum(kk * kk), 1e-24))
    r_ref[...] = r
    v_ref[...] = v
    a_ref[...] = -kk * valid
    bonus = jnp.zeros_like(r)
    for d, (lw_ref, kd_ref, b_ref) in enumerate(((lwf_ref, kdf_ref, bf_ref), (lwb_ref, kdb_ref, bb_ref))):
        x = -dec[:, d * w:(d + 1) * w]
        softplus = jnp.maximum(x, 0.0) + jnp.log(1.0 + jnp.exp(-jnp.abs(x)))
        lw_ref[...] = -jnp.exp(-softplus - 0.5)
        gate = _sigmoid(icl[:, d * w:(d + 1) * w])
        kd = k * (1.0 + (gate - 1.0) * ka_ref[...])
        kd_ref[...] = kd * valid
        b_ref[...] = kk * gate * valid
        bonus = bonus + head_sum(r * kd * rk_ref[...]) * v
    bonus_ref[...] = bonus


def _scan_kernel(fwd_blk, bwd_blk, first,
                 rf, vf, af, lwf, kdf, bf_, rb, vb, ab, lwb, kdb, bb_,
                 of_ref, ob_ref, s_scr, *, nk):
    del fwd_blk, bwd_blk
    step = pl.program_id(0)
    c = SCAN_CHUNK

    @pl.when(first[step] == 1)
    def _():
        s_scr[...] = jnp.zeros(s_scr.shape, F32)

    t_sq = lax.broadcasted_iota(jnp.int32, (nk * c, nk * c), 0)
    s_sq = lax.broadcasted_iota(jnp.int32, (nk * c, nk * c), 1)
    t_cat = lax.broadcasted_iota(jnp.int32, (c, HGROUP), 0)
    s_cat = lax.broadcasted_iota(jnp.int32, (c, HGROUP), 1) & (c - 1)
    bd_row = lax.broadcasted_iota(jnp.int32, (HGROUP, HGROUP), 0) >> 6
    bd_col = lax.broadcasted_iota(jnp.int32, (HGROUP, HGROUP), 1) >> 6
    bd_mask = bd_row == bd_col
    eye_cat = jnp.where(t_cat == s_cat, 1.0, 0.0).astype(F32)

    def block_diag(x):
        return jnp.where(bd_mask, jnp.concatenate([x] * (HGROUP // c), axis=0), 0.0).astype(BF16)

    chains = []
    dirs = ((0, rf, vf, af, lwf, kdf, bf_), (1, rb, vb, ab, lwb, kdb, bb_))
    for d, r_ref, v_ref, a_ref, lw_ref, kd_ref, b_ref in dirs:
        rev = d == 1
        earlier = (s_sq >= t_sq) if rev else (s_sq <= t_sq)
        tri = jnp.where((t_sq >> 6) == (s_sq >> 6), jnp.where(earlier, 1.0, 0.0), 0.0).astype(BF16)
        lw = lw_ref[...]
        h1 = lw.astype(BF16)
        r1 = lw - h1.astype(F32)
        h2 = r1.astype(BF16)
        h3 = (r1 - h2.astype(F32)).astype(BF16)
        cum = _dot(tri, h1) + _dot(tri, h2) + _dot(tri, h3)
        e_in = jnp.exp(cum)
        e_neg = jnp.exp(-cum)
        a_t = a_ref[...] * jnp.exp(cum - lw)
        r_t = r_ref[...] * e_in
        b_t = b_ref[...] * e_neg
        k_t = kd_ref[...] * e_neg
        v_all = v_ref[...]
        for ck in range(nk):
            rows = slice(ck * c, (ck + 1) * c)
            w_row = ck * c if rev else (ck + 1) * c - 1
            for g in range(D_WIDTH // HGROUP):
                cs = slice(g * HGROUP, (g + 1) * HGROUP)
                chains.append(dict(d=d, ck=ck, g=g, rev=rev, at=a_t[rows, cs], rt=r_t[rows, cs],
                                   bt=b_t[rows, cs], kt=k_t[rows, cs], vc=v_all[rows, cs],
                                   w=e_in[w_row:w_row + 1, cs]))

    def strict(ch):
        return (s_cat > t_cat) if ch['rev'] else (s_cat < t_cat)

    def incl(ch):
        return (s_cat >= t_cat) if ch['rev'] else (s_cat <= t_cat)

    for ch in chains:
        ch['lhs'] = jnp.concatenate([ch['at'], ch['rt']], axis=0).astype(BF16)
        ch['bd_v'] = block_diag(ch['vc'])
    m_b = [_dot_nt(ch['lhs'], block_diag(ch['bt'])) for ch in chains]
    m_k = [_dot_nt(ch['lhs'], block_diag(ch['kt'])) for ch in chains]
    for ch, mb, mk in zip(chains, m_b, m_k):
        ch['a_ab'] = jnp.where(strict(ch), mb[:c], 0.0)
        ch['a_rb'] = jnp.where(incl(ch), mb[c:], 0.0).astype(BF16)
        ch['a_ak'] = jnp.where(strict(ch), mk[:c], 0.0).astype(BF16)
        ch['a_rk'] = jnp.where(incl(ch), mk[c:], 0.0).astype(BF16)
        ch['t'] = eye_cat + jnp.where((t_cat >> 1) == (s_cat >> 1), ch['a_ab'], 0.0)
    m = 2
    while m < c:
        sh = int(math.log2(m))
        xs = []
        for ch in chains:
            a_m = jnp.where((t_cat >> (sh + 1)) == (s_cat >> (sh + 1)),
                            jnp.where((t_cat >> sh) != (s_cat >> sh), ch['a_ab'], 0.0), 0.0)
            xs.append(_dot(a_m.astype(BF16), block_diag(ch['t'])))
        for ch, x in zip(chains, xs):
            ch['t'] = ch['t'] + _dot(ch['t'].astype(BF16), block_diag(x))
        m *= 2
    akv = [_dot(ch['a_ak'], ch['bd_v']) for ch in chains]
    for ch in chains:
        ch['t'] = ch['t'].astype(BF16)
    a_hat = [_dot(ch['t'], block_diag(ch['at'])) for ch in chains]
    u_hat = [_dot(ch['t'], block_diag(x)) for ch, x in zip(chains, akv)]
    r_hat = [(ch['rt'] + _dot(ch['a_rb'], block_diag(x))).astype(BF16) for ch, x in zip(chains, a_hat)]
    o_hat = [_dot(ch['a_rb'], block_diag(x)) + _dot(ch['a_rk'], ch['bd_v']) for ch, x in zip(chains, u_hat)]
    m_c = [jnp.where(bd_mask, _dot_tn(x.astype(BF16), ch['bt'].astype(BF16)), 0.0).astype(BF16)
           for ch, x in zip(chains, a_hat)]
    n_c = [jnp.where(bd_mask, _dot_tn(jnp.concatenate([x, ch['vc']], axis=0).astype(BF16),
                                      jnp.concatenate([ch['bt'], ch['kt']], axis=0).astype(BF16)), 0.0)
           for ch, x in zip(chains, u_hat)]
    pre = {(ch['d'], ch['ck'], ch['g']): (r_hat[i], o_hat[i], m_c[i], n_c[i], ch['w'])
           for i, ch in enumerate(chains)}

    groups = [(d, g) for d in range(2) for g in range(D_WIDTH // HGROUP)]
    states = {dg: s_scr[dg[0], dg[1]] for dg in groups}
    for i in range(nk):
        st_b = {dg: states[dg].astype(BF16) for dg in groups}
        for d, g in groups:
            ck = nk - 1 - i if d == 1 else i
            rh, oh, mc, nc_, w = pre[(d, ck, g)]
            o_ref = ob_ref if d == 1 else of_ref
            o_ref[ck * c:(ck + 1) * c, g * HGROUP:(g + 1) * HGROUP] = _dot_nt(rh, st_b[(d, g)]) + oh
            states[(d, g)] = (states[(d, g)] + _dot(st_b[(d, g)], mc) + nc_) * w
    for d, g in groups:
        s_scr[d, g] = states[(d, g)]


def _d_post_kernel(of_ref, ob_ref, bonus_ref, g_ref, lng_ref, lnb_ref, bs_ref, o_ref):
    o = of_ref[...] + ob_ref[...]
    bs = bs_ref[...]

    def head_mean(x):
        return jnp.concatenate(
            [_group_mean(x[:, s * HGROUP:(s + 1) * HGROUP], bs) for s in range(D_WIDTH // HGROUP)], axis=1)

    cen = o - head_mean(o)
    var = head_mean(cen * cen)
    y = cen * lax.rsqrt(var + LNX_EPS) * lng_ref[...] + lnb_ref[...] + bonus_ref[...]
    o_ref[...] = (y * g_ref[...]).astype(o_ref.dtype)


def _scan_tables(segments, rows):
    fwd, bwd, first = [], [], []
    for start, bsz, lp in segments:
        nc = lp // rows
        for b in range(bsz):
            base = (start + b * lp) // rows
            for ci in range(nc):
                fwd.append(base + ci)
                bwd.append(base + nc - 1 - ci)
                first.append(1 if ci == 0 else 0)
    as_i32 = lambda t: jnp.asarray(np.asarray(t, np.int32))
    return as_i32(fwd), as_i32(bwd), as_i32(first)


def _scan(segments, n_rows, r, v, a, lwf, lwb, kdf, kdb, bf_, bb_):
    nk = SCAN_CHUNKS_PER_STEP
    rows = nk * SCAN_CHUNK
    if any(start % rows or lp % rows for start, _, lp in segments):
        nk, rows = 1, SCAN_CHUNK
    fwd, bwd, first = _scan_tables(segments, rows)
    nsteps = fwd.shape[0]
    blk = (rows, D_WIDTH)
    fspec = pl.BlockSpec(blk, lambda s, fw, bw, fi: (fw[s], 0))
    bspec = pl.BlockSpec(blk, lambda s, fw, bw, fi: (bw[s], 0))
    grid_spec = pltpu.PrefetchScalarGridSpec(
        num_scalar_prefetch=3, grid=(nsteps,),
        in_specs=[fspec] * 6 + [bspec] * 6,
        out_specs=[fspec, bspec],
        scratch_shapes=[pltpu.VMEM((2, D_WIDTH // HGROUP, HGROUP, HGROUP), F32)],
    )
    return pl.pallas_call(
        functools.partial(_scan_kernel, nk=nk), grid_spec=grid_spec,
        out_shape=[jax.ShapeDtypeStruct((n_rows, D_WIDTH), F32)] * 2,
        compiler_params=_cparams(("arbitrary",)), name="wkv_scan",
    )(fwd, bwd, first, r, v, a, lwf, kdf, bf_, r, v, a, lwb, kdb, bb_)


def _rope_tables(segments, n_rows, layer_kind):
    outs = []
    for start, bsz, lp in segments:
        p = jnp.arange(lp)
        d = jnp.arange(HEAD_DIM)
        if layer_kind == "axial":
            t = p - FRONT
            row = jnp.where(t >= 0, t // GRID_W, jnp.where(p >= PAD, t, 0)).astype(F32)
            col = jnp.where(t >= 0, t % GRID_W, jnp.where(p >= PAD, t, 0)).astype(F32)
            half = HEAD_DIM // 2
            inv = A_THETA ** (-jnp.arange(0, half, 2, dtype=F32) / half)
            ang = jnp.concatenate([row[:, None] * inv] * 2 + [col[:, None] * inv] * 2, axis=1)
            first = (d % 32) < 16
            cos = jnp.cos(ang)
            sa = jnp.where(first[None, :], -jnp.sin(ang), 0.0)
            sb = jnp.where(first[None, :], 0.0, jnp.sin(ang))
        else:
            pos = jnp.maximum(p - PAD, 0).astype(F32)
            inv = ROPE_THETA ** (-jnp.arange(0, ROPE_DIMS, 2, dtype=F32) / ROPE_DIMS)
            ang8 = pos[:, None] * inv
            ang = jnp.concatenate([ang8, ang8] + [jnp.zeros_like(ang8)] * 6, axis=1)
            cos = jnp.where((d < ROPE_DIMS)[None, :], jnp.cos(ang), 1.0)
            sa = jnp.where((d < 8)[None, :], -jnp.sin(ang), 0.0)
            sb = jnp.where(((d >= 8) & (d < 16))[None, :], jnp.sin(ang), 0.0)
        tabs = [jnp.tile(jnp.concatenate([t_, t_], axis=1), (bsz, 1)) for t_ in (cos, sa, sb)]
        outs.append(tabs)
    tail = n_rows - sum(b * lp for _, b, lp in segments)
    res = []
    for idx in range(3):
        parts = [o[idx] for o in outs] + [jnp.zeros((tail, LANES), F32)]
        res.append(jnp.concatenate(parts, axis=0))
    return res


def _seq_view(flat, seg, width):
    start, bsz, lp = seg
    return flat[start:start + bsz * lp].reshape(bsz, lp, width)


def _attn_operands(qT, k, vT, seg):
    start, bsz, lp = seg
    chunk = vT.shape[2]
    q_seg = qT[:, start:start + bsz * lp]
    k_seg = k[start:start + bsz * lp].reshape(bsz, lp, k.shape[1])
    v_seg = vT[start // chunk:(start + bsz * lp) // chunk].reshape(bsz, lp // chunk, vT.shape[1], chunk)
    return q_seg, k_seg, v_seg


def _to_flat(parts, n_rows, width, dtype):
    used = sum(p.shape[0] for p in parts)
    return jnp.concatenate(parts + [jnp.zeros((n_rows - used, width), dtype)], axis=0)


def _to_flat_cols(parts, n_rows):
    used = sum(p.shape[1] for p in parts)
    return jnp.concatenate(parts + [jnp.zeros((parts[0].shape[0], n_rows - used), parts[0].dtype)], axis=1)


def _block_avg(n, group):
    idx = np.arange(n) // group
    return jnp.asarray((idx[:, None] == idx[None, :]).astype(np.float32) / group).astype(BF16)


def _forward(xs, p):
    segments = []
    start = 0
    for x in xs:
        bsz, s, _ = x.shape
        lp = s + FRONT
        segments.append((start, bsz, lp))
        start += bsz * lp
    n_used = start
    tm = 512 if n_used >= 4096 else ATTN_CHUNK
    n_rows = _round_up(n_used, tm)

    meta = p['meta_tokens'].astype(F32)
    parts = []
    for x in xs:
        bsz = x.shape[0]
        lead = jnp.concatenate([jnp.zeros((PAD, D_MODEL), F32), meta], axis=0)
        parts.append(jnp.concatenate([jnp.broadcast_to(lead[None], (bsz, FRONT, D_MODEL)), x], axis=1)
                     .reshape(-1, D_MODEL))
    h = _to_flat(parts, n_rows, D_MODEL, F32)

    bs = _block_avg(HGROUP, HEAD_DIM)
    row = lambda t: t.reshape(1, -1).astype(F32)

    depth = p['pre_mix_g'].shape[0]
    for i in range(depth):
        if i % 2 == 0:
            e = i // 2
            cos, sa, sb = _rope_tables(segments, n_rows, "axial")
            ch = np.arange(B_DIM)
            ang = 2.0 * np.pi * ((ch[:, None] * ch[None, :]) % B_DIM) / B_DIM
            cc, sc = jnp.asarray(np.cos(ang), F32), jnp.asarray(np.sin(ang), F32)
            wl = p['b_w'][e].astype(F32)
            hp = lax.Precision.HIGHEST
            pmat = jnp.einsum('cd,gde->gce', cc, wl, precision=hp)
            qmat = -jnp.einsum('cd,gde->gce', sc, wl, precision=hp)
            bd = lambda m: jax.scipy.linalg.block_diag(*[m[g] for g in range(B_GROUPS)])
            pq_base = jnp.concatenate([bd(pmat), bd(qmat)], axis=1)
            qT, k, vT, gf = _row_call(
                "in_even", _in_even_kernel, n_rows, tm, [h, cos, sa, sb],
                [row(p['pre_mix_g'][i]), p['even_w_in'][e].astype(BF16),
                 row(jnp.tile(p['a_q_gain'][e], A_HEADS) * (HEAD_DIM ** -0.5 * LOG2_E)),
                 row(jnp.tile(p['a_k_gain'][e], A_KV_HEADS)), row(p['b_norm_g'][e]), bs,
                 pq_base.astype(BF16)],
                [("cols", A_Q, BF16), ("rows", A_KV, BF16), ("chunks", A_KV, BF16), ("rows", 2 * B_W, F32)])
            ya_parts, yb_parts = [], []
            for seg in segments:
                _, bsz, lp = seg
                ya_parts.append(_attn_a(*_attn_operands(qT, k, vT, seg), bsz))
                gseq = _seq_view(gf, seg, 2 * B_W)[:, PAD:]
                yb = _mixer_b(gseq, row(p['b_b'][e]))
                yb_parts.append(jnp.pad(yb, ((0, 0), (PAD, 0), (0, 0))).reshape(bsz * lp, B_W).astype(BF16))
            yat = _to_flat_cols(ya_parts, n_rows)
            yb = _to_flat(yb_parts, n_rows, B_W, BF16)
            w_out = p['even_w_out'][e].astype(BF16)
            wa, wb = w_out[:A_Q], w_out[A_Q:]
        else:
            o = i // 2
            cos, sa, sb = _rope_tables(segments, n_rows, "partial")
            qT, k, vT, u = _row_call(
                "in_odd", _in_odd_kernel, n_rows, tm, [h, cos, sa, sb],
                [row(p['pre_mix_g'][i]), p['odd_w_in'][o].astype(BF16)],
                [("cols", C_Q, BF16), ("rows", C_KV, BF16), ("chunks", C_KV, BF16), ("rows", D_IN, F32)])
            sink = p['c_sink'][o].astype(F32)
            yat = _to_flat_cols([_attn_c(sink, *_attn_operands(qT, k, vT, seg), seg[1]) for seg in segments],
                                n_rows)
            yb = _mixer_d(u, p, o, bs, segments, n_rows, tm)
            w_out = p['odd_w_out'][o].astype(BF16)
            wa, wb = w_out[:C_Q], w_out[C_Q:]
        (h,) = _row_call("out_proj", _out_proj_kernel, n_rows, tm, [h, yb],
                         [wa, wb, row(p['post_mix_g'][i])], [("rows", D_MODEL, F32)], col_ins=[yat])
        (h,) = _row_call(
            "ffn", functools.partial(_ffn_kernel, chunk=256), n_rows, tm, [h],
            [row(p['pre_ffn_g'][i]), p['ffn_w_gate'][i].astype(BF16), p['ffn_w_up'][i].astype(BF16),
             p['ffn_w_down'][i].astype(BF16), row(p['post_ffn_g'][i])],
            [("rows", D_MODEL, F32)], scratch=[pltpu.VMEM((tm, D_MODEL), F32)])

    outs = []
    for seg in segments:
        outs.append(_seq_view(h, seg, D_MODEL)[:, FRONT:])
    return tuple(outs)


def _mixer_d(u, p, o, bs, segments, n_rows, tm):
    row = lambda t: t.reshape(1, -1).astype(F32)
    w = D_WIDTH
    zeros = lambda r: jnp.zeros((r, w), F32)
    w2 = jnp.concatenate([jnp.concatenate([p['d_w_up'][o][0], zeros(DECAY_RANK)], axis=1),
                          jnp.concatenate([zeros(DECAY_RANK), p['d_w_up'][o][1]], axis=1)], axis=0)
    a2 = jnp.concatenate([jnp.concatenate([p['d_a_up'][o][0], zeros(ICLR_RANK)], axis=1),
                          jnp.concatenate([zeros(ICLR_RANK), p['d_a_up'][o][1]], axis=1)], axis=0)
    consts = [row(p['d_mu_prev'][o]), row(p['d_mu_next'][o]), w2.astype(BF16), a2.astype(BF16),
              p['d_g_up'][o].astype(BF16), row(p['d_w0'][o]), row(p['d_a0'][o]),
              row(p['d_k_k'][o]), row(p['d_k_a'][o]), row(p['d_r_k'][o]), bs]
    nb8 = n_rows // 8
    t8 = tm // 8
    in_specs = [pl.BlockSpec((tm, D_IN), lambda i: (i, 0)),
                pl.BlockSpec((8, D_IN), lambda i: (jnp.maximum(i * t8 - 1, 0), 0)),
                pl.BlockSpec((8, D_IN), lambda i: (jnp.minimum((i + 1) * t8, nb8 - 1), 0))]
    in_specs += [pl.BlockSpec(a.shape, lambda i, nd=a.ndim: (0,) * nd) for a in consts]
    n_out = 11
    prep = pl.pallas_call(
        functools.partial(_d_prep_kernel, tm=tm, segments=tuple(segments)),
        grid=(n_rows // tm,), in_specs=in_specs,
        out_specs=[pl.BlockSpec((tm, w), lambda i: (i, 0))] * n_out,
        out_shape=[jax.ShapeDtypeStruct((n_rows, w), F32)] * n_out,
        compiler_params=_cparams(("parallel",)), name="d_prep",
    )(u, u, u, *consts)
    r, v, a, lwf, lwb, kdf, kdb, bf_, bb_, bonus, g = prep
    of, ob = _scan(segments, n_rows, r, v, a, lwf, lwb, kdf, kdb, bf_, bb_)
    (yd,) = _row_call("d_post", _d_post_kernel, n_rows, tm, [of, ob, bonus, g],
                      [row(p['d_ln_g'][o]), row(p['d_ln_b'][o]), bs], [("rows", w, BF16)])
    return yd


def kernel(x_prompt, x_sample, meta_tokens, pre_mix_g, post_mix_g, pre_ffn_g, post_ffn_g, even_w_in, even_w_out, a_q_gain, a_k_gain, b_norm_g, b_w, b_b, odd_w_in, odd_w_out, c_sink, d_mu_prev, d_mu_next, d_w0, d_w_up, d_a0, d_a_up, d_g_up, d_k_k, d_k_a, d_r_k, d_ln_g, d_ln_b, ffn_w_gate, ffn_w_up, ffn_w_down):
    params = dict(meta_tokens=meta_tokens, pre_mix_g=pre_mix_g, post_mix_g=post_mix_g,
                  pre_ffn_g=pre_ffn_g, post_ffn_g=post_ffn_g,
                  even_w_in=even_w_in, even_w_out=even_w_out, a_q_gain=a_q_gain, a_k_gain=a_k_gain,
                  b_norm_g=b_norm_g, b_w=b_w, b_b=b_b,
                  odd_w_in=odd_w_in, odd_w_out=odd_w_out, c_sink=c_sink,
                  d_mu_prev=d_mu_prev, d_mu_next=d_mu_next, d_w0=d_w0, d_w_up=d_w_up,
                  d_a0=d_a0, d_a_up=d_a_up, d_g_up=d_g_up, d_k_k=d_k_k, d_k_a=d_k_a, d_r_k=d_r_k,
                  d_ln_g=d_ln_g, d_ln_b=d_ln_b,
                  ffn_w_gate=ffn_w_gate, ffn_w_up=ffn_w_up, ffn_w_down=ffn_w_down)
    return _forward([x_prompt, x_sample], params)
```

```python
import functools
import math

import numpy as np
import jax
import jax.numpy as jnp
from jax import lax
from jax.experimental import pallas as pl
from jax.experimental.pallas import tpu as pltpu

F32 = jnp.float32
BF16 = jnp.bfloat16

D_MODEL = 1024
HEAD_DIM = 64
N_META = 16
GRID_W = 64
WINDOW = 128
RMS_EPS = 1e-6
A_HEADS, A_KV_HEADS, A_THETA = 12, 4, 10000.0
B_GROUPS, B_DIM = 4, 64
C_HEADS, C_KV_HEADS = 8, 2
ROPE_THETA = 500000.0
ROPE_DIMS = HEAD_DIM // 4
D_HEADS = 8
D_WIDTH = D_HEADS * HEAD_DIM
DECAY_RANK, ICLR_RANK, GATE_RANK = 64, 64, 128
LNX_EPS = 64e-5
D_FF = 2816
A_Q, A_KV, B_W = A_HEADS * HEAD_DIM, A_KV_HEADS * HEAD_DIM, B_GROUPS * B_DIM
EVEN_IN = A_Q + 2 * A_KV + B_W
C_Q, C_KV = C_HEADS * HEAD_DIM, C_KV_HEADS * HEAD_DIM
D_IN = 3 * D_WIDTH + 2 * DECAY_RANK + 2 * ICLR_RANK + GATE_RANK
ODD_IN = C_Q + 2 * C_KV + D_IN

LANES = 128
FRONT = 256
PAD = FRONT - N_META
NEG = -1e30
SCAN_CHUNK = 64
SCAN_CHUNKS_PER_STEP = 4
HGROUP = 256
FFT_L1 = 16
VMEM_LIMIT = 56 * 1024 * 1024
ATTN_CHUNK = 256
ATTN_MAX_CHUNKS = 20
ATTN_UNROLL = 4
LOG2_E = math.log2(math.e)


def _round_up(x, m):
    return (x + m - 1) // m * m


def _cparams(sem):
    return pltpu.CompilerParams(dimension_semantics=sem, vmem_limit_bytes=VMEM_LIMIT)


def _dot(a, b):
    return jnp.dot(a, b, preferred_element_type=F32)


def _dot_nt(a, b):
    return lax.dot_general(a, b, (((1,), (1,)), ((), ())), preferred_element_type=F32)


def _dot_tn(a, b):
    return lax.dot_general(a, b, (((0,), (0,)), ((), ())), preferred_element_type=F32)


def _split2(x):
    hi = x.astype(BF16)
    lo = (x - hi.astype(F32)).astype(BF16)
    return hi, lo


def _group_mean(x, bs):
    hi, lo = _split2(x)
    return _dot(hi, bs) + _dot(lo, bs)


def _rope(x, cos, sin_a, sin_b, shift):
    n = x.shape[1]
    return x * cos + pltpu.roll(x, n - shift, 1) * sin_a + pltpu.roll(x, shift, 1) * sin_b


def _wide(t, n):
    return t if n == LANES else jnp.concatenate([t] * (n // LANES), axis=1)


def _rms_rows(x, g):
    ms = jnp.mean(x * x, axis=-1, keepdims=True)
    return x * lax.rsqrt(ms + RMS_EPS) * g


def _store_vt(v_ref, v):
    n = v.shape[1]
    for r in range(v.shape[0] // n):
        v_ref[r] = v[r * n:(r + 1) * n, :].T.astype(BF16)


def _in_even_kernel(h_ref, cos_ref, sa_ref, sb_ref, g_ref, w_ref, qg_ref, kg_ref, bg_ref, bs_ref, pq_ref,
                    q_ref, k_ref, v_ref, gf_ref):
    hn = _rms_rows(h_ref[...], g_ref[...]).astype(BF16)
    proj = _dot(hn, w_ref[...])
    bs = bs_ref[...]
    cos, sa, sb = (_wide(r[...], HGROUP) for r in (cos_ref, sa_ref, sb_ref))

    def norm_rope(x, gain):
        xn = x * lax.rsqrt(_group_mean(x * x, bs) + RMS_EPS) * gain
        return _rope(xn, cos, sa, sb, 16)

    for s in range(A_Q // HGROUP):
        cs = slice(s * HGROUP, (s + 1) * HGROUP)
        q_ref[cs, :] = norm_rope(proj[:, cs], qg_ref[:, cs]).T.astype(BF16)
    k_ref[...] = norm_rope(proj[:, A_Q:A_Q + A_KV], kg_ref[...]).astype(BF16)
    _store_vt(v_ref, proj[:, A_Q + A_KV:A_Q + 2 * A_KV])
    f = proj[:, A_Q + 2 * A_KV:]
    fn = f * lax.rsqrt(_group_mean(f * f, bs) + RMS_EPS) * bg_ref[...]
    gf_ref[...] = _dot(fn.astype(BF16), pq_ref[...])


def _in_odd_kernel(h_ref, cos_ref, sa_ref, sb_ref, g_ref, w_ref, q_ref, k_ref, v_ref, u_ref):
    hn = _rms_rows(h_ref[...], g_ref[...]).astype(BF16)
    proj = _dot(hn, w_ref[...])
    cos, sa, sb = cos_ref[...], sa_ref[...], sb_ref[...]
    cos2, sa2, sb2 = (_wide(t, HGROUP) for t in (cos, sa, sb))
    for s in range(C_Q // HGROUP):
        cs = slice(s * HGROUP, (s + 1) * HGROUP)
        q_ref[cs, :] = (_rope(proj[:, cs], cos2, sa2, sb2, 8) * (HEAD_DIM ** -0.5 * LOG2_E)).T.astype(BF16)
    k_ref[...] = _rope(proj[:, C_Q:C_Q + C_KV], cos, sa, sb, 8).astype(BF16)
    _store_vt(v_ref, proj[:, C_Q + C_KV:C_Q + 2 * C_KV])
    u_ref[...] = proj[:, C_Q + 2 * C_KV:]


def _out_proj_kernel(h_ref, yb_ref, wa_ref, wb_ref, g_ref, yat_ref, o_ref):
    mix = _dot_tn(yat_ref[...], wa_ref[...]) + _dot(yb_ref[...], wb_ref[...])
    o_ref[...] = h_ref[...] + _rms_rows(mix, g_ref[...])


def _ffn_kernel(h_ref, g1_ref, wg_ref, wu_ref, wd_ref, g2_ref, o_ref, acc_ref, *, chunk):
    h = h_ref[...]
    hn = _rms_rows(h, g1_ref[...]).astype(BF16)
    for c in range(D_FF // chunk):
        cs = slice(c * chunk, (c + 1) * chunk)
        gate = _dot(hn, wg_ref[:, cs])
        up = _dot(hn, wu_ref[:, cs])
        act = (gate * (1.0 / (1.0 + jnp.exp(-gate))) * up).astype(BF16)
        part = _dot(act, wd_ref[cs, :])
        if c == 0:
            acc_ref[...] = part
        else:
            acc_ref[...] += part
    o_ref[...] = h + _rms_rows(acc_ref[...], g2_ref[...])


def _row_call(name, kernel, n_rows, tm, row_ins, const_ins, outs, scratch=(), col_ins=()):
    grid = (n_rows // tm,)
    in_specs = [pl.BlockSpec((tm, a.shape[1]), lambda i: (i, 0)) for a in row_ins]
    in_specs += [pl.BlockSpec(a.shape, lambda i, nd=a.ndim: (0,) * nd) for a in const_ins]
    in_specs += [pl.BlockSpec((a.shape[0], tm), lambda i: (0, i)) for a in col_ins]
    out_specs, out_shape = [], []
    for kind, n, dt in outs:
        if kind == "rows":
            out_specs.append(pl.BlockSpec((tm, n), lambda i: (i, 0)))
            out_shape.append(jax.ShapeDtypeStruct((n_rows, n), dt))
        elif kind == "cols":
            out_specs.append(pl.BlockSpec((n, tm), lambda i: (0, i)))
            out_shape.append(jax.ShapeDtypeStruct((n, n_rows), dt))
        else:
            out_specs.append(pl.BlockSpec((tm // n, n, n), lambda i: (i, 0, 0)))
            out_shape.append(jax.ShapeDtypeStruct((n_rows // n, n, n), dt))
    return pl.pallas_call(
        kernel, grid=grid, in_specs=in_specs, out_specs=out_specs, out_shape=out_shape,
        scratch_shapes=list(scratch), compiler_params=_cparams(("parallel",)), name=name,
    )(*row_ins, *const_ins, *col_ins)


def _padded_q(q_ref, h, kv):
    qh = q_ref[h * HEAD_DIM:(h + 1) * HEAD_DIM, :]
    zero = jnp.zeros_like(qh)
    return jnp.concatenate([qh, zero] if kv % 2 == 0 else [zero, qh], axis=0)


def _attn_a_kernel(qT_ref, k_ref, vT_ref, o_ref, m_scr, l_scr, acc_scr, s_scr, *, nchunk, nkb, tq):
    j = pl.program_id(2)

    @pl.when(j == 0)
    def _():
        m_scr[...] = jnp.full(m_scr.shape, NEG, F32)
        l_scr[...] = jnp.zeros(l_scr.shape, F32)
        acc_scr[...] = jnp.zeros(acc_scr.shape, F32)

    group = A_HEADS // A_KV_HEADS
    ck = ATTN_CHUNK
    key_row = lax.broadcasted_iota(jnp.int32, (ck, tq), 0) + j * (nchunk * ck)

    def fold(s, op):
        return op(s.reshape(ck // 8, 8, tq), axis=0)

    def gang_q(g):
        return [_padded_q(qT_ref, g * group + t, g) for t in range(group)]

    def gang_scores(g, qps, c, mxs, first=False):
        kc = k_ref[0, pl.ds(pl.multiple_of(c * ck, ck), ck), (g // 2) * LANES:(g // 2 + 1) * LANES]
        out = []
        for t in range(group):
            s = _dot(kc, qps[t])
            if first:
                s = jnp.where(key_row >= PAD, s, NEG)
            s_scr[g % 2, t, c] = s
            out.append(fold(s, jnp.max) if mxs is None else jnp.maximum(mxs[t], fold(s, jnp.max)))
        return tuple(out)

    def gang_values(g, c, m_news):
        vc = vT_ref[0, c, g * HEAD_DIM:(g + 1) * HEAD_DIM, :]
        ps = [jnp.exp2(s_scr[g % 2, t, c] - m_news[t]) for t in range(group)]
        return tuple(_dot(vc, p.astype(BF16)) for p in ps), tuple(fold(p, jnp.sum) for p in ps)

    def add(xs, ys):
        return tuple(x + y for x, y in zip(xs, ys))

    unroll = max(u for u in range(1, ATTN_UNROLL + 1) if max(nchunk - 1, 1) % u == 0)
    qps = gang_q(0)
    mxs = lax.fori_loop(1, nchunk, lambda c, mxs, qps=qps: gang_scores(0, qps, c, mxs),
                        gang_scores(0, qps, 0, None, first=True), unroll=unroll)
    for g in range(A_KV_HEADS):
        heads = [g * group + t for t in range(group)]
        m_olds = [m_scr[h] for h in heads]
        m_news = [jnp.maximum(mo, jnp.max(mx, axis=0, keepdims=True)) for mo, mx in zip(m_olds, mxs)]
        if g + 1 < A_KV_HEADS:
            qps = gang_q(g + 1)

            def body(c, carry, g=g, m_news=m_news, qps=qps):
                accs, lsums, mxn = carry
                mxn = gang_scores(g + 1, qps, c, mxn)
                pvs, pss = gang_values(g, c, m_news)
                return add(accs, pvs), add(lsums, pss), mxn

            mx0 = gang_scores(g + 1, qps, 0, None, first=True)
            accs, lsums, mxs = lax.fori_loop(1, nchunk, body, (*gang_values(g, 0, m_news), mx0),
                                             unroll=unroll)
        else:
            def tail(c, carry, g=g, m_news=m_news):
                pvs, pss = gang_values(g, c, m_news)
                return add(carry[0], pvs), add(carry[1], pss)

            accs, lsums = lax.fori_loop(1, nchunk, tail, gang_values(g, 0, m_news), unroll=unroll)
        for t, h in enumerate(heads):
            alpha = jnp.exp2(m_olds[t] - m_news[t])
            acc_scr[h] = acc_scr[h] * alpha + accs[t]
            l_scr[h] = l_scr[h] * alpha + jnp.sum(lsums[t], axis=0, keepdims=True)
            m_scr[h] = m_news[t]

    @pl.when(j == nkb - 1)
    def _():
        for h in range(A_HEADS):
            o_ref[h * HEAD_DIM:(h + 1) * HEAD_DIM, :] = (acc_scr[h] / l_scr[h]).astype(o_ref.dtype)


def _attn_a(qT, k, vT, bsz):
    lp = k.shape[1]
    tq = ATTN_CHUNK
    total = lp // ATTN_CHUNK
    nkb = min(n for n in range(1, total + 1) if total % n == 0 and total // n <= ATTN_MAX_CHUNKS)
    tk = lp // nkb
    nchunk = tk // ATTN_CHUNK
    nq = lp // tq
    kernel = functools.partial(_attn_a_kernel, nchunk=nchunk, nkb=nkb, tq=tq)
    return pl.pallas_call(
        kernel, grid=(bsz, nq, nkb),
        in_specs=[
            pl.BlockSpec((A_Q, tq), lambda b, i, j: (0, b * nq + i)),
            pl.BlockSpec((1, tk, A_KV), lambda b, i, j: (b, j, 0)),
            pl.BlockSpec((1, nchunk, A_KV, ATTN_CHUNK), lambda b, i, j: (b, j, 0, 0)),
        ],
        out_specs=pl.BlockSpec((A_Q, tq), lambda b, i, j: (0, b * nq + i)),
        out_shape=jax.ShapeDtypeStruct((A_Q, bsz * lp), BF16),
        scratch_shapes=[
            pltpu.VMEM((A_HEADS, 1, tq), F32),
            pltpu.VMEM((A_HEADS, 1, tq), F32),
            pltpu.VMEM((A_HEADS, HEAD_DIM, tq), F32),
            pltpu.VMEM((2, A_HEADS // A_KV_HEADS, nchunk, ATTN_CHUNK, tq), F32),
        ],
        compiler_params=_cparams(("parallel", "parallel", "arbitrary")), name="attn_a",
    )(qT, k, vT)


def _attn_c_kernel(sink_ref, qT_ref, k0, k1, k2, k3, v0, v1, v2, v3, o_ref, *, lp):
    j = pl.program_id(1)
    k_refs, v_refs = (k0, k1, k2, k3), (v0, v1, v2, v3)
    row = lax.broadcasted_iota(jnp.int32, (LANES, LANES), 0)
    pq = lax.broadcasted_iota(jnp.int32, (LANES, LANES), 1) + j * LANES
    biases = []
    for slot in range(4):
        if slot == 0:
            bias = jnp.where(row >= PAD % LANES, 0.0, NEG)
        else:
            pk = row + (j + slot - 2) * LANES
            in_window = jnp.where(jnp.abs(pq - pk) <= WINDOW, 0.0, NEG)
            bias = jnp.where(pk >= FRONT, jnp.where(pk < lp, in_window, NEG), NEG)
        biases.append(bias.astype(F32))
    group = C_HEADS // C_KV_HEADS
    for h in range(C_HEADS):
        kv = h // group
        qp = _padded_q(qT_ref, h, kv)
        sink = sink_ref[h] * LOG2_E
        ss = [_dot(k_refs[t][0], qp) + biases[t] for t in range(4)]
        m = jnp.maximum(jnp.maximum(ss[0], ss[1]), jnp.maximum(ss[2], ss[3]))
        m = jnp.maximum(jnp.max(m, axis=0, keepdims=True), sink)
        acc = jnp.zeros((HEAD_DIM, LANES), F32)
        denom = jnp.exp2(sink - m)
        for t in range(4):
            p = jnp.exp2(ss[t] - m)
            acc = acc + _dot(v_refs[t][0, 0, kv * HEAD_DIM:(kv + 1) * HEAD_DIM, :], p.astype(BF16))
            denom = denom + jnp.sum(p, axis=0, keepdims=True)
        o_ref[h * HEAD_DIM:(h + 1) * HEAD_DIM, :] = (acc / denom).astype(o_ref.dtype)


def _attn_c(sink, qT, k, vT, bsz):
    lp = k.shape[1]
    nb = lp // LANES
    kernel = functools.partial(_attn_c_kernel, lp=lp)

    def kspec(fn):
        return pl.BlockSpec((1, LANES, C_KV), lambda b, j: (b, fn(j), 0))

    def vspec(fn):
        return pl.BlockSpec((1, 1, C_KV, LANES), lambda b, j: (b, fn(j), 0, 0))

    fns = (lambda j: PAD // LANES, lambda j: jnp.maximum(j - 1, 0), lambda j: j,
           lambda j: jnp.minimum(j + 1, nb - 1))
    return pl.pallas_call(
        kernel, grid=(bsz, nb),
        in_specs=[pl.BlockSpec(memory_space=pltpu.SMEM),
                  pl.BlockSpec((C_Q, LANES), lambda b, j: (0, b * nb + j))]
                 + [kspec(f) for f in fns] + [vspec(f) for f in fns],
        out_specs=pl.BlockSpec((C_Q, LANES), lambda b, j: (0, b * nb + j)),
        out_shape=jax.ShapeDtypeStruct((C_Q, bsz * lp), BF16),
        compiler_params=_cparams(("parallel", "parallel")), name="attn_c",
    )(sink, qT, k, k, k, k, vT, vT, vT, vT)


def _fft1_kernel(x_ref, c1_ref, s1_ref, o_ref):
    x = x_ref[0]
    tn = x.shape[1]
    lane = lax.broadcasted_iota(jnp.int32, x.shape, 1)
    is_re = (lane & (2 * B_W - 1)) < B_W
    xs = jnp.where(is_re, pltpu.roll(x, tn - B_W, 1), -pltpu.roll(x, B_W, 1))
    o_ref[0] = _dot(c1_ref[...], x.astype(BF16)) + _dot(s1_ref[...], xs.astype(BF16))


def _fft2_kernel(a_ref, tc_ref, ts_ref, c2_ref, s2_ref, bias_ref, o_ref):
    a = a_ref[0, 0]
    tc, ts = _wide(tc_ref[0], B_W), _wide(ts_ref[0], B_W)
    are, aim = a[:, :B_W], a[:, B_W:]
    bre = (are * tc + aim * ts).astype(BF16)
    bim = (aim * tc - are * ts).astype(BF16)
    o_ref[0] = _dot(c2_ref[...], bre) + _dot(s2_ref[...], bim) + bias_ref[...]


def _dft_tables(l2, l2p, l2o):
    l1 = FFT_L1
    length = l1 * l2
    n1 = np.arange(l1)
    ang1 = 2.0 * np.pi * ((n1[:, None] * n1[None, :]) % l1) / l1
    c1, s1 = np.cos(ang1), np.sin(ang1)
    n2 = np.arange(l2)
    angt = 2.0 * np.pi * (n1[:, None] * n2[None, :]) / length
    tc = np.zeros((l1, l2p, LANES), np.float32)
    ts = np.zeros((l1, l2p, LANES), np.float32)
    tc[:, :l2, :] = np.cos(angt)[:, :, None]
    ts[:, :l2, :] = np.sin(angt)[:, :, None]
    ang2 = 2.0 * np.pi * ((n2[:, None] * n2[None, :]) % l2) / l2
    scale = 1.0 / math.sqrt(B_DIM * length)
    c2 = np.zeros((l2o, l2p), np.float32)
    s2 = np.zeros((l2o, l2p), np.float32)
    c2[:l2, :l2] = np.cos(ang2) * scale
    s2[:l2, :l2] = np.sin(ang2) * scale
    as_bf = lambda t: jnp.asarray(t, F32).astype(BF16)
    return as_bf(c1), as_bf(s1), jnp.asarray(tc), jnp.asarray(ts), as_bf(c2), as_bf(s2)


def _mixer_b(gseq, bias):
    bsz, length, _ = gseq.shape
    l1 = FFT_L1
    l2 = length // l1
    l2p, l2o = _round_up(l2, LANES), _round_up(l2, 8)
    c1, s1, tc, ts, c2, s2 = _dft_tables(l2, l2p, l2o)
    x = jnp.pad(gseq.reshape(bsz, l1, l2, 2 * B_W), ((0, 0), (0, 0), (0, l2p - l2), (0, 0)))
    x = x.reshape(bsz, l1, l2p * 2 * B_W)
    ncol = l2p * 2 * B_W
    tn = 2 * B_W * 48 if l2p % 48 == 0 else 2 * B_W
    a = pl.pallas_call(
        _fft1_kernel, grid=(bsz, ncol // tn),
        in_specs=[pl.BlockSpec((1, l1, tn), lambda b, i: (b, 0, i)),
                  pl.BlockSpec((l1, l1), lambda b, i: (0, 0)),
                  pl.BlockSpec((l1, l1), lambda b, i: (0, 0))],
        out_specs=pl.BlockSpec((1, l1, tn), lambda b, i: (b, 0, i)),
        out_shape=jax.ShapeDtypeStruct((bsz, l1, ncol), F32),
        compiler_params=_cparams(("parallel", "parallel")), name="fft1",
    )(x, c1, s1)
    a = a.reshape(bsz, l1, l2p, 2 * B_W)
    y = pl.pallas_call(
        _fft2_kernel, grid=(bsz, l1),
        in_specs=[pl.BlockSpec((1, 1, l2p, 2 * B_W), lambda b, k: (b, k, 0, 0)),
                  pl.BlockSpec((1, l2p, LANES), lambda b, k: (k, 0, 0)),
                  pl.BlockSpec((1, l2p, LANES), lambda b, k: (k, 0, 0)),
                  pl.BlockSpec((l2o, l2p), lambda b, k: (0, 0)),
                  pl.BlockSpec((l2o, l2p), lambda b, k: (0, 0)),
                  pl.BlockSpec((1, B_W), lambda b, k: (0, 0))],
        out_specs=pl.BlockSpec((1, l2o, B_W), lambda b, k: (b, 0, k)),
        out_shape=jax.ShapeDtypeStruct((bsz, l2o, l1 * B_W), F32),
        compiler_params=_cparams(("parallel", "parallel")), name="fft2",
    )(a, tc, ts, c2, s2, bias)
    return y[:, :l2].reshape(bsz, length, B_W)


def _seq_position(rows, segments):
    pos = jnp.full(rows.shape, -1.0, F32)
    seqlen = jnp.full(rows.shape, 1.0, F32)
    for start, bsz, lp in segments:
        rel = rows - float(start)
        q = jnp.floor((rel + 0.5) * (1.0 / lp))
        inside = jnp.where(rel >= 0.0, jnp.where(rel < float(bsz * lp), 1.0, 0.0), 0.0) > 0.5
        pos = jnp.where(inside, rel - q * lp, pos)
        seqlen = jnp.where(inside, float(lp), seqlen)
    return pos, seqlen


def _sigmoid(x):
    return 1.0 / (1.0 + jnp.exp(-x))


def _d_prep_kernel(u_ref, up_ref, un_ref, mup_ref, mun_ref, w2_ref, a2_ref, gup_ref, w0_ref, a0_ref,
                   kk_ref, ka_ref, rk_ref, bs_ref,
                   r_ref, v_ref, a_ref, lwf_ref, lwb_ref, kdf_ref, kdb_ref, bf_ref, bb_ref,
                   bonus_ref, g_ref, *, tm, segments):
    i = pl.program_id(0)
    u = u_ref[...]
    rows = (lax.broadcasted_iota(jnp.int32, (tm, 1), 0) + i * tm).astype(F32)
    pos, seqlen = _seq_position(rows, segments)
    local = lax.broadcasted_iota(jnp.int32, (tm, 1), 0)
    u_prev = jnp.where(local == 0, up_ref[7:8, :], pltpu.roll(u, 1, 0))
    u_next = jnp.where(local == tm - 1, un_ref[0:1, :], pltpu.roll(u, tm - 1, 0))
    u_prev = jnp.where(pos == float(PAD), 0.0, u_prev)
    u_next = jnp.where(pos == seqlen - 1.0, 0.0, u_next)
    u = u + mup_ref[...] * (u_prev - u) + mun_ref[...] * (u_next - u)
    valid = jnp.where(pos >= float(PAD), 1.0, 0.0)

    w = D_WIDTH
    r, k, v = u[:, :w], u[:, w:2 * w], u[:, 2 * w:3 * w]
    c0 = 3 * w
    dec = _dot(jnp.tanh(u[:, c0:c0 + 2 * DECAY_RANK]).astype(BF16), w2_ref[...]) + w0_ref[...]
    c0 += 2 * DECAY_RANK
    icl = _dot(u[:, c0:c0 + 2 * ICLR_RANK].astype(BF16), a2_ref[...]) + a0_ref[...]
    c0 += 2 * ICLR_RANK
    g_ref[...] = _dot(_sigmoid(u[:, c0:c0 + GATE_RANK]).astype(BF16), gup_ref[...])

    bs = bs_ref[...]

    def head_sum(x):
        return jnp.concatenate(
            [_group_mean(x[:, s * HGROUP:(s + 1) * HGROUP], bs) for s in range(w // HGROUP)], axis=1
        ) * float(HEAD_DIM)

    kk = k * kk_ref[...]
    kk = kk * lax.rsqrt(jnp.maximum(head_sum(kk * kk), 1e-24))
    r_ref[...] = r
    v_ref[...] = v
    a_ref[...] = -kk * valid
    bonus = jnp.zeros_like(r)
    for d, (lw_ref, kd_ref, b_ref) in enumerate(((lwf_ref, kdf_ref, bf_ref), (lwb_ref, kdb_ref, bb_ref))):
        x = -dec[:, d * w:(d + 1) * w]
        softplus = jnp.maximum(x, 0.0) + jnp.log(1.0 + jnp.exp(-jnp.abs(x)))
        lw_ref[...] = -jnp.exp(-softplus - 0.5)
        gate = _sigmoid(icl[:, d * w:(d + 1) * w])
        kd = k * (1.0 + (gate - 1.0) * ka_ref[...])
        kd_ref[...] = kd * valid
        b_ref[...] = kk * gate * valid
        bonus = bonus + head_sum(r * kd * rk_ref[...]) * v
    bonus_ref[...] = bonus


def _scan_kernel(fwd_blk, bwd_blk, first,
                 rf, vf, af, lwf, kdf, bf_, rb, vb, ab, lwb, kdb, bb_,
                 of_ref, ob_ref, s_scr, *, nk):
    del fwd_blk, bwd_blk
    step = pl.program_id(0)
    c = SCAN_CHUNK

    @pl.when(first[step] == 1)
    def _():
        s_scr[...] = jnp.zeros(s_scr.shape, F32)

    t_sq = lax.broadcasted_iota(jnp.int32, (nk * c, nk * c), 0)
    s_sq = lax.broadcasted_iota(jnp.int32, (nk * c, nk * c), 1)
    t_cat = lax.broadcasted_iota(jnp.int32, (c, HGROUP), 0)
    s_cat = lax.broadcasted_iota(jnp.int32, (c, HGROUP), 1) & (c - 1)
    bd_row = lax.broadcasted_iota(jnp.int32, (HGROUP, HGROUP), 0) >> 6
    bd_col = lax.broadcasted_iota(jnp.int32, (HGROUP, HGROUP), 1) >> 6
    bd_mask = bd_row == bd_col
    eye_cat = jnp.where(t_cat == s_cat, 1.0, 0.0).astype(F32)

    def block_diag(x):
        return jnp.where(bd_mask, jnp.concatenate([x] * (HGROUP // c), axis=0), 0.0).astype(BF16)

    chains = []
    dirs = ((0, rf, vf, af, lwf, kdf, bf_), (1, rb, vb, ab, lwb, kdb, bb_))
    for d, r_ref, v_ref, a_ref, lw_ref, kd_ref, b_ref in dirs:
        rev = d == 1
        earlier = (s_sq >= t_sq) if rev else (s_sq <= t_sq)
        tri = jnp.where((t_sq >> 6) == (s_sq >> 6), jnp.where(earlier, 1.0, 0.0), 0.0).astype(BF16)
        lw = lw_ref[...]
        h1 = lw.astype(BF16)
        r1 = lw - h1.astype(F32)
        h2 = r1.astype(BF16)
        h3 = (r1 - h2.astype(F32)).astype(BF16)
        cum = _dot(tri, h1) + _dot(tri, h2) + _dot(tri, h3)
        e_in = jnp.exp(cum)
        e_neg = jnp.exp(-cum)
        a_t = a_ref[...] * jnp.exp(cum - lw)
        r_t = r_ref[...] * e_in
        b_t = b_ref[...] * e_neg
        k_t = kd_ref[...] * e_neg
        v_all = v_ref[...]
        for ck in range(nk):
            rows = slice(ck * c, (ck + 1) * c)
            w_row = ck * c if rev else (ck + 1) * c - 1
            for g in range(D_WIDTH // HGROUP):
                cs = slice(g * HGROUP, (g + 1) * HGROUP)
                chains.append(dict(d=d, ck=ck, g=g, rev=rev, at=a_t[rows, cs], rt=r_t[rows, cs],
                                   bt=b_t[rows, cs], kt=k_t[rows, cs], vc=v_all[rows, cs],
                                   w=e_in[w_row:w_row + 1, cs]))

    def strict(ch):
        return (s_cat > t_cat) if ch['rev'] else (s_cat < t_cat)

    def incl(ch):
        return (s_cat >= t_cat) if ch['rev'] else (s_cat <= t_cat)

    for ch in chains:
        ch['lhs'] = jnp.concatenate([ch['at'], ch['rt']], axis=0).astype(BF16)
        ch['bd_v'] = block_diag(ch['vc'])
    m_b = [_dot_nt(ch['lhs'], block_diag(ch['bt'])) for ch in chains]
    m_k = [_dot_nt(ch['lhs'], block_diag(ch['kt'])) for ch in chains]
    for ch, mb, mk in zip(chains, m_b, m_k):
        ch['a_ab'] = jnp.where(strict(ch), mb[:c], 0.0)
        ch['a_rb'] = jnp.where(incl(ch), mb[c:], 0.0).astype(BF16)
        ch['a_ak'] = jnp.where(strict(ch), mk[:c], 0.0).astype(BF16)
        ch['a_rk'] = jnp.where(incl(ch), mk[c:], 0.0).astype(BF16)
        ch['t'] = eye_cat + jnp.where((t_cat >> 1) == (s_cat >> 1), ch['a_ab'], 0.0)
    m = 2
    while m < c:
        sh = int(math.log2(m))
        xs = []
        for ch in chains:
            a_m = jnp.where((t_cat >> (sh + 1)) == (s_cat >> (sh + 1)),
                            jnp.where((t_cat >> sh) != (s_cat >> sh), ch['a_ab'], 0.0), 0.0)
            xs.append(_dot(a_m.astype(BF16), block_diag(ch['t'])))
        for ch, x in zip(chains, xs):
            ch['t'] = ch['t'] + _dot(ch['t'].astype(BF16), block_diag(x))
        m *= 2
    akv = [_dot(ch['a_ak'], ch['bd_v']) for ch in chains]
    for ch in chains:
        ch['t'] = ch['t'].astype(BF16)
    a_hat = [_dot(ch['t'], block_diag(ch['at'])) for ch in chains]
    u_hat = [_dot(ch['t'], block_diag(x)) for ch, x in zip(chains, akv)]
    r_hat = [(ch['rt'] + _dot(ch['a_rb'], block_diag(x))).astype(BF16) for ch, x in zip(chains, a_hat)]
    o_hat = [_dot(ch['a_rb'], block_diag(x)) + _dot(ch['a_rk'], ch['bd_v']) for ch, x in zip(chains, u_hat)]
    m_c = [jnp.where(bd_mask, _dot_tn(x.astype(BF16), ch['bt'].astype(BF16)), 0.0).astype(BF16)
           for ch, x in zip(chains, a_hat)]
    n_c = [jnp.where(bd_mask, _dot_tn(jnp.concatenate([x, ch['vc']], axis=0).astype(BF16),
                                      jnp.concatenate([ch['bt'], ch['kt']], axis=0).astype(BF16)), 0.0)
           for ch, x in zip(chains, u_hat)]
    pre = {(ch['d'], ch['ck'], ch['g']): (r_hat[i], o_hat[i], m_c[i], n_c[i], ch['w'])
           for i, ch in enumerate(chains)}

    groups = [(d, g) for d in range(2) for g in range(D_WIDTH // HGROUP)]
    states = {dg: s_scr[dg[0], dg[1]] for dg in groups}
    for i in range(nk):
        st_b = {dg: states[dg].astype(BF16) for dg in groups}
        for d, g in groups:
            ck = nk - 1 - i if d == 1 else i
            rh, oh, mc, nc_, w = pre[(d, ck, g)]
            o_ref = ob_ref if d == 1 else of_ref
            o_ref[ck * c:(ck + 1) * c, g * HGROUP:(g + 1) * HGROUP] = _dot_nt(rh, st_b[(d, g)]) + oh
            states[(d, g)] = (states[(d, g)] + _dot(st_b[(d, g)], mc) + nc_) * w
    for d, g in groups:
        s_scr[d, g] = states[(d, g)]


def _d_post_kernel(of_ref, ob_ref, bonus_ref, g_ref, lng_ref, lnb_ref, bs_ref, o_ref):
    o = of_ref[...] + ob_ref[...]
    bs = bs_ref[...]

    def head_mean(x):
        return jnp.concatenate(
            [_group_mean(x[:, s * HGROUP:(s + 1) * HGROUP], bs) for s in range(D_WIDTH // HGROUP)], axis=1)

    cen = o - head_mean(o)
    var = head_mean(cen * cen)
    y = cen * lax.rsqrt(var + LNX_EPS) * lng_ref[...] + lnb_ref[...] + bonus_ref[...]
    o_ref[...] = (y * g_ref[...]).astype(o_ref.dtype)


def _scan_tables(segments, rows):
    fwd, bwd, first = [], [], []
    for start, bsz, lp in segments:
        nc = lp // rows
        for b in range(bsz):
            base = (start + b * lp) // rows
            for ci in range(nc):
                fwd.append(base + ci)
                bwd.append(base + nc - 1 - ci)
                first.append(1 if ci == 0 else 0)
    as_i32 = lambda t: jnp.asarray(np.asarray(t, np.int32))
    return as_i32(fwd), as_i32(bwd), as_i32(first)


def _scan(segments, n_rows, r, v, a, lwf, lwb, kdf, kdb, bf_, bb_):
    nk = SCAN_CHUNKS_PER_STEP
    rows = nk * SCAN_CHUNK
    if any(start % rows or lp % rows for start, _, lp in segments):
        nk, rows = 1, SCAN_CHUNK
    fwd, bwd, first = _scan_tables(segments, rows)
    nsteps = fwd.shape[0]
    blk = (rows, D_WIDTH)
    fspec = pl.BlockSpec(blk, lambda s, fw, bw, fi: (fw[s], 0))
    bspec = pl.BlockSpec(blk, lambda s, fw, bw, fi: (bw[s], 0))
    grid_spec = pltpu.PrefetchScalarGridSpec(
        num_scalar_prefetch=3, grid=(nsteps,),
        in_specs=[fspec] * 6 + [bspec] * 6,
        out_specs=[fspec, bspec],
        scratch_shapes=[pltpu.VMEM((2, D_WIDTH // HGROUP, HGROUP, HGROUP), F32)],
    )
    return pl.pallas_call(
        functools.partial(_scan_kernel, nk=nk), grid_spec=grid_spec,
        out_shape=[jax.ShapeDtypeStruct((n_rows, D_WIDTH), F32)] * 2,
        compiler_params=_cparams(("arbitrary",)), name="wkv_scan",
    )(fwd, bwd, first, r, v, a, lwf, kdf, bf_, r, v, a, lwb, kdb, bb_)


def _rope_tables(segments, n_rows, layer_kind):
    outs = []
    for start, bsz, lp in segments:
        p = jnp.arange(lp)
        d = jnp.arange(HEAD_DIM)
        if layer_kind == "axial":
            t = p - FRONT
            row = jnp.where(t >= 0, t // GRID_W, jnp.where(p >= PAD, t, 0)).astype(F32)
            col = jnp.where(t >= 0, t % GRID_W, jnp.where(p >= PAD, t, 0)).astype(F32)
            half = HEAD_DIM // 2
            inv = A_THETA ** (-jnp.arange(0, half, 2, dtype=F32) / half)
            ang = jnp.concatenate([row[:, None] * inv] * 2 + [col[:, None] * inv] * 2, axis=1)
            first = (d % 32) < 16
            cos = jnp.cos(ang)
            sa = jnp.where(first[None, :], -jnp.sin(ang), 0.0)
            sb = jnp.where(first[None, :], 0.0, jnp.sin(ang))
        else:
            pos = jnp.maximum(p - PAD, 0).astype(F32)
            inv = ROPE_THETA ** (-jnp.arange(0, ROPE_DIMS, 2, dtype=F32) / ROPE_DIMS)
            ang8 = pos[:, None] * inv
            ang = jnp.concatenate([ang8, ang8] + [jnp.zeros_like(ang8)] * 6, axis=1)
            cos = jnp.where((d < ROPE_DIMS)[None, :], jnp.cos(ang), 1.0)
            sa = jnp.where((d < 8)[None, :], -jnp.sin(ang), 0.0)
            sb = jnp.where(((d >= 8) & (d < 16))[None, :], jnp.sin(ang), 0.0)
        small = lax.optimization_barrier([jnp.concatenate([t_, t_], axis=1) for t_ in (cos, sa, sb)])
        tabs = [jnp.tile(t_, (bsz, 1)) for t_ in small]
        outs.append(tabs)
    tail = n_rows - sum(b * lp for _, b, lp in segments)
    res = []
    for idx in range(3):
        parts = [o[idx] for o in outs] + [jnp.zeros((tail, LANES), F32)]
        res.append(jnp.concatenate(parts, axis=0))
    return res


def _seq_view(flat, seg, width):
    start, bsz, lp = seg
    return flat[start:start + bsz * lp].reshape(bsz, lp, width)


def _attn_operands(qT, k, vT, seg):
    start, bsz, lp = seg
    chunk = vT.shape[2]
    q_seg = qT[:, start:start + bsz * lp]
    k_seg = k[start:start + bsz * lp].reshape(bsz, lp, k.shape[1])
    v_seg = vT[start // chunk:(start + bsz * lp) // chunk].reshape(bsz, lp // chunk, vT.shape[1], chunk)
    return q_seg, k_seg, v_seg


def _to_flat(parts, n_rows, width, dtype):
    used = sum(p.shape[0] for p in parts)
    return jnp.concatenate(parts + [jnp.zeros((n_rows - used, width), dtype)], axis=0)


def _to_flat_cols(parts, n_rows):
    used = sum(p.shape[1] for p in parts)
    return jnp.concatenate(parts + [jnp.zeros((parts[0].shape[0], n_rows - used), parts[0].dtype)], axis=1)


def _block_avg(n, group):
    idx = np.arange(n) // group
    return jnp.asarray((idx[:, None] == idx[None, :]).astype(np.float32) / group).astype(BF16)


def _forward(xs, p):
    segments = []
    start = 0
    for x in xs:
        bsz, s, _ = x.shape
        lp = s + FRONT
        segments.append((start, bsz, lp))
        start += bsz * lp
    n_used = start
    tm = 512 if n_used >= 4096 else ATTN_CHUNK
    n_rows = _round_up(n_used, tm)

    meta = p['meta_tokens'].astype(F32)
    parts = []
    for x in xs:
        bsz = x.shape[0]
        lead = jnp.concatenate([jnp.zeros((PAD, D_MODEL), F32), meta], axis=0)
        parts.append(jnp.concatenate([jnp.broadcast_to(lead[None], (bsz, FRONT, D_MODEL)), x], axis=1)
                     .reshape(-1, D_MODEL))
    h = _to_flat(parts, n_rows, D_MODEL, F32)

    bs = _block_avg(HGROUP, HEAD_DIM)
    row = lambda t: t.reshape(1, -1).astype(F32)

    depth = p['pre_mix_g'].shape[0]
    for i in range(depth):
        if i % 2 == 0:
            e = i // 2
            cos, sa, sb = _rope_tables(segments, n_rows, "axial")
            ch = np.arange(B_DIM)
            ang = 2.0 * np.pi * ((ch[:, None] * ch[None, :]) % B_DIM) / B_DIM
            cc, sc = jnp.asarray(np.cos(ang), F32), jnp.asarray(np.sin(ang), F32)
            wl = p['b_w'][e].astype(F32)
            hp = lax.Precision.HIGHEST
            pmat = jnp.einsum('cd,gde->gce', cc, wl, precision=hp)
            qmat = -jnp.einsum('cd,gde->gce', sc, wl, precision=hp)
            bd = lambda m: jax.scipy.linalg.block_diag(*[m[g] for g in range(B_GROUPS)])
            pq_base = jnp.concatenate([bd(pmat), bd(qmat)], axis=1)
            qT, k, vT, gf = _row_call(
                "in_even", _in_even_kernel, n_rows, tm, [h, cos, sa, sb],
                [row(p['pre_mix_g'][i]), p['even_w_in'][e].astype(BF16),
                 row(jnp.tile(p['a_q_gain'][e], A_HEADS) * (HEAD_DIM ** -0.5 * LOG2_E)),
                 row(jnp.tile(p['a_k_gain'][e], A_KV_HEADS)), row(p['b_norm_g'][e]), bs,
                 pq_base.astype(BF16)],
                [("cols", A_Q, BF16), ("rows", A_KV, BF16), ("chunks", A_KV, BF16), ("rows", 2 * B_W, F32)])
            ya_parts, yb_parts = [], []
            for seg in segments:
                _, bsz, lp = seg
                ya_parts.append(_attn_a(*_attn_operands(qT, k, vT, seg), bsz))
                gseq = _seq_view(gf, seg, 2 * B_W)[:, PAD:]
                yb = _mixer_b(gseq, row(p['b_b'][e]))
                yb_parts.append(jnp.pad(yb, ((0, 0), (PAD, 0), (0, 0))).reshape(bsz * lp, B_W).astype(BF16))
            yat = _to_flat_cols(ya_parts, n_rows)
            yb = _to_flat(yb_parts, n_rows, B_W, BF16)
            w_out = p['even_w_out'][e].astype(BF16)
            wa, wb = w_out[:A_Q], w_out[A_Q:]
        else:
            o = i // 2
            cos, sa, sb = _rope_tables(segments, n_rows, "partial")
            qT, k, vT, u = _row_call(
                "in_odd", _in_odd_kernel, n_rows, tm, [h, cos, sa, sb],
                [row(p['pre_mix_g'][i]), p['odd_w_in'][o].astype(BF16)],
                [("cols", C_Q, BF16), ("rows", C_KV, BF16), ("chunks", C_KV, BF16), ("rows", D_IN, F32)])
            sink = p['c_sink'][o].astype(F32)
            yat = _to_flat_cols([_attn_c(sink, *_attn_operands(qT, k, vT, seg), seg[1]) for seg in segments],
                                n_rows)
            yb = _mixer_d(u, p, o, bs, segments, n_rows, tm)
            w_out = p['odd_w_out'][o].astype(BF16)
            wa, wb = w_out[:C_Q], w_out[C_Q:]
        (h,) = _row_call("out_proj", _out_proj_kernel, n_rows, tm, [h, yb],
                         [wa, wb, row(p['post_mix_g'][i])], [("rows", D_MODEL, F32)], col_ins=[yat])
        (h,) = _row_call(
            "ffn", functools.partial(_ffn_kernel, chunk=256), n_rows, tm, [h],
            [row(p['pre_ffn_g'][i]), p['ffn_w_gate'][i].astype(BF16), p['ffn_w_up'][i].astype(BF16),
             p['ffn_w_down'][i].astype(BF16), row(p['post_ffn_g'][i])],
            [("rows", D_MODEL, F32)], scratch=[pltpu.VMEM((tm, D_MODEL), F32)])

    outs = []
    for seg in segments:
        outs.append(_seq_view(h, seg, D_MODEL)[:, FRONT:])
    return tuple(outs)


def _mixer_d(u, p, o, bs, segments, n_rows, tm):
    row = lambda t: t.reshape(1, -1).astype(F32)
    w = D_WIDTH
    zeros = lambda r: jnp.zeros((r, w), F32)
    w2 = jnp.concatenate([jnp.concatenate([p['d_w_up'][o][0], zeros(DECAY_RANK)], axis=1),
                          jnp.concatenate([zeros(DECAY_RANK), p['d_w_up'][o][1]], axis=1)], axis=0)
    a2 = jnp.concatenate([jnp.concatenate([p['d_a_up'][o][0], zeros(ICLR_RANK)], axis=1),
                          jnp.concatenate([zeros(ICLR_RANK), p['d_a_up'][o][1]], axis=1)], axis=0)
    consts = [row(p['d_mu_prev'][o]), row(p['d_mu_next'][o]), w2.astype(BF16), a2.astype(BF16),
              p['d_g_up'][o].astype(BF16), row(p['d_w0'][o]), row(p['d_a0'][o]),
              row(p['d_k_k'][o]), row(p['d_k_a'][o]), row(p['d_r_k'][o]), bs]
    nb8 = n_rows // 8
    t8 = tm // 8
    in_specs = [pl.BlockSpec((tm, D_IN), lambda i: (i, 0)),
                pl.BlockSpec((8, D_IN), lambda i: (jnp.maximum(i * t8 - 1, 0), 0)),
                pl.BlockSpec((8, D_IN), lambda i: (jnp.minimum((i + 1) * t8, nb8 - 1), 0))]
    in_specs += [pl.BlockSpec(a.shape, lambda i, nd=a.ndim: (0,) * nd) for a in consts]
    n_out = 11
    prep = pl.pallas_call(
        functools.partial(_d_prep_kernel, tm=tm, segments=tuple(segments)),
        grid=(n_rows // tm,), in_specs=in_specs,
        out_specs=[pl.BlockSpec((tm, w), lambda i: (i, 0))] * n_out,
        out_shape=[jax.ShapeDtypeStruct((n_rows, w), F32)] * n_out,
        compiler_params=_cparams(("parallel",)), name="d_prep",
    )(u, u, u, *consts)
    r, v, a, lwf, lwb, kdf, kdb, bf_, bb_, bonus, g = prep
    of, ob = _scan(segments, n_rows, r, v, a, lwf, lwb, kdf, kdb, bf_, bb_)
    (yd,) = _row_call("d_post", _d_post_kernel, n_rows, tm, [of, ob, bonus, g],
                      [row(p['d_ln_g'][o]), row(p['d_ln_b'][o]), bs], [("rows", w, BF16)])
    return yd


def kernel(x_prompt, x_sample, meta_tokens, pre_mix_g, post_mix_g, pre_ffn_g, post_ffn_g, even_w_in, even_w_out, a_q_gain, a_k_gain, b_norm_g, b_w, b_b, odd_w_in, odd_w_out, c_sink, d_mu_prev, d_mu_next, d_w0, d_w_up, d_a0, d_a_up, d_g_up, d_k_k, d_k_a, d_r_k, d_ln_g, d_ln_b, ffn_w_gate, ffn_w_up, ffn_w_down):
    params = dict(meta_tokens=meta_tokens, pre_mix_g=pre_mix_g, post_mix_g=post_mix_g,
                  pre_ffn_g=pre_ffn_g, post_ffn_g=post_ffn_g,
                  even_w_in=even_w_in, even_w_out=even_w_out, a_q_gain=a_q_gain, a_k_gain=a_k_gain,
                  b_norm_g=b_norm_g, b_w=b_w, b_b=b_b,
                  odd_w_in=odd_w_in, odd_w_out=odd_w_out, c_sink=c_sink,
                  d_mu_prev=d_mu_prev, d_mu_next=d_mu_next, d_w0=d_w0, d_w_up=d_w_up,
                  d_a0=d_a0, d_a_up=d_a_up, d_g_up=d_g_up, d_k_k=d_k_k, d_k_a=d_k_a, d_r_k=d_r_k,
                  d_ln_g=d_ln_g, d_ln_b=d_ln_b,
                  ffn_w_gate=ffn_w_gate, ffn_w_up=ffn_w_up, ffn_w_down=ffn_w_down)
    return _forward([x_prompt, x_sample], params)
```

```python
import functools
import math

import numpy as np
import jax
import jax.numpy as jnp
from jax import lax
from jax.experimental import pallas as pl
from jax.experimental.pallas import tpu as pltpu

F32 = jnp.float32
BF16 = jnp.bfloat16

D_MODEL = 1024
HEAD_DIM = 64
N_META = 16
GRID_W = 64
WINDOW = 128
RMS_EPS = 1e-6
A_HEADS, A_KV_HEADS, A_THETA = 12, 4, 10000.0
B_GROUPS, B_DIM = 4, 64
C_HEADS, C_KV_HEADS = 8, 2
ROPE_THETA = 500000.0
ROPE_DIMS = HEAD_DIM // 4
D_HEADS = 8
D_WIDTH = D_HEADS * HEAD_DIM
DECAY_RANK, ICLR_RANK, GATE_RANK = 64, 64, 128
LNX_EPS = 64e-5
D_FF = 2816
A_Q, A_KV, B_W = A_HEADS * HEAD_DIM, A_KV_HEADS * HEAD_DIM, B_GROUPS * B_DIM
EVEN_IN = A_Q + 2 * A_KV + B_W
C_Q, C_KV = C_HEADS * HEAD_DIM, C_KV_HEADS * HEAD_DIM
D_IN = 3 * D_WIDTH + 2 * DECAY_RANK + 2 * ICLR_RANK + GATE_RANK
ODD_IN = C_Q + 2 * C_KV + D_IN

LANES = 128
FRONT = 256
PAD = FRONT - N_META
NEG = -1e30
SCAN_CHUNK = 64
SCAN_CHUNKS_PER_STEP = 4
HGROUP = 256
FFT_L1 = 16
VMEM_LIMIT = 56 * 1024 * 1024
ATTN_CHUNK = 256
ATTN_MAX_CHUNKS = 20
ATTN_UNROLL = 8
LOG2_E = math.log2(math.e)


def _round_up(x, m):
    return (x + m - 1) // m * m


def _cparams(sem):
    return pltpu.CompilerParams(dimension_semantics=sem, vmem_limit_bytes=VMEM_LIMIT)


def _dot(a, b):
    return jnp.dot(a, b, preferred_element_type=F32)


def _dot_nt(a, b):
    return lax.dot_general(a, b, (((1,), (1,)), ((), ())), preferred_element_type=F32)


def _dot_tn(a, b):
    return lax.dot_general(a, b, (((0,), (0,)), ((), ())), preferred_element_type=F32)


def _split2(x):
    hi = x.astype(BF16)
    lo = (x - hi.astype(F32)).astype(BF16)
    return hi, lo


def _group_mean(x, bs):
    hi, lo = _split2(x)
    return _dot(hi, bs) + _dot(lo, bs)


def _rope(x, cos, sin_a, sin_b, shift):
    n = x.shape[1]
    return x * cos + pltpu.roll(x, n - shift, 1) * sin_a + pltpu.roll(x, shift, 1) * sin_b


def _wide(t, n):
    return t if n == LANES else jnp.concatenate([t] * (n // LANES), axis=1)


def _rms_rows(x, g):
    ms = jnp.mean(x * x, axis=-1, keepdims=True)
    return x * lax.rsqrt(ms + RMS_EPS) * g


def _store_vt(v_ref, v):
    n = v.shape[1]
    for r in range(v.shape[0] // n):
        v_ref[r] = v[r * n:(r + 1) * n, :].T.astype(BF16)


def _in_even_kernel(h_ref, cos_ref, sa_ref, sb_ref, g_ref, w_ref, qg_ref, kg_ref, bg_ref, bs_ref, pq_ref,
                    q_ref, k_ref, v_ref, gf_ref):
    hn = _rms_rows(h_ref[...], g_ref[...]).astype(BF16)
    proj = _dot(hn, w_ref[...])
    bs = bs_ref[...]
    cos, sa, sb = (_wide(r[...], HGROUP) for r in (cos_ref, sa_ref, sb_ref))

    def norm_rope(x, gain):
        xn = x * lax.rsqrt(_group_mean(x * x, bs) + RMS_EPS) * gain
        return _rope(xn, cos, sa, sb, 16)

    for s in range(A_Q // HGROUP):
        cs = slice(s * HGROUP, (s + 1) * HGROUP)
        q_ref[cs, :] = norm_rope(proj[:, cs], qg_ref[:, cs]).T.astype(BF16)
    k_ref[...] = norm_rope(proj[:, A_Q:A_Q + A_KV], kg_ref[...]).astype(BF16)
    _store_vt(v_ref, proj[:, A_Q + A_KV:A_Q + 2 * A_KV])
    f = proj[:, A_Q + 2 * A_KV:]
    fn = f * lax.rsqrt(_group_mean(f * f, bs) + RMS_EPS) * bg_ref[...]
    gf_ref[...] = _dot(fn.astype(BF16), pq_ref[...])


def _in_odd_kernel(h_ref, cos_ref, sa_ref, sb_ref, g_ref, w_ref, q_ref, k_ref, v_ref, u_ref):
    hn = _rms_rows(h_ref[...], g_ref[...]).astype(BF16)
    proj = _dot(hn, w_ref[...])
    cos, sa, sb = cos_ref[...], sa_ref[...], sb_ref[...]
    cos2, sa2, sb2 = (_wide(t, HGROUP) for t in (cos, sa, sb))
    for s in range(C_Q // HGROUP):
        cs = slice(s * HGROUP, (s + 1) * HGROUP)
        q_ref[cs, :] = (_rope(proj[:, cs], cos2, sa2, sb2, 8) * (HEAD_DIM ** -0.5 * LOG2_E)).T.astype(BF16)
    k_ref[...] = _rope(proj[:, C_Q:C_Q + C_KV], cos, sa, sb, 8).astype(BF16)
    _store_vt(v_ref, proj[:, C_Q + C_KV:C_Q + 2 * C_KV])
    u_ref[...] = proj[:, C_Q + 2 * C_KV:]


def _mix_ffn_kernel(h_ref, yb_ref, wa_ref, wb_ref, gm_ref, g1_ref, wg_ref, wu_ref, wd_ref, g2_ref,
                    yat_ref, o_ref, acc_ref, *, chunk):
    mix = _dot_tn(yat_ref[...], wa_ref[...]) + _dot(yb_ref[...], wb_ref[...])
    h = h_ref[...] + _rms_rows(mix, gm_ref[...])
    hn = _rms_rows(h, g1_ref[...]).astype(BF16)
    for c in range(D_FF // chunk):
        cs = slice(c * chunk, (c + 1) * chunk)
        gate = _dot(hn, wg_ref[:, cs])
        up = _dot(hn, wu_ref[:, cs])
        act = (gate * (1.0 / (1.0 + jnp.exp(-gate))) * up).astype(BF16)
        part = _dot(act, wd_ref[cs, :])
        if c == 0:
            acc_ref[...] = part
        else:
            acc_ref[...] += part
    o_ref[...] = h + _rms_rows(acc_ref[...], g2_ref[...])


def _row_call(name, kernel, n_rows, tm, row_ins, const_ins, outs, scratch=(), col_ins=()):
    grid = (n_rows // tm,)
    in_specs = [pl.BlockSpec((tm, a.shape[1]), lambda i: (i, 0)) for a in row_ins]
    in_specs += [pl.BlockSpec(a.shape, lambda i, nd=a.ndim: (0,) * nd, pipeline_mode=pl.Buffered(1))
                 for a in const_ins]
    in_specs += [pl.BlockSpec((a.shape[0], tm), lambda i: (0, i)) for a in col_ins]
    out_specs, out_shape = [], []
    for kind, n, dt in outs:
        if kind == "rows":
            out_specs.append(pl.BlockSpec((tm, n), lambda i: (i, 0)))
            out_shape.append(jax.ShapeDtypeStruct((n_rows, n), dt))
        elif kind == "cols":
            out_specs.append(pl.BlockSpec((n, tm), lambda i: (0, i)))
            out_shape.append(jax.ShapeDtypeStruct((n, n_rows), dt))
        else:
            out_specs.append(pl.BlockSpec((tm // n, n, n), lambda i: (i, 0, 0)))
            out_shape.append(jax.ShapeDtypeStruct((n_rows // n, n, n), dt))
    return pl.pallas_call(
        kernel, grid=grid, in_specs=in_specs, out_specs=out_specs, out_shape=out_shape,
        scratch_shapes=list(scratch), compiler_params=_cparams(("parallel",)), name=name,
    )(*row_ins, *const_ins, *col_ins)


def _padded_q(q_ref, h, kv):
    qh = q_ref[h * HEAD_DIM:(h + 1) * HEAD_DIM, :]
    zero = jnp.zeros_like(qh)
    return jnp.concatenate([qh, zero] if kv % 2 == 0 else [zero, qh], axis=0)


def _attn_a_kernel(qT_ref, k_ref, vT_ref, o_ref, m_scr, l_scr, acc_scr, s_scr, *, nchunk, nkb, tq):
    j = pl.program_id(2)

    @pl.when(j == 0)
    def _():
        m_scr[...] = jnp.full(m_scr.shape, NEG, F32)
        l_scr[...] = jnp.zeros(l_scr.shape, F32)
        acc_scr[...] = jnp.zeros(acc_scr.shape, F32)

    group = A_HEADS // A_KV_HEADS
    ck = ATTN_CHUNK
    key_row = lax.broadcasted_iota(jnp.int32, (ck, tq), 0) + j * (nchunk * ck)

    def fold(s, op):
        return op(s.reshape(ck // 8, 8, tq), axis=0)

    def gang_q(g):
        return [_padded_q(qT_ref, g * group + t, g) for t in range(group)]

    def gang_scores(g, qps, c, mxs, first=False):
        kc = k_ref[0, pl.ds(pl.multiple_of(c * ck, ck), ck), (g // 2) * LANES:(g // 2 + 1) * LANES]
        out = []
        for t in range(group):
            s = _dot(kc, qps[t])
            if first:
                s = jnp.where(key_row >= PAD, s, NEG)
            s_scr[g % 2, t, c] = s
            out.append(fold(s, jnp.max) if mxs is None else jnp.maximum(mxs[t], fold(s, jnp.max)))
        return tuple(out)

    def gang_values(g, c, m_news):
        vc = vT_ref[0, c, g * HEAD_DIM:(g + 1) * HEAD_DIM, :]
        ps = [jnp.exp2(s_scr[g % 2, t, c] - m_news[t]) for t in range(group)]
        return tuple(_dot(vc, p.astype(BF16)) for p in ps), tuple(fold(p, jnp.sum) for p in ps)

    def add(xs, ys):
        return tuple(x + y for x, y in zip(xs, ys))

    unroll = max(u for u in range(1, ATTN_UNROLL + 1) if max(nchunk - 1, 1) % u == 0)
    qps = gang_q(0)
    mxs = lax.fori_loop(1, nchunk, lambda c, mxs, qps=qps: gang_scores(0, qps, c, mxs),
                        gang_scores(0, qps, 0, None, first=True), unroll=unroll)
    for g in range(A_KV_HEADS):
        heads = [g * group + t for t in range(group)]
        m_olds = [m_scr[h] for h in heads]
        m_news = [jnp.maximum(mo, jnp.max(mx, axis=0, keepdims=True)) for mo, mx in zip(m_olds, mxs)]
        if g + 1 < A_KV_HEADS:
            qps = gang_q(g + 1)

            def body(c, carry, g=g, m_news=m_news, qps=qps):
                accs, lsums, mxn = carry
                mxn = gang_scores(g + 1, qps, c, mxn)
                pvs, pss = gang_values(g, c, m_news)
                return add(accs, pvs), add(lsums, pss), mxn

            mx0 = gang_scores(g + 1, qps, 0, None, first=True)
            accs, lsums, mxs = lax.fori_loop(1, nchunk, body, (*gang_values(g, 0, m_news), mx0),
                                             unroll=unroll)
        else:
            def tail(c, carry, g=g, m_news=m_news):
                pvs, pss = gang_values(g, c, m_news)
                return add(carry[0], pvs), add(carry[1], pss)

            accs, lsums = lax.fori_loop(1, nchunk, tail, gang_values(g, 0, m_news), unroll=unroll)
        for t, h in enumerate(heads):
            alpha = jnp.exp2(m_olds[t] - m_news[t])
            acc_scr[h] = acc_scr[h] * alpha + accs[t]
            l_scr[h] = l_scr[h] * alpha + jnp.sum(lsums[t], axis=0, keepdims=True)
            m_scr[h] = m_news[t]

    @pl.when(j == nkb - 1)
    def _():
        for h in range(A_HEADS):
            o_ref[h * HEAD_DIM:(h + 1) * HEAD_DIM, :] = (acc_scr[h] / l_scr[h]).astype(o_ref.dtype)


def _attn_a(qT, k, vT, start):
    bsz, lp = k.shape[:2]
    off = start // ATTN_CHUNK
    tq = ATTN_CHUNK
    total = lp // ATTN_CHUNK
    nkb = min(n for n in range(1, total + 1) if total % n == 0 and total // n <= ATTN_MAX_CHUNKS)
    tk = lp // nkb
    nchunk = tk // ATTN_CHUNK
    nq = lp // tq
    kernel = functools.partial(_attn_a_kernel, nchunk=nchunk, nkb=nkb, tq=tq)
    return pl.pallas_call(
        kernel, grid=(bsz, nq, nkb),
        in_specs=[
            pl.BlockSpec((A_Q, tq), lambda b, i, j: (0, off + b * nq + i)),
            pl.BlockSpec((1, tk, A_KV), lambda b, i, j: (b, j, 0)),
            pl.BlockSpec((1, nchunk, A_KV, ATTN_CHUNK), lambda b, i, j: (b, j, 0, 0)),
        ],
        out_specs=pl.BlockSpec((A_Q, tq), lambda b, i, j: (0, b * nq + i)),
        out_shape=jax.ShapeDtypeStruct((A_Q, bsz * lp), BF16),
        scratch_shapes=[
            pltpu.VMEM((A_HEADS, 1, tq), F32),
            pltpu.VMEM((A_HEADS, 1, tq), F32),
            pltpu.VMEM((A_HEADS, HEAD_DIM, tq), F32),
            pltpu.VMEM((2, A_HEADS // A_KV_HEADS, nchunk, ATTN_CHUNK, tq), F32),
        ],
        compiler_params=_cparams(("parallel", "parallel", "arbitrary")), name="attn_a",
    )(qT, k, vT)


def _attn_c_kernel(sink_ref, qT_ref, k0, k1, k2, k3, v0, v1, v2, v3, o_ref, *, lp):
    j = pl.program_id(1)
    k_refs, v_refs = (k0, k1, k2, k3), (v0, v1, v2, v3)
    row = lax.broadcasted_iota(jnp.int32, (LANES, LANES), 0)
    pq = lax.broadcasted_iota(jnp.int32, (LANES, LANES), 1) + j * LANES
    biases = []
    for slot in range(4):
        if slot == 0:
            bias = jnp.where(row >= PAD % LANES, 0.0, NEG)
        else:
            pk = row + (j + slot - 2) * LANES
            in_window = jnp.where(jnp.abs(pq - pk) <= WINDOW, 0.0, NEG)
            bias = jnp.where(pk >= FRONT, jnp.where(pk < lp, in_window, NEG), NEG)
        biases.append(bias.astype(F32))
    group = C_HEADS // C_KV_HEADS
    for h in range(C_HEADS):
        kv = h // group
        qp = _padded_q(qT_ref, h, kv)
        sink = sink_ref[h] * LOG2_E
        ss = [_dot(k_refs[t][0], qp) + biases[t] for t in range(4)]
        m = jnp.maximum(jnp.maximum(ss[0], ss[1]), jnp.maximum(ss[2], ss[3]))
        m = jnp.maximum(jnp.max(m, axis=0, keepdims=True), sink)
        acc = jnp.zeros((HEAD_DIM, LANES), F32)
        denom = jnp.exp2(sink - m)
        for t in range(4):
            p = jnp.exp2(ss[t] - m)
            acc = acc + _dot(v_refs[t][0, 0, kv * HEAD_DIM:(kv + 1) * HEAD_DIM, :], p.astype(BF16))
            denom = denom + jnp.sum(p, axis=0, keepdims=True)
        o_ref[h * HEAD_DIM:(h + 1) * HEAD_DIM, :] = (acc / denom).astype(o_ref.dtype)


def _attn_c(sink, qT, k, vT, start):
    bsz, lp = k.shape[:2]
    nb = lp // LANES
    off = start // LANES
    kernel = functools.partial(_attn_c_kernel, lp=lp)

    def kspec(fn):
        return pl.BlockSpec((1, LANES, C_KV), lambda b, j: (b, fn(j), 0))

    def vspec(fn):
        return pl.BlockSpec((1, 1, C_KV, LANES), lambda b, j: (b, fn(j), 0, 0))

    fns = (lambda j: PAD // LANES, lambda j: jnp.maximum(j - 1, 0), lambda j: j,
           lambda j: jnp.minimum(j + 1, nb - 1))
    return pl.pallas_call(
        kernel, grid=(bsz, nb),
        in_specs=[pl.BlockSpec(memory_space=pltpu.SMEM),
                  pl.BlockSpec((C_Q, LANES), lambda b, j: (0, off + b * nb + j))]
                 + [kspec(f) for f in fns] + [vspec(f) for f in fns],
        out_specs=pl.BlockSpec((C_Q, LANES), lambda b, j: (0, b * nb + j)),
        out_shape=jax.ShapeDtypeStruct((C_Q, bsz * lp), BF16),
        compiler_params=_cparams(("parallel", "parallel")), name="attn_c",
    )(sink, qT, k, k, k, k, vT, vT, vT, vT)


def _fft1_kernel(x_ref, c1_ref, s1_ref, o_ref):
    x = x_ref[0]
    tn = x.shape[1]
    lane = lax.broadcasted_iota(jnp.int32, x.shape, 1)
    is_re = (lane & (2 * B_W - 1)) < B_W
    xs = jnp.where(is_re, pltpu.roll(x, tn - B_W, 1), -pltpu.roll(x, B_W, 1))
    o_ref[0] = _dot(c1_ref[...], x.astype(BF16)) + _dot(s1_ref[...], xs.astype(BF16))


def _fft2_kernel(a_ref, tc_ref, ts_ref, c2_ref, s2_ref, bias_ref, o_ref):
    a = a_ref[0, 0]
    tc, ts = _wide(tc_ref[0], B_W), _wide(ts_ref[0], B_W)
    are, aim = a[:, :B_W], a[:, B_W:]
    bre = (are * tc + aim * ts).astype(BF16)
    bim = (aim * tc - are * ts).astype(BF16)
    o_ref[0] = _dot(c2_ref[...], bre) + _dot(s2_ref[...], bim) + bias_ref[...]


def _dft_tables(l2, l2p, l2o):
    l1 = FFT_L1
    length = l1 * l2
    n1 = np.arange(l1)
    ang1 = 2.0 * np.pi * ((n1[:, None] * n1[None, :]) % l1) / l1
    c1, s1 = np.cos(ang1), np.sin(ang1)
    n2 = np.arange(l2)
    angt = 2.0 * np.pi * (n1[:, None] * n2[None, :]) / length
    tc = np.zeros((l1, l2p, LANES), np.float32)
    ts = np.zeros((l1, l2p, LANES), np.float32)
    tc[:, :l2, :] = np.cos(angt)[:, :, None]
    ts[:, :l2, :] = np.sin(angt)[:, :, None]
    ang2 = 2.0 * np.pi * ((n2[:, None] * n2[None, :]) % l2) / l2
    scale = 1.0 / math.sqrt(B_DIM * length)
    c2 = np.zeros((l2o, l2p), np.float32)
    s2 = np.zeros((l2o, l2p), np.float32)
    c2[:l2, :l2] = np.cos(ang2) * scale
    s2[:l2, :l2] = np.sin(ang2) * scale
    as_bf = lambda t: jnp.asarray(t, F32).astype(BF16)
    return as_bf(c1), as_bf(s1), jnp.asarray(tc), jnp.asarray(ts), as_bf(c2), as_bf(s2)


def _mixer_b(gseq, bias):
    bsz, length, _ = gseq.shape
    l1 = FFT_L1
    l2 = length // l1
    l2p, l2o = _round_up(l2, LANES), _round_up(l2, 8)
    c1, s1, tc, ts, c2, s2 = _dft_tables(l2, l2p, l2o)
    x = jnp.pad(gseq.reshape(bsz, l1, l2, 2 * B_W), ((0, 0), (0, 0), (0, l2p - l2), (0, 0)))
    x = x.reshape(bsz, l1, l2p * 2 * B_W)
    ncol = l2p * 2 * B_W
    tn = 2 * B_W * 48 if l2p % 48 == 0 else 2 * B_W
    a = pl.pallas_call(
        _fft1_kernel, grid=(bsz, ncol // tn),
        in_specs=[pl.BlockSpec((1, l1, tn), lambda b, i: (b, 0, i)),
                  pl.BlockSpec((l1, l1), lambda b, i: (0, 0)),
                  pl.BlockSpec((l1, l1), lambda b, i: (0, 0))],
        out_specs=pl.BlockSpec((1, l1, tn), lambda b, i: (b, 0, i)),
        out_shape=jax.ShapeDtypeStruct((bsz, l1, ncol), F32),
        compiler_params=_cparams(("parallel", "parallel")), name="fft1",
    )(x, c1, s1)
    a = a.reshape(bsz, l1, l2p, 2 * B_W)
    y = pl.pallas_call(
        _fft2_kernel, grid=(bsz, l1),
        in_specs=[pl.BlockSpec((1, 1, l2p, 2 * B_W), lambda b, k: (b, k, 0, 0)),
                  pl.BlockSpec((1, l2p, LANES), lambda b, k: (k, 0, 0)),
                  pl.BlockSpec((1, l2p, LANES), lambda b, k: (k, 0, 0)),
                  pl.BlockSpec((l2o, l2p), lambda b, k: (0, 0)),
                  pl.BlockSpec((l2o, l2p), lambda b, k: (0, 0)),
                  pl.BlockSpec((1, B_W), lambda b, k: (0, 0))],
        out_specs=pl.BlockSpec((1, l2o, B_W), lambda b, k: (b, 0, k)),
        out_shape=jax.ShapeDtypeStruct((bsz, l2o, l1 * B_W), F32),
        compiler_params=_cparams(("parallel", "parallel")), name="fft2",
    )(a, tc, ts, c2, s2, bias)
    return y[:, :l2].reshape(bsz, length, B_W)


def _seq_position(rows, segments):
    pos = jnp.full(rows.shape, -1.0, F32)
    seqlen = jnp.full(rows.shape, 1.0, F32)
    for start, bsz, lp in segments:
        rel = rows - float(start)
        q = jnp.floor((rel + 0.5) * (1.0 / lp))
        inside = jnp.where(rel >= 0.0, jnp.where(rel < float(bsz * lp), 1.0, 0.0), 0.0) > 0.5
        pos = jnp.where(inside, rel - q * lp, pos)
        seqlen = jnp.where(inside, float(lp), seqlen)
    return pos, seqlen


def _sigmoid(x):
    return 1.0 / (1.0 + jnp.exp(-x))


def _d_prep_kernel(u_ref, up_ref, un_ref, mup_ref, mun_ref, w2_ref, a2_ref, gup_ref, w0_ref, a0_ref,
                   kk_ref, ka_ref, rk_ref, bs_ref,
                   r_ref, v_ref, a_ref, lwf_ref, lwb_ref, kdf_ref, kdb_ref, bf_ref, bb_ref,
                   bonus_ref, g_ref, *, tm, segments):
    i = pl.program_id(0)
    u = u_ref[...]
    rows = (lax.broadcasted_iota(jnp.int32, (tm, 1), 0) + i * tm).astype(F32)
    pos, seqlen = _seq_position(rows, segments)
    local = lax.broadcasted_iota(jnp.int32, (tm, 1), 0)
    u_prev = jnp.where(local == 0, up_ref[7:8, :], pltpu.roll(u, 1, 0))
    u_next = jnp.where(local == tm - 1, un_ref[0:1, :], pltpu.roll(u, tm - 1, 0))
    u_prev = jnp.where(pos == float(PAD), 0.0, u_prev)
    u_next = jnp.where(pos == seqlen - 1.0, 0.0, u_next)
    u = u + mup_ref[...] * (u_prev - u) + mun_ref[...] * (u_next - u)
    valid = jnp.where(pos >= float(PAD), 1.0, 0.0)

    w = D_WIDTH
    r, k, v = u[:, :w], u[:, w:2 * w], u[:, 2 * w:3 * w]
    c0 = 3 * w
    dec = _dot(jnp.tanh(u[:, c0:c0 + 2 * DECAY_RANK]).astype(BF16), w2_ref[...]) + w0_ref[...]
    c0 += 2 * DECAY_RANK
    icl = _dot(u[:, c0:c0 + 2 * ICLR_RANK].astype(BF16), a2_ref[...]) + a0_ref[...]
    c0 += 2 * ICLR_RANK
    g_ref[...] = _dot(_sigmoid(u[:, c0:c0 + GATE_RANK]).astype(BF16), gup_ref[...])

    bs = bs_ref[...]

    def head_sum(x):
        return jnp.concatenate(
            [_group_mean(x[:, s * HGROUP:(s + 1) * HGROUP], bs) for s in range(w // HGROUP)], axis=1
        ) * float(HEAD_DIM)

    kk = k * kk_ref[...]
    kk = kk * lax.rsqrt(jnp.maximum(head_sum(kk * kk), 1e-24))
    r_ref[...] = r
    v_ref[...] = v
    a_ref[...] = -kk * valid
    bonus = jnp.zeros_like(r)
    for d, (lw_ref, kd_ref, b_ref) in enumerate(((lwf_ref, kdf_ref, bf_ref), (lwb_ref, kdb_ref, bb_ref))):
        x = -dec[:, d * w:(d + 1) * w]
        softplus = jnp.maximum(x, 0.0) + jnp.log(1.0 + jnp.exp(-jnp.abs(x)))
        lw_ref[...] = -jnp.exp(-softplus - 0.5)
        gate = _sigmoid(icl[:, d * w:(d + 1) * w])
        kd = k * (1.0 + (gate - 1.0) * ka_ref[...])
        kd_ref[...] = kd * valid
        b_ref[...] = kk * gate * valid
        bonus = bonus + head_sum(r * kd * rk_ref[...]) * v
    bonus_ref[...] = bonus


def _scan_kernel(fwd_blk, bwd_blk, first,
                 rf, vf, af, lwf, kdf, bf_, rb, vb, ab, lwb, kdb, bb_,
                 of_ref, ob_ref, s_scr, *, nk):
    del fwd_blk, bwd_blk
    step = pl.program_id(0)
    c = SCAN_CHUNK

    @pl.when(first[step] == 1)
    def _():
        s_scr[...] = jnp.zeros(s_scr.shape, F32)

    t_sq = lax.broadcasted_iota(jnp.int32, (nk * c, nk * c), 0)
    s_sq = lax.broadcasted_iota(jnp.int32, (nk * c, nk * c), 1)
    t_cat = lax.broadcasted_iota(jnp.int32, (c, HGROUP), 0)
    s_cat = lax.broadcasted_iota(jnp.int32, (c, HGROUP), 1) & (c - 1)
    bd_row = lax.broadcasted_iota(jnp.int32, (HGROUP, HGROUP), 0) >> 6
    bd_col = lax.broadcasted_iota(jnp.int32, (HGROUP, HGROUP), 1) >> 6
    bd_mask = bd_row == bd_col
    eye_cat = jnp.where(t_cat == s_cat, 1.0, 0.0).astype(F32)

    def block_diag(x):
        return jnp.where(bd_mask, jnp.concatenate([x] * (HGROUP // c), axis=0), 0.0).astype(BF16)

    chains = []
    dirs = ((0, rf, vf, af, lwf, kdf, bf_), (1, rb, vb, ab, lwb, kdb, bb_))
    for d, r_ref, v_ref, a_ref, lw_ref, kd_ref, b_ref in dirs:
        rev = d == 1
        earlier = (s_sq >= t_sq) if rev else (s_sq <= t_sq)
        tri = jnp.where((t_sq >> 6) == (s_sq >> 6), jnp.where(earlier, 1.0, 0.0), 0.0).astype(BF16)
        lw = lw_ref[...]
        h1 = lw.astype(BF16)
        r1 = lw - h1.astype(F32)
        h2 = r1.astype(BF16)
        h3 = (r1 - h2.astype(F32)).astype(BF16)
        cum = _dot(tri, h1) + _dot(tri, h2) + _dot(tri, h3)
        e_in = jnp.exp(cum)
        e_neg = jnp.exp(-cum)
        a_t = a_ref[...] * jnp.exp(cum - lw)
        r_t = r_ref[...] * e_in
        b_t = b_ref[...] * e_neg
        k_t = kd_ref[...] * e_neg
        v_all = v_ref[...]
        for ck in range(nk):
            rows = slice(ck * c, (ck + 1) * c)
            w_row = ck * c if rev else (ck + 1) * c - 1
            for g in range(D_WIDTH // HGROUP):
                cs = slice(g * HGROUP, (g + 1) * HGROUP)
                chains.append(dict(d=d, ck=ck, g=g, rev=rev, at=a_t[rows, cs], rt=r_t[rows, cs],
                                   bt=b_t[rows, cs], kt=k_t[rows, cs], vc=v_all[rows, cs],
                                   w=e_in[w_row:w_row + 1, cs]))

    def strict(ch):
        return (s_cat > t_cat) if ch['rev'] else (s_cat < t_cat)

    def incl(ch):
        return (s_cat >= t_cat) if ch['rev'] else (s_cat <= t_cat)

    for ch in chains:
        ch['lhs'] = jnp.concatenate([ch['at'], ch['rt']], axis=0).astype(BF16)
        ch['bd_v'] = block_diag(ch['vc'])
    m_b = [_dot_nt(ch['lhs'], block_diag(ch['bt'])) for ch in chains]
    m_k = [_dot_nt(ch['lhs'], block_diag(ch['kt'])) for ch in chains]
    for ch, mb, mk in zip(chains, m_b, m_k):
        ch['a_ab'] = jnp.where(strict(ch), mb[:c], 0.0)
        ch['a_rb'] = jnp.where(incl(ch), mb[c:], 0.0).astype(BF16)
        ch['a_ak'] = jnp.where(strict(ch), mk[:c], 0.0).astype(BF16)
        ch['a_rk'] = jnp.where(incl(ch), mk[c:], 0.0).astype(BF16)
        ch['t'] = eye_cat + jnp.where((t_cat >> 1) == (s_cat >> 1), ch['a_ab'], 0.0)
    m = 2
    while m < c:
        sh = int(math.log2(m))
        xs = []
        for ch in chains:
            a_m = jnp.where((t_cat >> (sh + 1)) == (s_cat >> (sh + 1)),
                            jnp.where((t_cat >> sh) != (s_cat >> sh), ch['a_ab'], 0.0), 0.0)
            xs.append(_dot(a_m.astype(BF16), block_diag(ch['t'])))
        for ch, x in zip(chains, xs):
            ch['t'] = ch['t'] + _dot(ch['t'].astype(BF16), block_diag(x))
        m *= 2
    akv = [_dot(ch['a_ak'], ch['bd_v']) for ch in chains]
    for ch in chains:
        ch['t'] = ch['t'].astype(BF16)
    a_hat = [_dot(ch['t'], block_diag(ch['at'])) for ch in chains]
    u_hat = [_dot(ch['t'], block_diag(x)) for ch, x in zip(chains, akv)]
    r_hat = [(ch['rt'] + _dot(ch['a_rb'], block_diag(x))).astype(BF16) for ch, x in zip(chains, a_hat)]
    o_hat = [_dot(ch['a_rb'], block_diag(x)) + _dot(ch['a_rk'], ch['bd_v']) for ch, x in zip(chains, u_hat)]
    m_c = [jnp.where(bd_mask, _dot_tn(x.astype(BF16), ch['bt'].astype(BF16)), 0.0).astype(BF16)
           for ch, x in zip(chains, a_hat)]
    n_c = [jnp.where(bd_mask, _dot_tn(jnp.concatenate([x, ch['vc']], axis=0).astype(BF16),
                                      jnp.concatenate([ch['bt'], ch['kt']], axis=0).astype(BF16)), 0.0)
           for ch, x in zip(chains, u_hat)]
    pre = {(ch['d'], ch['ck'], ch['g']): (r_hat[i], o_hat[i], m_c[i], n_c[i], ch['w'])
           for i, ch in enumerate(chains)}

    groups = [(d, g) for d in range(2) for g in range(D_WIDTH // HGROUP)]
    states = {dg: s_scr[dg[0], dg[1]] for dg in groups}
    for i in range(nk):
        st_b = {dg: states[dg].astype(BF16) for dg in groups}
        for d, g in groups:
            ck = nk - 1 - i if d == 1 else i
            rh, oh, mc, nc_, w = pre[(d, ck, g)]
            o_ref = ob_ref if d == 1 else of_ref
            o_ref[ck * c:(ck + 1) * c, g * HGROUP:(g + 1) * HGROUP] = _dot_nt(rh, st_b[(d, g)]) + oh
            states[(d, g)] = (states[(d, g)] + _dot(st_b[(d, g)], mc) + nc_) * w
    for d, g in groups:
        s_scr[d, g] = states[(d, g)]


def _d_post_kernel(of_ref, ob_ref, bonus_ref, g_ref, lng_ref, lnb_ref, bs_ref, o_ref):
    o = of_ref[...] + ob_ref[...]
    bs = bs_ref[...]

    def head_mean(x):
        return jnp.concatenate(
            [_group_mean(x[:, s * HGROUP:(s + 1) * HGROUP], bs) for s in range(D_WIDTH // HGROUP)], axis=1)

    cen = o - head_mean(o)
    var = head_mean(cen * cen)
    y = cen * lax.rsqrt(var + LNX_EPS) * lng_ref[...] + lnb_ref[...] + bonus_ref[...]
    o_ref[...] = (y * g_ref[...]).astype(o_ref.dtype)


def _scan_tables(segments, rows):
    fwd, bwd, first = [], [], []
    for start, bsz, lp in segments:
        nc = lp // rows
        for b in range(bsz):
            base = (start + b * lp) // rows
            for ci in range(nc):
                fwd.append(base + ci)
                bwd.append(base + nc - 1 - ci)
                first.append(1 if ci == 0 else 0)
    as_i32 = lambda t: jnp.asarray(np.asarray(t, np.int32))
    return as_i32(fwd), as_i32(bwd), as_i32(first)


def _scan(segments, n_rows, r, v, a, lwf, lwb, kdf, kdb, bf_, bb_):
    nk = SCAN_CHUNKS_PER_STEP
    rows = nk * SCAN_CHUNK
    if any(start % rows or lp % rows for start, _, lp in segments):
        nk, rows = 1, SCAN_CHUNK
    fwd, bwd, first = _scan_tables(segments, rows)
    nsteps = fwd.shape[0]
    blk = (rows, D_WIDTH)
    fspec = pl.BlockSpec(blk, lambda s, fw, bw, fi: (fw[s], 0))
    bspec = pl.BlockSpec(blk, lambda s, fw, bw, fi: (bw[s], 0))
    grid_spec = pltpu.PrefetchScalarGridSpec(
        num_scalar_prefetch=3, grid=(nsteps,),
        in_specs=[fspec] * 6 + [bspec] * 6,
        out_specs=[fspec, bspec],
        scratch_shapes=[pltpu.VMEM((2, D_WIDTH // HGROUP, HGROUP, HGROUP), F32)],
    )
    return pl.pallas_call(
        functools.partial(_scan_kernel, nk=nk), grid_spec=grid_spec,
        out_shape=[jax.ShapeDtypeStruct((n_rows, D_WIDTH), F32)] * 2,
        compiler_params=_cparams(("arbitrary",)), name="wkv_scan",
    )(fwd, bwd, first, r, v, a, lwf, kdf, bf_, r, v, a, lwb, kdb, bb_)


def _rope_tables(segments, n_rows, layer_kind):
    outs = []
    for start, bsz, lp in segments:
        p = jnp.arange(lp)
        d = jnp.arange(HEAD_DIM)
        if layer_kind == "axial":
            t = p - FRONT
            row = jnp.where(t >= 0, t // GRID_W, jnp.where(p >= PAD, t, 0)).astype(F32)
            col = jnp.where(t >= 0, t % GRID_W, jnp.where(p >= PAD, t, 0)).astype(F32)
            half = HEAD_DIM // 2
            inv = A_THETA ** (-jnp.arange(0, half, 2, dtype=F32) / half)
            ang = jnp.concatenate([row[:, None] * inv] * 2 + [col[:, None] * inv] * 2, axis=1)
            first = (d % 32) < 16
            cos = jnp.cos(ang)
            sa = jnp.where(first[None, :], -jnp.sin(ang), 0.0)
            sb = jnp.where(first[None, :], 0.0, jnp.sin(ang))
        else:
            pos = jnp.maximum(p - PAD, 0).astype(F32)
            inv = ROPE_THETA ** (-jnp.arange(0, ROPE_DIMS, 2, dtype=F32) / ROPE_DIMS)
            ang8 = pos[:, None] * inv
            ang = jnp.concatenate([ang8, ang8] + [jnp.zeros_like(ang8)] * 6, axis=1)
            cos = jnp.where((d < ROPE_DIMS)[None, :], jnp.cos(ang), 1.0)
            sa = jnp.where((d < 8)[None, :], -jnp.sin(ang), 0.0)
            sb = jnp.where(((d >= 8) & (d < 16))[None, :], jnp.sin(ang), 0.0)
        small = lax.optimization_barrier([jnp.concatenate([t_, t_], axis=1) for t_ in (cos, sa, sb)])
        tabs = [jnp.tile(t_, (bsz, 1)) for t_ in small]
        outs.append(tabs)
    tail = n_rows - sum(b * lp for _, b, lp in segments)
    res = []
    for idx in range(3):
        parts = [o[idx] for o in outs] + [jnp.zeros((tail, LANES), F32)]
        res.append(jnp.concatenate(parts, axis=0))
    return res


def _seq_view(flat, seg, width):
    start, bsz, lp = seg
    return flat[start:start + bsz * lp].reshape(bsz, lp, width)


def _attn_operands(qT, k, vT, seg):
    start, bsz, lp = seg
    chunk = vT.shape[2]
    k_seg = k[start:start + bsz * lp].reshape(bsz, lp, k.shape[1])
    v_seg = vT[start // chunk:(start + bsz * lp) // chunk].reshape(bsz, lp // chunk, vT.shape[1], chunk)
    return qT, k_seg, v_seg, start


def _to_flat(parts, n_rows, width, dtype):
    used = sum(p.shape[0] for p in parts)
    return jnp.concatenate(parts + [jnp.zeros((n_rows - used, width), dtype)], axis=0)


def _to_flat_cols(parts, n_rows):
    used = sum(p.shape[1] for p in parts)
    return jnp.concatenate(parts + [jnp.zeros((parts[0].shape[0], n_rows - used), parts[0].dtype)], axis=1)


def _block_avg(n, group):
    idx = np.arange(n) // group
    return jnp.asarray((idx[:, None] == idx[None, :]).astype(np.float32) / group).astype(BF16)


def _forward(xs, p):
    segments = []
    start = 0
    for x in xs:
        bsz, s, _ = x.shape
        lp = s + FRONT
        segments.append((start, bsz, lp))
        start += bsz * lp
    n_used = start
    tm = 512 if n_used >= 4096 else ATTN_CHUNK
    n_rows = _round_up(n_used, tm)

    meta = p['meta_tokens'].astype(F32)
    parts = []
    for x in xs:
        bsz = x.shape[0]
        lead = jnp.concatenate([jnp.zeros((PAD, D_MODEL), F32), meta], axis=0)
        parts.append(jnp.concatenate([jnp.broadcast_to(lead[None], (bsz, FRONT, D_MODEL)), x], axis=1)
                     .reshape(-1, D_MODEL))
    h = _to_flat(parts, n_rows, D_MODEL, F32)

    bs = _block_avg(HGROUP, HEAD_DIM)
    row = lambda t: t.reshape(1, -1).astype(F32)

    depth = p['pre_mix_g'].shape[0]
    for i in range(depth):
        if i % 2 == 0:
            e = i // 2
            cos, sa, sb = _rope_tables(segments, n_rows, "axial")
            ch = np.arange(B_DIM)
            ang = 2.0 * np.pi * ((ch[:, None] * ch[None, :]) % B_DIM) / B_DIM
            cc, sc = jnp.asarray(np.cos(ang), F32), jnp.asarray(np.sin(ang), F32)
            wl = p['b_w'][e].astype(F32)
            hp = lax.Precision.HIGHEST
            pmat = jnp.einsum('cd,gde->gce', cc, wl, precision=hp)
            qmat = -jnp.einsum('cd,gde->gce', sc, wl, precision=hp)
            bd = lambda m: jax.scipy.linalg.block_diag(*[m[g] for g in range(B_GROUPS)])
            pq_base = jnp.concatenate([bd(pmat), bd(qmat)], axis=1)
            qT, k, vT, gf = _row_call(
                "in_even", _in_even_kernel, n_rows, tm, [h, cos, sa, sb],
                [row(p['pre_mix_g'][i]), p['even_w_in'][e].astype(BF16),
                 row(jnp.tile(p['a_q_gain'][e], A_HEADS) * (HEAD_DIM ** -0.5 * LOG2_E)),
                 row(jnp.tile(p['a_k_gain'][e], A_KV_HEADS)), row(p['b_norm_g'][e]), bs,
                 pq_base.astype(BF16)],
                [("cols", A_Q, BF16), ("rows", A_KV, BF16), ("chunks", A_KV, BF16), ("rows", 2 * B_W, F32)])
            ya_parts, yb_parts = [], []
            for seg in segments:
                _, bsz, lp = seg
                ya_parts.append(_attn_a(*_attn_operands(qT, k, vT, seg)))
                gseq = _seq_view(gf, seg, 2 * B_W)[:, PAD:]
                yb = _mixer_b(gseq, row(p['b_b'][e]))
                yb_parts.append(jnp.pad(yb, ((0, 0), (PAD, 0), (0, 0))).reshape(bsz * lp, B_W).astype(BF16))
            yat = _to_flat_cols(ya_parts, n_rows)
            yb = _to_flat(yb_parts, n_rows, B_W, BF16)
            w_out = p['even_w_out'][e].astype(BF16)
            wa, wb = w_out[:A_Q], w_out[A_Q:]
        else:
            o = i // 2
            cos, sa, sb = _rope_tables(segments, n_rows, "partial")
            qT, k, vT, u = _row_call(
                "in_odd", _in_odd_kernel, n_rows, tm, [h, cos, sa, sb],
                [row(p['pre_mix_g'][i]), p['odd_w_in'][o].astype(BF16)],
                [("cols", C_Q, BF16), ("rows", C_KV, BF16), ("chunks", C_KV, BF16), ("rows", D_IN, F32)])
            sink = p['c_sink'][o].astype(F32)
            yat = _to_flat_cols([_attn_c(sink, *_attn_operands(qT, k, vT, seg)) for seg in segments], n_rows)
            yb = _mixer_d(u, p, o, bs, segments, n_rows, tm)
            w_out = p['odd_w_out'][o].astype(BF16)
            wa, wb = w_out[:C_Q], w_out[C_Q:]
        (h,) = _row_call(
            "mix_ffn", functools.partial(_mix_ffn_kernel, chunk=256), n_rows, tm, [h, yb],
            [wa, wb, row(p['post_mix_g'][i]),
             row(p['pre_ffn_g'][i]), p['ffn_w_gate'][i].astype(BF16), p['ffn_w_up'][i].astype(BF16),
             p['ffn_w_down'][i].astype(BF16), row(p['post_ffn_g'][i])],
            [("rows", D_MODEL, F32)], scratch=[pltpu.VMEM((tm, D_MODEL), F32)], col_ins=[yat])

    outs = []
    for seg in segments:
        outs.append(_seq_view(h, seg, D_MODEL)[:, FRONT:])
    return tuple(outs)


def _mixer_d(u, p, o, bs, segments, n_rows, tm):
    row = lambda t: t.reshape(1, -1).astype(F32)
    w = D_WIDTH
    zeros = lambda r: jnp.zeros((r, w), F32)
    w2 = jnp.concatenate([jnp.concatenate([p['d_w_up'][o][0], zeros(DECAY_RANK)], axis=1),
                          jnp.concatenate([zeros(DECAY_RANK), p['d_w_up'][o][1]], axis=1)], axis=0)
    a2 = jnp.concatenate([jnp.concatenate([p['d_a_up'][o][0], zeros(ICLR_RANK)], axis=1),
                          jnp.concatenate([zeros(ICLR_RANK), p['d_a_up'][o][1]], axis=1)], axis=0)
    consts = [row(p['d_mu_prev'][o]), row(p['d_mu_next'][o]), w2.astype(BF16), a2.astype(BF16),
              p['d_g_up'][o].astype(BF16), row(p['d_w0'][o]), row(p['d_a0'][o]),
              row(p['d_k_k'][o]), row(p['d_k_a'][o]), row(p['d_r_k'][o]), bs]
    nb8 = n_rows // 8
    t8 = tm // 8
    in_specs = [pl.BlockSpec((tm, D_IN), lambda i: (i, 0)),
                pl.BlockSpec((8, D_IN), lambda i: (jnp.maximum(i * t8 - 1, 0), 0)),
                pl.BlockSpec((8, D_IN), lambda i: (jnp.minimum((i + 1) * t8, nb8 - 1), 0))]
    in_specs += [pl.BlockSpec(a.shape, lambda i, nd=a.ndim: (0,) * nd) for a in consts]
    n_out = 11
    prep = pl.pallas_call(
        functools.partial(_d_prep_kernel, tm=tm, segments=tuple(segments)),
        grid=(n_rows // tm,), in_specs=in_specs,
        out_specs=[pl.BlockSpec((tm, w), lambda i: (i, 0))] * n_out,
        out_shape=[jax.ShapeDtypeStruct((n_rows, w), F32)] * n_out,
        compiler_params=_cparams(("parallel",)), name="d_prep",
    )(u, u, u, *consts)
    r, v, a, lwf, lwb, kdf, kdb, bf_, bb_, bonus, g = prep
    of, ob = _scan(segments, n_rows, r, v, a, lwf, lwb, kdf, kdb, bf_, bb_)
    (yd,) = _row_call("d_post", _d_post_kernel, n_rows, tm, [of, ob, bonus, g],
                      [row(p['d_ln_g'][o]), row(p['d_ln_b'][o]), bs], [("rows", w, BF16)])
    return yd


def kernel(x_prompt, x_sample, meta_tokens, pre_mix_g, post_mix_g, pre_ffn_g, post_ffn_g, even_w_in, even_w_out, a_q_gain, a_k_gain, b_norm_g, b_w, b_b, odd_w_in, odd_w_out, c_sink, d_mu_prev, d_mu_next, d_w0, d_w_up, d_a0, d_a_up, d_g_up, d_k_k, d_k_a, d_r_k, d_ln_g, d_ln_b, ffn_w_gate, ffn_w_up, ffn_w_down):
    params = dict(meta_tokens=meta_tokens, pre_mix_g=pre_mix_g, post_mix_g=post_mix_g,
                  pre_ffn_g=pre_ffn_g, post_ffn_g=post_ffn_g,
                  even_w_in=even_w_in, even_w_out=even_w_out, a_q_gain=a_q_gain, a_k_gain=a_k_gain,
                  b_norm_g=b_norm_g, b_w=b_w, b_b=b_b,
                  odd_w_in=odd_w_in, odd_w_out=odd_w_out, c_sink=c_sink,
                  d_mu_prev=d_mu_prev, d_mu_next=d_mu_next, d_w0=d_w0, d_w_up=d_w_up,
                  d_a0=d_a0, d_a_up=d_a_up, d_g_up=d_g_up, d_k_k=d_k_k, d_k_a=d_k_a, d_r_k=d_r_k,
                  d_ln_g=d_ln_g, d_ln_b=d_ln_b,
                  ffn_w_gate=ffn_w_gate, ffn_w_up=ffn_w_up, ffn_w_down=ffn_w_down)
    return _forward([x_prompt, x_sample], params)
```

```python
import functools
import math

import numpy as np
import jax
import jax.numpy as jnp
from jax import lax
from jax.experimental import pallas as pl
from jax.experimental.pallas import tpu as pltpu

F32 = jnp.float32
BF16 = jnp.bfloat16

D_MODEL = 1024
HEAD_DIM = 64
N_META = 16
GRID_W = 64
WINDOW = 128
RMS_EPS = 1e-6
A_HEADS, A_KV_HEADS, A_THETA = 12, 4, 10000.0
B_GROUPS, B_DIM = 4, 64
C_HEADS, C_KV_HEADS = 8, 2
ROPE_THETA = 500000.0
ROPE_DIMS = HEAD_DIM // 4
D_HEADS = 8
D_WIDTH = D_HEADS * HEAD_DIM
DECAY_RANK, ICLR_RANK, GATE_RANK = 64, 64, 128
LNX_EPS = 64e-5
D_FF = 2816
A_Q, A_KV, B_W = A_HEADS * HEAD_DIM, A_KV_HEADS * HEAD_DIM, B_GROUPS * B_DIM
EVEN_IN = A_Q + 2 * A_KV + B_W
C_Q, C_KV = C_HEADS * HEAD_DIM, C_KV_HEADS * HEAD_DIM
D_IN = 3 * D_WIDTH + 2 * DECAY_RANK + 2 * ICLR_RANK + GATE_RANK
ODD_IN = C_Q + 2 * C_KV + D_IN

LANES = 128
FRONT = 256
PAD = FRONT - N_META
NEG = -1e30
SCAN_CHUNK = 64
SCAN_CHUNKS_PER_STEP = 4
HGROUP = 256
FFT_L1 = 16
VMEM_LIMIT = 56 * 1024 * 1024
ROW_SUBTILE = 256
ATTN_CHUNK = 256
ATTN_MAX_CHUNKS = 20
ATTN_UNROLL = 8
LOG2_E = math.log2(math.e)


def _round_up(x, m):
    return (x + m - 1) // m * m


def _cparams(sem):
    return pltpu.CompilerParams(dimension_semantics=sem, vmem_limit_bytes=VMEM_LIMIT)


def _dot(a, b):
    return jnp.dot(a, b, preferred_element_type=F32)


def _dot_nt(a, b):
    return lax.dot_general(a, b, (((1,), (1,)), ((), ())), preferred_element_type=F32)


def _dot_tn(a, b):
    return lax.dot_general(a, b, (((0,), (0,)), ((), ())), preferred_element_type=F32)


def _split2(x):
    hi = x.astype(BF16)
    lo = (x - hi.astype(F32)).astype(BF16)
    return hi, lo


def _group_mean(x, bs):
    hi, lo = _split2(x)
    return _dot(hi, bs) + _dot(lo, bs)


def _rope(x, cos, sin_a, sin_b, shift):
    n = x.shape[1]
    return x * cos + pltpu.roll(x, n - shift, 1) * sin_a + pltpu.roll(x, shift, 1) * sin_b


def _wide(t, n):
    return t if n == LANES else jnp.concatenate([t] * (n // LANES), axis=1)


def _rms_rows(x, g):
    ms = jnp.mean(x * x, axis=-1, keepdims=True)
    return x * lax.rsqrt(ms + RMS_EPS) * g


def _in_even_kernel(h_ref, cos_ref, sa_ref, sb_ref, g_ref, w_ref, qg_ref, kg_ref, bg_ref, bs_ref, pq_ref,
                    q_ref, k_ref, v_ref, gf_ref):
    bs = bs_ref[...]
    for r in range(h_ref.shape[0] // ROW_SUBTILE):
        rows = slice(r * ROW_SUBTILE, (r + 1) * ROW_SUBTILE)
        hn = _rms_rows(h_ref[rows, :], g_ref[...]).astype(BF16)
        proj = _dot(hn, w_ref[...])
        cos, sa, sb = (_wide(t[rows, :], HGROUP) for t in (cos_ref, sa_ref, sb_ref))

        def norm_rope(x, gain):
            xn = x * lax.rsqrt(_group_mean(x * x, bs) + RMS_EPS) * gain
            return _rope(xn, cos, sa, sb, 16)

        for s in range(A_Q // HGROUP):
            cs = slice(s * HGROUP, (s + 1) * HGROUP)
            q_ref[cs, rows] = norm_rope(proj[:, cs], qg_ref[:, cs]).T.astype(BF16)
        k_ref[rows, :] = norm_rope(proj[:, A_Q:A_Q + A_KV], kg_ref[...]).astype(BF16)
        v_ref[r] = proj[:, A_Q + A_KV:A_Q + 2 * A_KV].T.astype(BF16)
        f = proj[:, A_Q + 2 * A_KV:]
        fn = f * lax.rsqrt(_group_mean(f * f, bs) + RMS_EPS) * bg_ref[...]
        gf_ref[rows, :] = _dot(fn.astype(BF16), pq_ref[...])


def _in_odd_kernel(h_ref, cos_ref, sa_ref, sb_ref, g_ref, w_ref, q_ref, k_ref, v_ref, u_ref):
    for r in range(h_ref.shape[0] // ROW_SUBTILE):
        rows = slice(r * ROW_SUBTILE, (r + 1) * ROW_SUBTILE)
        hn = _rms_rows(h_ref[rows, :], g_ref[...]).astype(BF16)
        proj = _dot(hn, w_ref[...])
        cos, sa, sb = cos_ref[rows, :], sa_ref[rows, :], sb_ref[rows, :]
        cos2, sa2, sb2 = (_wide(t, HGROUP) for t in (cos, sa, sb))
        for s in range(C_Q // HGROUP):
            cs = slice(s * HGROUP, (s + 1) * HGROUP)
            q_ref[cs, rows] = (_rope(proj[:, cs], cos2, sa2, sb2, 8)
                               * (HEAD_DIM ** -0.5 * LOG2_E)).T.astype(BF16)
        k_ref[rows, :] = _rope(proj[:, C_Q:C_Q + C_KV], cos, sa, sb, 8).astype(BF16)
        v = proj[:, C_Q + C_KV:C_Q + 2 * C_KV]
        for c in range(ROW_SUBTILE // C_KV):
            v_ref[r * (ROW_SUBTILE // C_KV) + c] = v[c * C_KV:(c + 1) * C_KV, :].T.astype(BF16)
        u_ref[rows, :] = proj[:, C_Q + 2 * C_KV:]


def _mix_ffn_even_kernel(h_ref, yb_ref, *rest, chunk):
    _mix_ffn(h_ref, yb_ref[...], *rest, chunk=chunk)


def _mix_ffn_odd_kernel(h_ref, of_ref, ob_ref, bonus_ref, gate_ref, lng_ref, lnb_ref, bs_ref, *rest, chunk):
    o = of_ref[...] + ob_ref[...]
    bs = bs_ref[...]

    def head_mean(x):
        return jnp.concatenate(
            [_group_mean(x[:, s * HGROUP:(s + 1) * HGROUP], bs) for s in range(D_WIDTH // HGROUP)], axis=1)

    cen = o - head_mean(o)
    var = head_mean(cen * cen)
    y = cen * lax.rsqrt(var + LNX_EPS) * lng_ref[...] + lnb_ref[...] + bonus_ref[...]
    _mix_ffn(h_ref, (y * gate_ref[...]).astype(BF16), *rest, chunk=chunk)


def _mix_ffn(h_ref, yb, wa_ref, wb_ref, gm_ref, g1_ref, wg_ref, wu_ref, wd_ref, g2_ref,
             yat_ref, o_ref, acc_ref, *, chunk):
    mix = _dot_tn(yat_ref[...], wa_ref[...]) + _dot(yb, wb_ref[...])
    h = h_ref[...] + _rms_rows(mix, gm_ref[...])
    hn = _rms_rows(h, g1_ref[...]).astype(BF16)
    for c in range(D_FF // chunk):
        cs = slice(c * chunk, (c + 1) * chunk)
        gate = _dot(hn, wg_ref[:, cs])
        up = _dot(hn, wu_ref[:, cs])
        act = (gate * (1.0 / (1.0 + jnp.exp(-gate))) * up).astype(BF16)
        part = _dot(act, wd_ref[cs, :])
        if c == 0:
            acc_ref[...] = part
        else:
            acc_ref[...] += part
    o_ref[...] = h + _rms_rows(acc_ref[...], g2_ref[...])


def _row_call(name, kernel, n_rows, tm, row_ins, const_ins, outs, scratch=(), col_ins=()):
    grid = (n_rows // tm,)
    in_specs = [pl.BlockSpec((tm, a.shape[1]), lambda i: (i, 0)) for a in row_ins]
    in_specs += [pl.BlockSpec(a.shape, lambda i, nd=a.ndim: (0,) * nd, pipeline_mode=pl.Buffered(1))
                 for a in const_ins]
    in_specs += [pl.BlockSpec((a.shape[0], tm), lambda i: (0, i)) for a in col_ins]
    out_specs, out_shape = [], []
    for kind, n, dt in outs:
        if kind == "rows":
            out_specs.append(pl.BlockSpec((tm, n), lambda i: (i, 0)))
            out_shape.append(jax.ShapeDtypeStruct((n_rows, n), dt))
        elif kind == "cols":
            out_specs.append(pl.BlockSpec((n, tm), lambda i: (0, i)))
            out_shape.append(jax.ShapeDtypeStruct((n, n_rows), dt))
        else:
            out_specs.append(pl.BlockSpec((tm // n, n, n), lambda i: (i, 0, 0)))
            out_shape.append(jax.ShapeDtypeStruct((n_rows // n, n, n), dt))
    return pl.pallas_call(
        kernel, grid=grid, in_specs=in_specs, out_specs=out_specs, out_shape=out_shape,
        scratch_shapes=list(scratch), compiler_params=_cparams(("parallel",)), name=name,
    )(*row_ins, *const_ins, *col_ins)


def _padded_q(q_ref, h, kv):
    qh = q_ref[h * HEAD_DIM:(h + 1) * HEAD_DIM, :]
    zero = jnp.zeros_like(qh)
    return jnp.concatenate([qh, zero] if kv % 2 == 0 else [zero, qh], axis=0)


def _attn_a_kernel(qT_ref, k_ref, vT_ref, o_ref, m_scr, l_scr, acc_scr, s_scr, *, nchunk, nkb, tq):
    j = pl.program_id(2)

    @pl.when(j == 0)
    def _():
        m_scr[...] = jnp.full(m_scr.shape, NEG, F32)
        l_scr[...] = jnp.zeros(l_scr.shape, F32)
        acc_scr[...] = jnp.zeros(acc_scr.shape, F32)

    group = A_HEADS // A_KV_HEADS
    ck = ATTN_CHUNK
    key_row = lax.broadcasted_iota(jnp.int32, (ck, tq), 0) + j * (nchunk * ck)

    def fold(s, op):
        return op(s.reshape(ck // 8, 8, tq), axis=0)

    def gang_q(g):
        return [_padded_q(qT_ref, g * group + t, g) for t in range(group)]

    def gang_scores(g, qps, c, mxs, first=False):
        kc = k_ref[0, pl.ds(pl.multiple_of(c * ck, ck), ck), (g // 2) * LANES:(g // 2 + 1) * LANES]
        out = []
        for t in range(group):
            s = _dot(kc, qps[t])
            if first:
                s = jnp.where(key_row >= PAD, s, NEG)
            s_scr[g % 2, t, c] = s
            out.append(fold(s, jnp.max) if mxs is None else jnp.maximum(mxs[t], fold(s, jnp.max)))
        return tuple(out)

    def gang_values(g, c, m_news):
        vc = vT_ref[0, c, g * HEAD_DIM:(g + 1) * HEAD_DIM, :]
        ps = [jnp.exp2(s_scr[g % 2, t, c] - m_news[t]) for t in range(group)]
        return tuple(_dot(vc, p.astype(BF16)) for p in ps), tuple(fold(p, jnp.sum) for p in ps)

    def add(xs, ys):
        return tuple(x + y for x, y in zip(xs, ys))

    unroll = max(u for u in range(1, ATTN_UNROLL + 1) if max(nchunk - 1, 1) % u == 0)
    qps = gang_q(0)
    mxs = lax.fori_loop(1, nchunk, lambda c, mxs, qps=qps: gang_scores(0, qps, c, mxs),
                        gang_scores(0, qps, 0, None, first=True), unroll=unroll)
    for g in range(A_KV_HEADS):
        heads = [g * group + t for t in range(group)]
        m_olds = [m_scr[h] for h in heads]
        m_news = [jnp.maximum(mo, jnp.max(mx, axis=0, keepdims=True)) for mo, mx in zip(m_olds, mxs)]
        if g + 1 < A_KV_HEADS:
            qps = gang_q(g + 1)

            def body(c, carry, g=g, m_news=m_news, qps=qps):
                accs, lsums, mxn = carry
                mxn = gang_scores(g + 1, qps, c, mxn)
                pvs, pss = gang_values(g, c, m_news)
                return add(accs, pvs), add(lsums, pss), mxn

            mx0 = gang_scores(g + 1, qps, 0, None, first=True)
            accs, lsums, mxs = lax.fori_loop(1, nchunk, body, (*gang_values(g, 0, m_news), mx0),
                                             unroll=unroll)
        else:
            def tail(c, carry, g=g, m_news=m_news):
                pvs, pss = gang_values(g, c, m_news)
                return add(carry[0], pvs), add(carry[1], pss)

            accs, lsums = lax.fori_loop(1, nchunk, tail, gang_values(g, 0, m_news), unroll=unroll)
        for t, h in enumerate(heads):
            alpha = jnp.exp2(m_olds[t] - m_news[t])
            acc_scr[h] = acc_scr[h] * alpha + accs[t]
            l_scr[h] = l_scr[h] * alpha + jnp.sum(lsums[t], axis=0, keepdims=True)
            m_scr[h] = m_news[t]

    @pl.when(j == nkb - 1)
    def _():
        for h in range(A_HEADS):
            o_ref[h * HEAD_DIM:(h + 1) * HEAD_DIM, :] = (acc_scr[h] / l_scr[h]).astype(o_ref.dtype)


def _attn_a(qT, k, vT, start):
    bsz, lp = k.shape[:2]
    off = start // ATTN_CHUNK
    tq = ATTN_CHUNK
    total = lp // ATTN_CHUNK
    nkb = min(n for n in range(1, total + 1) if total % n == 0 and total // n <= ATTN_MAX_CHUNKS)
    tk = lp // nkb
    nchunk = tk // ATTN_CHUNK
    nq = lp // tq
    kernel = functools.partial(_attn_a_kernel, nchunk=nchunk, nkb=nkb, tq=tq)
    return pl.pallas_call(
        kernel, grid=(bsz, nq, nkb),
        in_specs=[
            pl.BlockSpec((A_Q, tq), lambda b, i, j: (0, off + b * nq + i)),
            pl.BlockSpec((1, tk, A_KV), lambda b, i, j: (b, j, 0)),
            pl.BlockSpec((1, nchunk, A_KV, ATTN_CHUNK), lambda b, i, j: (b, j, 0, 0)),
        ],
        out_specs=pl.BlockSpec((A_Q, tq), lambda b, i, j: (0, b * nq + i)),
        out_shape=jax.ShapeDtypeStruct((A_Q, bsz * lp), BF16),
        scratch_shapes=[
            pltpu.VMEM((A_HEADS, 1, tq), F32),
            pltpu.VMEM((A_HEADS, 1, tq), F32),
            pltpu.VMEM((A_HEADS, HEAD_DIM, tq), F32),
            pltpu.VMEM((2, A_HEADS // A_KV_HEADS, nchunk, ATTN_CHUNK, tq), F32),
        ],
        compiler_params=_cparams(("parallel", "parallel", "arbitrary")), name="attn_a",
    )(qT, k, vT)


def _attn_c_kernel(sink_ref, qT_ref, k0, k1, k2, k3, v0, v1, v2, v3, o_ref, *, lp):
    j = pl.program_id(1)
    k_refs, v_refs = (k0, k1, k2, k3), (v0, v1, v2, v3)
    row = lax.broadcasted_iota(jnp.int32, (LANES, LANES), 0)
    pq = lax.broadcasted_iota(jnp.int32, (LANES, LANES), 1) + j * LANES
    biases = []
    for slot in range(4):
        if slot == 0:
            bias = jnp.where(row >= PAD % LANES, 0.0, NEG)
        else:
            pk = row + (j + slot - 2) * LANES
            in_window = jnp.where(jnp.abs(pq - pk) <= WINDOW, 0.0, NEG)
            bias = jnp.where(pk >= FRONT, jnp.where(pk < lp, in_window, NEG), NEG)
        biases.append(bias.astype(F32))
    group = C_HEADS // C_KV_HEADS
    for h in range(C_HEADS):
        kv = h // group
        qp = _padded_q(qT_ref, h, kv)
        sink = sink_ref[h] * LOG2_E
        ss = [_dot(k_refs[t][0], qp) + biases[t] for t in range(4)]
        m = jnp.maximum(jnp.maximum(ss[0], ss[1]), jnp.maximum(ss[2], ss[3]))
        m = jnp.maximum(jnp.max(m, axis=0, keepdims=True), sink)
        acc = jnp.zeros((HEAD_DIM, LANES), F32)
        denom = jnp.exp2(sink - m)
        for t in range(4):
            p = jnp.exp2(ss[t] - m)
            acc = acc + _dot(v_refs[t][0, 0, kv * HEAD_DIM:(kv + 1) * HEAD_DIM, :], p.astype(BF16))
            denom = denom + jnp.sum(p, axis=0, keepdims=True)
        o_ref[h * HEAD_DIM:(h + 1) * HEAD_DIM, :] = (acc / denom).astype(o_ref.dtype)


def _attn_c(sink, qT, k, vT, start):
    bsz, lp = k.shape[:2]
    nb = lp // LANES
    off = start // LANES
    kernel = functools.partial(_attn_c_kernel, lp=lp)

    def kspec(fn):
        return pl.BlockSpec((1, LANES, C_KV), lambda b, j: (b, fn(j), 0))

    def vspec(fn):
        return pl.BlockSpec((1, 1, C_KV, LANES), lambda b, j: (b, fn(j), 0, 0))

    fns = (lambda j: PAD // LANES, lambda j: jnp.maximum(j - 1, 0), lambda j: j,
           lambda j: jnp.minimum(j + 1, nb - 1))
    return pl.pallas_call(
        kernel, grid=(bsz, nb),
        in_specs=[pl.BlockSpec(memory_space=pltpu.SMEM),
                  pl.BlockSpec((C_Q, LANES), lambda b, j: (0, off + b * nb + j))]
                 + [kspec(f) for f in fns] + [vspec(f) for f in fns],
        out_specs=pl.BlockSpec((C_Q, LANES), lambda b, j: (0, b * nb + j)),
        out_shape=jax.ShapeDtypeStruct((C_Q, bsz * lp), BF16),
        compiler_params=_cparams(("parallel", "parallel")), name="attn_c",
    )(sink, qT, k, k, k, k, vT, vT, vT, vT)


def _fft1_kernel(x_ref, c1_ref, s1_ref, o_ref):
    x = x_ref[0]
    tn = x.shape[1]
    lane = lax.broadcasted_iota(jnp.int32, x.shape, 1)
    is_re = (lane & (2 * B_W - 1)) < B_W
    xs = jnp.where(is_re, pltpu.roll(x, tn - B_W, 1), -pltpu.roll(x, B_W, 1))
    o_ref[0] = _dot(c1_ref[...], x.astype(BF16)) + _dot(s1_ref[...], xs.astype(BF16))


def _fft2_kernel(a_ref, tc_ref, ts_ref, c2_ref, s2_ref, bias_ref, o_ref):
    a = a_ref[0, 0]
    tc, ts = _wide(tc_ref[0], B_W), _wide(ts_ref[0], B_W)
    are, aim = a[:, :B_W], a[:, B_W:]
    bre = (are * tc + aim * ts).astype(BF16)
    bim = (aim * tc - are * ts).astype(BF16)
    o_ref[0] = _dot(c2_ref[...], bre) + _dot(s2_ref[...], bim) + bias_ref[...]


def _dft_tables(l2, l2p, l2o):
    l1 = FFT_L1
    length = l1 * l2
    n1 = np.arange(l1)
    ang1 = 2.0 * np.pi * ((n1[:, None] * n1[None, :]) % l1) / l1
    c1, s1 = np.cos(ang1), np.sin(ang1)
    n2 = np.arange(l2)
    angt = 2.0 * np.pi * (n1[:, None] * n2[None, :]) / length
    tc = np.zeros((l1, l2p, LANES), np.float32)
    ts = np.zeros((l1, l2p, LANES), np.float32)
    tc[:, :l2, :] = np.cos(angt)[:, :, None]
    ts[:, :l2, :] = np.sin(angt)[:, :, None]
    ang2 = 2.0 * np.pi * ((n2[:, None] * n2[None, :]) % l2) / l2
    scale = 1.0 / math.sqrt(B_DIM * length)
    c2 = np.zeros((l2o, l2p), np.float32)
    s2 = np.zeros((l2o, l2p), np.float32)
    c2[:l2, :l2] = np.cos(ang2) * scale
    s2[:l2, :l2] = np.sin(ang2) * scale
    as_bf = lambda t: jnp.asarray(t, F32).astype(BF16)
    return as_bf(c1), as_bf(s1), jnp.asarray(tc), jnp.asarray(ts), as_bf(c2), as_bf(s2)


def _mixer_b(gseq, bias):
    bsz, length, _ = gseq.shape
    l1 = FFT_L1
    l2 = length // l1
    l2p, l2o = _round_up(l2, LANES), _round_up(l2, 8)
    c1, s1, tc, ts, c2, s2 = _dft_tables(l2, l2p, l2o)
    x = jnp.pad(gseq.reshape(bsz, l1, l2, 2 * B_W), ((0, 0), (0, 0), (0, l2p - l2), (0, 0)))
    x = x.reshape(bsz, l1, l2p * 2 * B_W)
    ncol = l2p * 2 * B_W
    tn = 2 * B_W * 48 if l2p % 48 == 0 else 2 * B_W
    a = pl.pallas_call(
        _fft1_kernel, grid=(bsz, ncol // tn),
        in_specs=[pl.BlockSpec((1, l1, tn), lambda b, i: (b, 0, i)),
                  pl.BlockSpec((l1, l1), lambda b, i: (0, 0)),
                  pl.BlockSpec((l1, l1), lambda b, i: (0, 0))],
        out_specs=pl.BlockSpec((1, l1, tn), lambda b, i: (b, 0, i)),
        out_shape=jax.ShapeDtypeStruct((bsz, l1, ncol), F32),
        compiler_params=_cparams(("parallel", "parallel")), name="fft1",
    )(x, c1, s1)
    a = a.reshape(bsz, l1, l2p, 2 * B_W)
    y = pl.pallas_call(
        _fft2_kernel, grid=(bsz, l1),
        in_specs=[pl.BlockSpec((1, 1, l2p, 2 * B_W), lambda b, k: (b, k, 0, 0)),
                  pl.BlockSpec((1, l2p, LANES), lambda b, k: (k, 0, 0)),
                  pl.BlockSpec((1, l2p, LANES), lambda b, k: (k, 0, 0)),
                  pl.BlockSpec((l2o, l2p), lambda b, k: (0, 0)),
                  pl.BlockSpec((l2o, l2p), lambda b, k: (0, 0)),
                  pl.BlockSpec((1, B_W), lambda b, k: (0, 0))],
        out_specs=pl.BlockSpec((1, l2o, B_W), lambda b, k: (b, 0, k)),
        out_shape=jax.ShapeDtypeStruct((bsz, l2o, l1 * B_W), F32),
        compiler_params=_cparams(("parallel", "parallel")), name="fft2",
    )(a, tc, ts, c2, s2, bias)
    return y[:, :l2].reshape(bsz, length, B_W)


def _seq_position(rows, segments):
    pos = jnp.full(rows.shape, -1.0, F32)
    seqlen = jnp.full(rows.shape, 1.0, F32)
    for start, bsz, lp in segments:
        rel = rows - float(start)
        q = jnp.floor((rel + 0.5) * (1.0 / lp))
        inside = jnp.where(rel >= 0.0, jnp.where(rel < float(bsz * lp), 1.0, 0.0), 0.0) > 0.5
        pos = jnp.where(inside, rel - q * lp, pos)
        seqlen = jnp.where(inside, float(lp), seqlen)
    return pos, seqlen


def _sigmoid(x):
    return 1.0 / (1.0 + jnp.exp(-x))


def _d_prep_kernel(u_ref, up_ref, un_ref, mup_ref, mun_ref, w2_ref, a2_ref, gup_ref, w0_ref, a0_ref,
                   kk_ref, ka_ref, rk_ref, bs_ref,
                   r_ref, v_ref, a_ref, lwf_ref, lwb_ref, kdf_ref, kdb_ref, bf_ref, bb_ref,
                   bonus_ref, g_ref, *, tm, segments):
    i = pl.program_id(0)
    u = u_ref[...]
    rows = (lax.broadcasted_iota(jnp.int32, (tm, 1), 0) + i * tm).astype(F32)
    pos, seqlen = _seq_position(rows, segments)
    local = lax.broadcasted_iota(jnp.int32, (tm, 1), 0)
    u_prev = jnp.where(local == 0, up_ref[7:8, :], pltpu.roll(u, 1, 0))
    u_next = jnp.where(local == tm - 1, un_ref[0:1, :], pltpu.roll(u, tm - 1, 0))
    u_prev = jnp.where(pos == float(PAD), 0.0, u_prev)
    u_next = jnp.where(pos == seqlen - 1.0, 0.0, u_next)
    u = u + mup_ref[...] * (u_prev - u) + mun_ref[...] * (u_next - u)
    valid = jnp.where(pos >= float(PAD), 1.0, 0.0)

    w = D_WIDTH
    r, k, v = u[:, :w], u[:, w:2 * w], u[:, 2 * w:3 * w]
    c0 = 3 * w
    dec = _dot(jnp.tanh(u[:, c0:c0 + 2 * DECAY_RANK]).astype(BF16), w2_ref[...]) + w0_ref[...]
    c0 += 2 * DECAY_RANK
    icl = _dot(u[:, c0:c0 + 2 * ICLR_RANK].astype(BF16), a2_ref[...]) + a0_ref[...]
    c0 += 2 * ICLR_RANK
    g_ref[...] = _dot(_sigmoid(u[:, c0:c0 + GATE_RANK]).astype(BF16), gup_ref[...]).astype(g_ref.dtype)

    bs = bs_ref[...]

    def head_sum(x):
        return jnp.concatenate(
            [_group_mean(x[:, s * HGROUP:(s + 1) * HGROUP], bs) for s in range(w // HGROUP)], axis=1
        ) * float(HEAD_DIM)

    kk = k * kk_ref[...]
    kk = kk * lax.rsqrt(jnp.maximum(head_sum(kk * kk), 1e-24))
    r_ref[...] = r.astype(r_ref.dtype)
    v_ref[...] = v.astype(v_ref.dtype)
    a_ref[...] = (-kk * valid).astype(a_ref.dtype)
    bonus = jnp.zeros_like(r)
    for d, (lw_ref, kd_ref, b_ref) in enumerate(((lwf_ref, kdf_ref, bf_ref), (lwb_ref, kdb_ref, bb_ref))):
        x = -dec[:, d * w:(d + 1) * w]
        softplus = jnp.maximum(x, 0.0) + jnp.log(1.0 + jnp.exp(-jnp.abs(x)))
        lw_ref[...] = -jnp.exp(-softplus - 0.5)
        gate = _sigmoid(icl[:, d * w:(d + 1) * w])
        kd = k * (1.0 + (gate - 1.0) * ka_ref[...])
        kd_ref[...] = (kd * valid).astype(kd_ref.dtype)
        b_ref[...] = (kk * gate * valid).astype(b_ref.dtype)
        bonus = bonus + head_sum(r * kd * rk_ref[...]) * v
    bonus_ref[...] = bonus


def _scan_kernel(fwd_blk, bwd_blk, first,
                 rf, vf, af, lwf, kdf, bf_, rb, vb, ab, lwb, kdb, bb_,
                 of_ref, ob_ref, s_scr, *, nk):
    del fwd_blk, bwd_blk
    step = pl.program_id(0)
    c = SCAN_CHUNK

    @pl.when(first[step] == 1)
    def _():
        s_scr[...] = jnp.zeros(s_scr.shape, F32)

    t_sq = lax.broadcasted_iota(jnp.int32, (nk * c, nk * c), 0)
    s_sq = lax.broadcasted_iota(jnp.int32, (nk * c, nk * c), 1)
    t_cat = lax.broadcasted_iota(jnp.int32, (c, HGROUP), 0)
    s_cat = lax.broadcasted_iota(jnp.int32, (c, HGROUP), 1) & (c - 1)
    bd_row = lax.broadcasted_iota(jnp.int32, (HGROUP, HGROUP), 0) >> 6
    bd_col = lax.broadcasted_iota(jnp.int32, (HGROUP, HGROUP), 1) >> 6
    bd_mask = bd_row == bd_col
    eye_cat = jnp.where(t_cat == s_cat, 1.0, 0.0).astype(F32)

    def block_diag(x):
        return jnp.where(bd_mask, jnp.concatenate([x] * (HGROUP // c), axis=0), 0.0).astype(BF16)

    chains = []
    dirs = ((0, rf, vf, af, lwf, kdf, bf_), (1, rb, vb, ab, lwb, kdb, bb_))
    for d, r_ref, v_ref, a_ref, lw_ref, kd_ref, b_ref in dirs:
        rev = d == 1
        earlier = (s_sq >= t_sq) if rev else (s_sq <= t_sq)
        tri = jnp.where((t_sq >> 6) == (s_sq >> 6), jnp.where(earlier, 1.0, 0.0), 0.0).astype(BF16)
        lw = lw_ref[...]
        h1 = lw.astype(BF16)
        r1 = lw - h1.astype(F32)
        h2 = r1.astype(BF16)
        h3 = (r1 - h2.astype(F32)).astype(BF16)
        cum = _dot(tri, h1) + _dot(tri, h2) + _dot(tri, h3)
        e_in = jnp.exp(cum)
        e_neg = jnp.exp(-cum)
        a_t = a_ref[...] * jnp.exp(cum - lw)
        r_t = r_ref[...] * e_in
        b_t = b_ref[...] * e_neg
        k_t = kd_ref[...] * e_neg
        v_all = v_ref[...]
        for ck in range(nk):
            rows = slice(ck * c, (ck + 1) * c)
            w_row = ck * c if rev else (ck + 1) * c - 1
            for g in range(D_WIDTH // HGROUP):
                cs = slice(g * HGROUP, (g + 1) * HGROUP)
                chains.append(dict(d=d, ck=ck, g=g, rev=rev, at=a_t[rows, cs], rt=r_t[rows, cs],
                                   bt=b_t[rows, cs], kt=k_t[rows, cs], vc=v_all[rows, cs],
                                   w=e_in[w_row:w_row + 1, cs]))

    def strict(ch):
        return (s_cat > t_cat) if ch['rev'] else (s_cat < t_cat)

    def incl(ch):
        return (s_cat >= t_cat) if ch['rev'] else (s_cat <= t_cat)

    for ch in chains:
        ch['lhs'] = jnp.concatenate([ch['at'], ch['rt']], axis=0).astype(BF16)
        ch['bd_v'] = block_diag(ch['vc'])
    m_b = [_dot_nt(ch['lhs'], block_diag(ch['bt'])) for ch in chains]
    m_k = [_dot_nt(ch['lhs'], block_diag(ch['kt'])) for ch in chains]
    for ch, mb, mk in zip(chains, m_b, m_k):
        ch['a_ab'] = jnp.where(strict(ch), mb[:c], 0.0)
        ch['a_rb'] = jnp.where(incl(ch), mb[c:], 0.0).astype(BF16)
        ch['a_ak'] = jnp.where(strict(ch), mk[:c], 0.0).astype(BF16)
        ch['a_rk'] = jnp.where(incl(ch), mk[c:], 0.0).astype(BF16)
        ch['t'] = eye_cat + jnp.where((t_cat >> 1) == (s_cat >> 1), ch['a_ab'], 0.0)
    m = 2
    while m < c:
        sh = int(math.log2(m))
        xs = []
        for ch in chains:
            a_m = jnp.where((t_cat >> (sh + 1)) == (s_cat >> (sh + 1)),
                            jnp.where((t_cat >> sh) != (s_cat >> sh), ch['a_ab'], 0.0), 0.0)
            xs.append(_dot(a_m.astype(BF16), block_diag(ch['t'])))
        for ch, x in zip(chains, xs):
            ch['t'] = ch['t'] + _dot(ch['t'].astype(BF16), block_diag(x))
        m *= 2
    akv = [_dot(ch['a_ak'], ch['bd_v']) for ch in chains]
    for ch in chains:
        ch['t'] = ch['t'].astype(BF16)
    a_hat = [_dot(ch['t'], block_diag(ch['at'])) for ch in chains]
    u_hat = [_dot(ch['t'], block_diag(x)) for ch, x in zip(chains, akv)]
    r_hat = [(ch['rt'] + _dot(ch['a_rb'], block_diag(x))).astype(BF16) for ch, x in zip(chains, a_hat)]
    o_hat = [_dot(ch['a_rb'], block_diag(x)) + _dot(ch['a_rk'], ch['bd_v']) for ch, x in zip(chains, u_hat)]
    m_c = [jnp.where(bd_mask, _dot_tn(x.astype(BF16), ch['bt'].astype(BF16)), 0.0).astype(BF16)
           for ch, x in zip(chains, a_hat)]
    n_c = [jnp.where(bd_mask, _dot_tn(jnp.concatenate([x, ch['vc']], axis=0).astype(BF16),
                                      jnp.concatenate([ch['bt'], ch['kt']], axis=0).astype(BF16)), 0.0)
           for ch, x in zip(chains, u_hat)]
    pre = {(ch['d'], ch['ck'], ch['g']): (r_hat[i], o_hat[i], m_c[i], n_c[i], ch['w'])
           for i, ch in enumerate(chains)}

    groups = [(d, g) for d in range(2) for g in range(D_WIDTH // HGROUP)]
    states = {dg: s_scr[dg[0], dg[1]] for dg in groups}
    for i in range(nk):
        st_b = {dg: states[dg].astype(BF16) for dg in groups}
        for d, g in groups:
            ck = nk - 1 - i if d == 1 else i
            rh, oh, mc, nc_, w = pre[(d, ck, g)]
            o_ref = ob_ref if d == 1 else of_ref
            o_ref[ck * c:(ck + 1) * c, g * HGROUP:(g + 1) * HGROUP] = _dot_nt(rh, st_b[(d, g)]) + oh
            states[(d, g)] = (states[(d, g)] + _dot(st_b[(d, g)], mc) + nc_) * w
    for d, g in groups:
        s_scr[d, g] = states[(d, g)]


def _scan_tables(segments, rows):
    fwd, bwd, first = [], [], []
    for start, bsz, lp in segments:
        nc = lp // rows
        for b in range(bsz):
            base = (start + b * lp) // rows
            for ci in range(nc):
                fwd.append(base + ci)
                bwd.append(base + nc - 1 - ci)
                first.append(1 if ci == 0 else 0)
    as_i32 = lambda t: jnp.asarray(np.asarray(t, np.int32))
    return as_i32(fwd), as_i32(bwd), as_i32(first)


def _scan(segments, n_rows, r, v, a, lwf, lwb, kdf, kdb, bf_, bb_):
    nk = SCAN_CHUNKS_PER_STEP
    rows = nk * SCAN_CHUNK
    if any(start % rows or lp % rows for start, _, lp in segments):
        nk, rows = 1, SCAN_CHUNK
    fwd, bwd, first = _scan_tables(segments, rows)
    nsteps = fwd.shape[0]
    blk = (rows, D_WIDTH)
    fspec = pl.BlockSpec(blk, lambda s, fw, bw, fi: (fw[s], 0))
    bspec = pl.BlockSpec(blk, lambda s, fw, bw, fi: (bw[s], 0))
    grid_spec = pltpu.PrefetchScalarGridSpec(
        num_scalar_prefetch=3, grid=(nsteps,),
        in_specs=[fspec] * 6 + [bspec] * 6,
        out_specs=[fspec, bspec],
        scratch_shapes=[pltpu.VMEM((2, D_WIDTH // HGROUP, HGROUP, HGROUP), F32)],
    )
    return pl.pallas_call(
        functools.partial(_scan_kernel, nk=nk), grid_spec=grid_spec,
        out_shape=[jax.ShapeDtypeStruct((n_rows, D_WIDTH), F32)] * 2,
        compiler_params=_cparams(("arbitrary",)), name="wkv_scan",
    )(fwd, bwd, first, r, v, a, lwf, kdf, bf_, r, v, a, lwb, kdb, bb_)


def _rope_tables(segments, n_rows, layer_kind):
    outs = []
    for start, bsz, lp in segments:
        p = jnp.arange(lp)
        d = jnp.arange(HEAD_DIM)
        if layer_kind == "axial":
            t = p - FRONT
            row = jnp.where(t >= 0, t // GRID_W, jnp.where(p >= PAD, t, 0)).astype(F32)
            col = jnp.where(t >= 0, t % GRID_W, jnp.where(p >= PAD, t, 0)).astype(F32)
            half = HEAD_DIM // 2
            inv = A_THETA ** (-jnp.arange(0, half, 2, dtype=F32) / half)
            ang = jnp.concatenate([row[:, None] * inv] * 2 + [col[:, None] * inv] * 2, axis=1)
            first = (d % 32) < 16
            cos = jnp.cos(ang)
            sa = jnp.where(first[None, :], -jnp.sin(ang), 0.0)
            sb = jnp.where(first[None, :], 0.0, jnp.sin(ang))
        else:
            pos = jnp.maximum(p - PAD, 0).astype(F32)
            inv = ROPE_THETA ** (-jnp.arange(0, ROPE_DIMS, 2, dtype=F32) / ROPE_DIMS)
            ang8 = pos[:, None] * inv
            ang = jnp.concatenate([ang8, ang8] + [jnp.zeros_like(ang8)] * 6, axis=1)
            cos = jnp.where((d < ROPE_DIMS)[None, :], jnp.cos(ang), 1.0)
            sa = jnp.where((d < 8)[None, :], -jnp.sin(ang), 0.0)
            sb = jnp.where(((d >= 8) & (d < 16))[None, :], jnp.sin(ang), 0.0)
        small = lax.optimization_barrier([jnp.concatenate([t_, t_], axis=1) for t_ in (cos, sa, sb)])
        tabs = [jnp.tile(t_, (bsz, 1)) for t_ in small]
        outs.append(tabs)
    tail = n_rows - sum(b * lp for _, b, lp in segments)
    res = []
    for idx in range(3):
        parts = [o[idx] for o in outs] + [jnp.zeros((tail, LANES), F32)]
        res.append(jnp.concatenate(parts, axis=0))
    return res


def _seq_view(flat, seg, width):
    start, bsz, lp = seg
    return flat[start:start + bsz * lp].reshape(bsz, lp, width)


def _attn_operands(qT, k, vT, seg):
    start, bsz, lp = seg
    chunk = vT.shape[2]
    k_seg = k[start:start + bsz * lp].reshape(bsz, lp, k.shape[1])
    v_seg = vT[start // chunk:(start + bsz * lp) // chunk].reshape(bsz, lp // chunk, vT.shape[1], chunk)
    return qT, k_seg, v_seg, start


def _to_flat(parts, n_rows, width, dtype):
    used = sum(p.shape[0] for p in parts)
    return jnp.concatenate(parts + [jnp.zeros((n_rows - used, width), dtype)], axis=0)


def _to_flat_cols(parts, n_rows):
    used = sum(p.shape[1] for p in parts)
    return jnp.concatenate(parts + [jnp.zeros((parts[0].shape[0], n_rows - used), parts[0].dtype)], axis=1)


def _block_avg(n, group):
    idx = np.arange(n) // group
    return jnp.asarray((idx[:, None] == idx[None, :]).astype(np.float32) / group).astype(BF16)


def _forward(xs, p):
    segments = []
    start = 0
    for x in xs:
        bsz, s, _ = x.shape
        lp = s + FRONT
        segments.append((start, bsz, lp))
        start += bsz * lp
    n_used = start
    tm = 512 if n_used >= 4096 else ATTN_CHUNK
    n_rows = _round_up(n_used, tm)

    meta = p['meta_tokens'].astype(F32)
    parts = []
    for x in xs:
        bsz = x.shape[0]
        lead = jnp.concatenate([jnp.zeros((PAD, D_MODEL), F32), meta], axis=0)
        parts.append(jnp.concatenate([jnp.broadcast_to(lead[None], (bsz, FRONT, D_MODEL)), x], axis=1)
                     .reshape(-1, D_MODEL))
    h = _to_flat(parts, n_rows, D_MODEL, F32)

    bs = _block_avg(HGROUP, HEAD_DIM)
    row = lambda t: t.reshape(1, -1).astype(F32)

    depth = p['pre_mix_g'].shape[0]
    for i in range(depth):
        if i % 2 == 0:
            e = i // 2
            cos, sa, sb = _rope_tables(segments, n_rows, "axial")
            ch = np.arange(B_DIM)
            ang = 2.0 * np.pi * ((ch[:, None] * ch[None, :]) % B_DIM) / B_DIM
            cc, sc = jnp.asarray(np.cos(ang), F32), jnp.asarray(np.sin(ang), F32)
            wl = p['b_w'][e].astype(F32)
            hp = lax.Precision.HIGHEST
            pmat = jnp.einsum('cd,gde->gce', cc, wl, precision=hp)
            qmat = -jnp.einsum('cd,gde->gce', sc, wl, precision=hp)
            bd = lambda m: jax.scipy.linalg.block_diag(*[m[g] for g in range(B_GROUPS)])
            pq_base = jnp.concatenate([bd(pmat), bd(qmat)], axis=1)
            qT, k, vT, gf = _row_call(
                "in_even", _in_even_kernel, n_rows, tm, [h, cos, sa, sb],
                [row(p['pre_mix_g'][i]), p['even_w_in'][e].astype(BF16),
                 row(jnp.tile(p['a_q_gain'][e], A_HEADS) * (HEAD_DIM ** -0.5 * LOG2_E)),
                 row(jnp.tile(p['a_k_gain'][e], A_KV_HEADS)), row(p['b_norm_g'][e]), bs,
                 pq_base.astype(BF16)],
                [("cols", A_Q, BF16), ("rows", A_KV, BF16), ("chunks", A_KV, BF16), ("rows", 2 * B_W, F32)])
            ya_parts, yb_parts = [], []
            for seg in segments:
                _, bsz, lp = seg
                ya_parts.append(_attn_a(*_attn_operands(qT, k, vT, seg)))
                gseq = _seq_view(gf, seg, 2 * B_W)[:, PAD:]
                yb = _mixer_b(gseq, row(p['b_b'][e]))
                yb_parts.append(jnp.pad(yb, ((0, 0), (PAD, 0), (0, 0))).reshape(bsz * lp, B_W).astype(BF16))
            yat = _to_flat_cols(ya_parts, n_rows)
            mix_kernel, mix_rows, mix_consts = _mix_ffn_even_kernel, [_to_flat(yb_parts, n_rows, B_W, BF16)], []
            w_out = p['even_w_out'][e].astype(BF16)
            wa, wb = w_out[:A_Q], w_out[A_Q:]
        else:
            o = i // 2
            cos, sa, sb = _rope_tables(segments, n_rows, "partial")
            qT, k, vT, u = _row_call(
                "in_odd", _in_odd_kernel, n_rows, tm, [h, cos, sa, sb],
                [row(p['pre_mix_g'][i]), p['odd_w_in'][o].astype(BF16)],
                [("cols", C_Q, BF16), ("rows", C_KV, BF16), ("chunks", C_KV, BF16), ("rows", D_IN, F32)])
            sink = p['c_sink'][o].astype(F32)
            yat = _to_flat_cols([_attn_c(sink, *_attn_operands(qT, k, vT, seg)) for seg in segments], n_rows)
            mix_kernel = _mix_ffn_odd_kernel
            mix_rows, mix_consts = _mixer_d(u, p, o, bs, segments, n_rows, tm)
            w_out = p['odd_w_out'][o].astype(BF16)
            wa, wb = w_out[:C_Q], w_out[C_Q:]
        (h,) = _row_call(
            "mix_ffn", functools.partial(mix_kernel, chunk=256), n_rows, tm, [h] + mix_rows,
            mix_consts + [wa, wb, row(p['post_mix_g'][i]),
             row(p['pre_ffn_g'][i]), p['ffn_w_gate'][i].astype(BF16), p['ffn_w_up'][i].astype(BF16),
             p['ffn_w_down'][i].astype(BF16), row(p['post_ffn_g'][i])],
            [("rows", D_MODEL, F32)], scratch=[pltpu.VMEM((tm, D_MODEL), F32)], col_ins=[yat])

    outs = []
    for seg in segments:
        outs.append(_seq_view(h, seg, D_MODEL)[:, FRONT:])
    return tuple(outs)


def _mixer_d(u, p, o, bs, segments, n_rows, tm):
    row = lambda t: t.reshape(1, -1).astype(F32)
    w = D_WIDTH
    zeros = lambda r: jnp.zeros((r, w), F32)
    w2 = jnp.concatenate([jnp.concatenate([p['d_w_up'][o][0], zeros(DECAY_RANK)], axis=1),
                          jnp.concatenate([zeros(DECAY_RANK), p['d_w_up'][o][1]], axis=1)], axis=0)
    a2 = jnp.concatenate([jnp.concatenate([p['d_a_up'][o][0], zeros(ICLR_RANK)], axis=1),
                          jnp.concatenate([zeros(ICLR_RANK), p['d_a_up'][o][1]], axis=1)], axis=0)
    consts = [row(p['d_mu_prev'][o]), row(p['d_mu_next'][o]), w2.astype(BF16), a2.astype(BF16),
              p['d_g_up'][o].astype(BF16), row(p['d_w0'][o]), row(p['d_a0'][o]),
              row(p['d_k_k'][o]), row(p['d_k_a'][o]), row(p['d_r_k'][o]), bs]
    nb8 = n_rows // 8
    t8 = tm // 8
    in_specs = [pl.BlockSpec((tm, D_IN), lambda i: (i, 0)),
                pl.BlockSpec((8, D_IN), lambda i: (jnp.maximum(i * t8 - 1, 0), 0)),
                pl.BlockSpec((8, D_IN), lambda i: (jnp.minimum((i + 1) * t8, nb8 - 1), 0))]
    in_specs += [pl.BlockSpec(a.shape, lambda i, nd=a.ndim: (0,) * nd) for a in consts]
    out_dtypes = [BF16, BF16, BF16, F32, F32, BF16, BF16, BF16, BF16, F32, BF16]
    prep = pl.pallas_call(
        functools.partial(_d_prep_kernel, tm=tm, segments=tuple(segments)),
        grid=(n_rows // tm,), in_specs=in_specs,
        out_specs=[pl.BlockSpec((tm, w), lambda i: (i, 0))] * len(out_dtypes),
        out_shape=[jax.ShapeDtypeStruct((n_rows, w), dt) for dt in out_dtypes],
        compiler_params=_cparams(("parallel",)), name="d_prep",
    )(u, u, u, *consts)
    r, v, a, lwf, lwb, kdf, kdb, bf_, bb_, bonus, g = prep
    of, ob = _scan(segments, n_rows, r, v, a, lwf, lwb, kdf, kdb, bf_, bb_)
    return [of, ob, bonus, g], [row(p['d_ln_g'][o]), row(p['d_ln_b'][o]), bs]


def kernel(x_prompt, x_sample, meta_tokens, pre_mix_g, post_mix_g, pre_ffn_g, post_ffn_g, even_w_in, even_w_out, a_q_gain, a_k_gain, b_norm_g, b_w, b_b, odd_w_in, odd_w_out, c_sink, d_mu_prev, d_mu_next, d_w0, d_w_up, d_a0, d_a_up, d_g_up, d_k_k, d_k_a, d_r_k, d_ln_g, d_ln_b, ffn_w_gate, ffn_w_up, ffn_w_down):
    params = dict(meta_tokens=meta_tokens, pre_mix_g=pre_mix_g, post_mix_g=post_mix_g,
                  pre_ffn_g=pre_ffn_g, post_ffn_g=post_ffn_g,
                  even_w_in=even_w_in, even_w_out=even_w_out, a_q_gain=a_q_gain, a_k_gain=a_k_gain,
                  b_norm_g=b_norm_g, b_w=b_w, b_b=b_b,
                  odd_w_in=odd_w_in, odd_w_out=odd_w_out, c_sink=c_sink,
                  d_mu_prev=d_mu_prev, d_mu_next=d_mu_next, d_w0=d_w0, d_w_up=d_w_up,
                  d_a0=d_a0, d_a_up=d_a_up, d_g_up=d_g_up, d_k_k=d_k_k, d_k_a=d_k_a, d_r_k=d_r_k,
                  d_ln_g=d_ln_g, d_ln_b=d_ln_b,
                  ffn_w_gate=ffn_w_gate, ffn_w_up=ffn_w_up, ffn_w_down=ffn_w_down)
    return _forward([x_prompt, x_sample], params)
```

```python
import functools
import math

import numpy as np
import jax
import jax.numpy as jnp
from jax import lax
from jax.experimental import pallas as pl
from jax.experimental.pallas import tpu as pltpu

F32 = jnp.float32
BF16 = jnp.bfloat16

D_MODEL = 1024
HEAD_DIM = 64
N_META = 16
GRID_W = 64
WINDOW = 128
RMS_EPS = 1e-6
A_HEADS, A_KV_HEADS, A_THETA = 12, 4, 10000.0
B_GROUPS, B_DIM = 4, 64
C_HEADS, C_KV_HEADS = 8, 2
ROPE_THETA = 500000.0
ROPE_DIMS = HEAD_DIM // 4
D_HEADS = 8
D_WIDTH = D_HEADS * HEAD_DIM
DECAY_RANK, ICLR_RANK, GATE_RANK = 64, 64, 128
LNX_EPS = 64e-5
D_FF = 2816
A_Q, A_KV, B_W = A_HEADS * HEAD_DIM, A_KV_HEADS * HEAD_DIM, B_GROUPS * B_DIM
EVEN_IN = A_Q + 2 * A_KV + B_W
C_Q, C_KV = C_HEADS * HEAD_DIM, C_KV_HEADS * HEAD_DIM
D_IN = 3 * D_WIDTH + 2 * DECAY_RANK + 2 * ICLR_RANK + GATE_RANK
ODD_IN = C_Q + 2 * C_KV + D_IN

LANES = 128
FRONT = 256
PAD = FRONT - N_META
NEG = -1e30
SCAN_CHUNK = 64
SCAN_CHUNKS_PER_STEP = 4
HGROUP = 256
FFT_L1 = 16
VMEM_LIMIT = 56 * 1024 * 1024
ROW_SUBTILE = 256
ATTN_CHUNK = 256
ATTN_MAX_CHUNKS = 20
ATTN_UNROLL = 8
LOG2_E = math.log2(math.e)


def _round_up(x, m):
    return (x + m - 1) // m * m


def _cparams(sem):
    return pltpu.CompilerParams(dimension_semantics=sem, vmem_limit_bytes=VMEM_LIMIT)


def _dot(a, b):
    return jnp.dot(a, b, preferred_element_type=F32)


def _dot_nt(a, b):
    return lax.dot_general(a, b, (((1,), (1,)), ((), ())), preferred_element_type=F32)


def _dot_tn(a, b):
    return lax.dot_general(a, b, (((0,), (0,)), ((), ())), preferred_element_type=F32)


def _split2(x):
    hi = x.astype(BF16)
    lo = (x - hi.astype(F32)).astype(BF16)
    return hi, lo


def _group_mean(x, bs):
    hi, lo = _split2(x)
    return _dot(hi, bs) + _dot(lo, bs)


def _rope(x, cos, sin_a, sin_b, shift):
    n = x.shape[1]
    return x * cos + pltpu.roll(x, n - shift, 1) * sin_a + pltpu.roll(x, shift, 1) * sin_b


def _wide(t, n):
    return t if n == LANES else jnp.concatenate([t] * (n // LANES), axis=1)


def _rms_rows(x, g):
    ms = jnp.mean(x * x, axis=-1, keepdims=True)
    return x * lax.rsqrt(ms + RMS_EPS) * g


def _in_even_kernel(h_ref, cos_ref, sa_ref, sb_ref, g_ref, w_ref, qg_ref, kg_ref, bg_ref, bs_ref, pq_ref,
                    q_ref, k_ref, v_ref, gf_ref):
    bs = bs_ref[...]
    for r in range(h_ref.shape[0] // ROW_SUBTILE):
        rows = slice(r * ROW_SUBTILE, (r + 1) * ROW_SUBTILE)
        hn = _rms_rows(h_ref[rows, :], g_ref[...]).astype(BF16)
        proj = _dot(hn, w_ref[...])
        cos, sa, sb = (_wide(t[rows, :], HGROUP) for t in (cos_ref, sa_ref, sb_ref))

        def norm_rope(x, gain):
            xn = x * lax.rsqrt(_group_mean(x * x, bs) + RMS_EPS) * gain
            return _rope(xn, cos, sa, sb, 16)

        for s in range(A_Q // HGROUP):
            cs = slice(s * HGROUP, (s + 1) * HGROUP)
            q_ref[cs, rows] = norm_rope(proj[:, cs], qg_ref[:, cs]).T.astype(BF16)
        k_ref[rows, :] = norm_rope(proj[:, A_Q:A_Q + A_KV], kg_ref[...]).astype(BF16)
        v_ref[r] = proj[:, A_Q + A_KV:A_Q + 2 * A_KV].T.astype(BF16)
        f = proj[:, A_Q + 2 * A_KV:]
        fn = f * lax.rsqrt(_group_mean(f * f, bs) + RMS_EPS) * bg_ref[...]
        gf_ref[rows, :] = _dot(fn.astype(BF16), pq_ref[...])


def _in_odd_kernel(h_ref, cos_ref, sa_ref, sb_ref, g_ref, w_ref, q_ref, k_ref, v_ref, u_ref):
    for r in range(h_ref.shape[0] // ROW_SUBTILE):
        rows = slice(r * ROW_SUBTILE, (r + 1) * ROW_SUBTILE)
        hn = _rms_rows(h_ref[rows, :], g_ref[...]).astype(BF16)
        proj = _dot(hn, w_ref[...])
        cos, sa, sb = cos_ref[rows, :], sa_ref[rows, :], sb_ref[rows, :]
        cos2, sa2, sb2 = (_wide(t, HGROUP) for t in (cos, sa, sb))
        for s in range(C_Q // HGROUP):
            cs = slice(s * HGROUP, (s + 1) * HGROUP)
            q_ref[cs, rows] = (_rope(proj[:, cs], cos2, sa2, sb2, 8)
                               * (HEAD_DIM ** -0.5 * LOG2_E)).T.astype(BF16)
        k_ref[rows, :] = _rope(proj[:, C_Q:C_Q + C_KV], cos, sa, sb, 8).astype(BF16)
        v = proj[:, C_Q + C_KV:C_Q + 2 * C_KV]
        for c in range(ROW_SUBTILE // C_KV):
            v_ref[r * (ROW_SUBTILE // C_KV) + c] = v[c * C_KV:(c + 1) * C_KV, :].T.astype(BF16)
        u_ref[rows, :] = proj[:, C_Q + 2 * C_KV:]


def _mix_ffn_even_kernel(h_ref, yb_ref, *rest, chunk):
    _mix_ffn(h_ref, yb_ref[...], *rest, chunk=chunk)


def _mix_ffn_odd_kernel(h_ref, of_ref, ob_ref, bonus_ref, gate_ref, lng_ref, lnb_ref, bs_ref, *rest, chunk):
    o = of_ref[...] + ob_ref[...]
    bs = bs_ref[...]

    def head_mean(x):
        return jnp.concatenate(
            [_group_mean(x[:, s * HGROUP:(s + 1) * HGROUP], bs) for s in range(D_WIDTH // HGROUP)], axis=1)

    cen = o - head_mean(o)
    var = head_mean(cen * cen)
    y = cen * lax.rsqrt(var + LNX_EPS) * lng_ref[...] + lnb_ref[...] + bonus_ref[...]
    _mix_ffn(h_ref, (y * gate_ref[...]).astype(BF16), *rest, chunk=chunk)


def _mix_ffn(h_ref, yb, wa_ref, wb_ref, gm_ref, g1_ref, wg_ref, wu_ref, wd_ref, g2_ref,
             yat_ref, o_ref, acc_ref, *, chunk):
    mix = _dot_tn(yat_ref[...], wa_ref[...]) + _dot(yb, wb_ref[...])
    h = h_ref[...] + _rms_rows(mix, gm_ref[...])
    hn = _rms_rows(h, g1_ref[...]).astype(BF16)
    for c in range(D_FF // chunk):
        cs = slice(c * chunk, (c + 1) * chunk)
        gate = _dot(hn, wg_ref[:, cs])
        up = _dot(hn, wu_ref[:, cs])
        act = (gate * (1.0 / (1.0 + jnp.exp(-gate))) * up).astype(BF16)
        part = _dot(act, wd_ref[cs, :])
        if c == 0:
            acc_ref[...] = part
        else:
            acc_ref[...] += part
    o_ref[...] = h + _rms_rows(acc_ref[...], g2_ref[...])


def _row_call(name, kernel, n_rows, tm, row_ins, const_ins, outs, scratch=(), col_ins=()):
    grid = (n_rows // tm,)
    in_specs = [pl.BlockSpec((tm, a.shape[1]), lambda i: (i, 0)) for a in row_ins]
    in_specs += [pl.BlockSpec(a.shape, lambda i, nd=a.ndim: (0,) * nd, pipeline_mode=pl.Buffered(1))
                 for a in const_ins]
    in_specs += [pl.BlockSpec((a.shape[0], tm), lambda i: (0, i)) for a in col_ins]
    out_specs, out_shape = [], []
    for kind, n, dt in outs:
        if kind == "rows":
            out_specs.append(pl.BlockSpec((tm, n), lambda i: (i, 0)))
            out_shape.append(jax.ShapeDtypeStruct((n_rows, n), dt))
        elif kind == "cols":
            out_specs.append(pl.BlockSpec((n, tm), lambda i: (0, i)))
            out_shape.append(jax.ShapeDtypeStruct((n, n_rows), dt))
        else:
            out_specs.append(pl.BlockSpec((tm // n, n, n), lambda i: (i, 0, 0)))
            out_shape.append(jax.ShapeDtypeStruct((n_rows // n, n, n), dt))
    return pl.pallas_call(
        kernel, grid=grid, in_specs=in_specs, out_specs=out_specs, out_shape=out_shape,
        scratch_shapes=list(scratch), compiler_params=_cparams(("parallel",)), name=name,
    )(*row_ins, *const_ins, *col_ins)


def _padded_q(q_ref, h, kv):
    qh = q_ref[h * HEAD_DIM:(h + 1) * HEAD_DIM, :]
    zero = jnp.zeros_like(qh)
    return jnp.concatenate([qh, zero] if kv % 2 == 0 else [zero, qh], axis=0)


def _next_step(b, i, j, bsz, nq, nkb):
    j1 = j + 1
    wrap_j = (j1 == nkb).astype(jnp.int32)
    i1 = i + wrap_j
    wrap_i = (i1 == nq).astype(jnp.int32)
    return jnp.minimum(b + wrap_i, bsz - 1), i1 * (1 - wrap_i), j1 * (1 - wrap_j)


def _attn_a_kernel(qT_ref, qn_ref, k_ref, kn_ref, vT_ref, o_ref, m_scr, l_scr, acc_scr, s_scr, mx_scr,
                   *, nchunk, nkb, nq, bsz, tq):
    b, i, j = pl.program_id(0), pl.program_id(1), pl.program_id(2)

    @pl.when(j == 0)
    def _():
        m_scr[...] = jnp.full(m_scr.shape, NEG, F32)
        l_scr[...] = jnp.zeros(l_scr.shape, F32)
        acc_scr[...] = jnp.zeros(acc_scr.shape, F32)

    group = A_HEADS // A_KV_HEADS
    ck = ATTN_CHUNK
    row_iota = lax.broadcasted_iota(jnp.int32, (ck, tq), 0)
    key_row = row_iota + j * (nchunk * ck)
    key_row_next = row_iota + _next_step(b, i, j, bsz, nq, nkb)[2] * (nchunk * ck)

    def fold(s, op):
        return op(s.reshape(ck // 8, 8, tq), axis=0)

    def gang_q(g):
        if g == A_KV_HEADS:
            return [_padded_q(qn_ref, t, 0) for t in range(group)]
        return [_padded_q(qT_ref, g * group + t, g) for t in range(group)]

    def gang_scores(g, qps, c, mxs, first=False):
        rows = pl.ds(pl.multiple_of(c * ck, ck), ck)
        if g == A_KV_HEADS:
            kc, krow = kn_ref[0, rows, :], key_row_next
        else:
            kc, krow = k_ref[0, rows, (g // 2) * LANES:(g // 2 + 1) * LANES], key_row
        out = []
        for t in range(group):
            s = _dot(kc, qps[t])
            if first:
                s = jnp.where(krow >= PAD, s, NEG)
            s_scr[g % 2, t, c] = s
            out.append(fold(s, jnp.max) if mxs is None else jnp.maximum(mxs[t], fold(s, jnp.max)))
        return tuple(out)

    def gang_values(g, c, m_news):
        vc = vT_ref[0, c, g * HEAD_DIM:(g + 1) * HEAD_DIM, :]
        ps = [jnp.exp2(s_scr[g % 2, t, c] - m_news[t]) for t in range(group)]
        return tuple(_dot(vc, p.astype(BF16)) for p in ps), tuple(fold(p, jnp.sum) for p in ps)

    def add(xs, ys):
        return tuple(x + y for x, y in zip(xs, ys))

    unroll = max(u for u in range(1, ATTN_UNROLL + 1) if max(nchunk - 1, 1) % u == 0)

    @pl.when((b == 0) & (i == 0) & (j == 0))
    def _():
        qps = gang_q(0)
        mxs = lax.fori_loop(1, nchunk, lambda c, mxs: gang_scores(0, qps, c, mxs),
                            gang_scores(0, qps, 0, None, first=True), unroll=unroll)
        for t in range(group):
            mx_scr[t] = mxs[t]

    mxs = tuple(mx_scr[t] for t in range(group))
    for g in range(A_KV_HEADS):
        heads = [g * group + t for t in range(group)]
        m_olds = [m_scr[h] for h in heads]
        m_news = [jnp.maximum(mo, jnp.max(mx, axis=0, keepdims=True)) for mo, mx in zip(m_olds, mxs)]
        qps = gang_q(g + 1)

        def body(c, carry, g=g, m_news=m_news, qps=qps):
            accs, lsums, mxn = carry
            mxn = gang_scores(g + 1, qps, c, mxn)
            pvs, pss = gang_values(g, c, m_news)
            return add(accs, pvs), add(lsums, pss), mxn

        mx0 = gang_scores(g + 1, qps, 0, None, first=True)
        accs, lsums, mxs = lax.fori_loop(1, nchunk, body, (*gang_values(g, 0, m_news), mx0), unroll=unroll)
        for t, h in enumerate(heads):
            alpha = jnp.exp2(m_olds[t] - m_news[t])
            acc_scr[h] = acc_scr[h] * alpha + accs[t]
            l_scr[h] = l_scr[h] * alpha + jnp.sum(lsums[t], axis=0, keepdims=True)
            m_scr[h] = m_news[t]
    for t in range(group):
        mx_scr[t] = mxs[t]

    @pl.when(j == nkb - 1)
    def _():
        for h in range(A_HEADS):
            o_ref[h * HEAD_DIM:(h + 1) * HEAD_DIM, :] = (acc_scr[h] / l_scr[h]).astype(o_ref.dtype)


def _attn_a(qT, k, vT, start):
    bsz, lp = k.shape[:2]
    off = start // ATTN_CHUNK
    tq = ATTN_CHUNK
    total = lp // ATTN_CHUNK
    nkb = min(n for n in range(1, total + 1) if total % n == 0 and total // n <= ATTN_MAX_CHUNKS)
    tk = lp // nkb
    nchunk = tk // ATTN_CHUNK
    nq = lp // tq
    group = A_HEADS // A_KV_HEADS
    kernel = functools.partial(_attn_a_kernel, nchunk=nchunk, nkb=nkb, nq=nq, bsz=bsz, tq=tq)

    def next_q(b, i, j):
        b2, i2, _ = _next_step(b, i, j, bsz, nq, nkb)
        return 0, off + b2 * nq + i2

    def next_k(b, i, j):
        b2, _, j2 = _next_step(b, i, j, bsz, nq, nkb)
        return b2, j2, 0

    return pl.pallas_call(
        kernel, grid=(bsz, nq, nkb),
        in_specs=[
            pl.BlockSpec((A_Q, tq), lambda b, i, j: (0, off + b * nq + i)),
            pl.BlockSpec((group * HEAD_DIM, tq), next_q),
            pl.BlockSpec((1, tk, A_KV), lambda b, i, j: (b, j, 0)),
            pl.BlockSpec((1, tk, LANES), next_k),
            pl.BlockSpec((1, nchunk, A_KV, ATTN_CHUNK), lambda b, i, j: (b, j, 0, 0)),
        ],
        out_specs=pl.BlockSpec((A_Q, tq), lambda b, i, j: (0, b * nq + i)),
        out_shape=jax.ShapeDtypeStruct((A_Q, bsz * lp), BF16),
        scratch_shapes=[
            pltpu.VMEM((A_HEADS, 1, tq), F32),
            pltpu.VMEM((A_HEADS, 1, tq), F32),
            pltpu.VMEM((A_HEADS, HEAD_DIM, tq), F32),
            pltpu.VMEM((2, group, nchunk, ATTN_CHUNK, tq), F32),
            pltpu.VMEM((group, 8, tq), F32),
        ],
        compiler_params=_cparams(("arbitrary", "arbitrary", "arbitrary")), name="attn_a",
    )(qT, qT, k, k, vT)


def _attn_c_kernel(sink_ref, qT_ref, k0, k1, k2, k3, v0, v1, v2, v3, o_ref, *, lp):
    j = pl.program_id(1)
    k_refs, v_refs = (k0, k1, k2, k3), (v0, v1, v2, v3)
    row = lax.broadcasted_iota(jnp.int32, (LANES, LANES), 0)
    pq = lax.broadcasted_iota(jnp.int32, (LANES, LANES), 1) + j * LANES
    biases = []
    for slot in range(4):
        if slot == 0:
            bias = jnp.where(row >= PAD % LANES, 0.0, NEG)
        else:
            pk = row + (j + slot - 2) * LANES
            in_window = jnp.where(jnp.abs(pq - pk) <= WINDOW, 0.0, NEG)
            bias = jnp.where(pk >= FRONT, jnp.where(pk < lp, in_window, NEG), NEG)
        biases.append(bias.astype(F32))
    group = C_HEADS // C_KV_HEADS
    for h in range(C_HEADS):
        kv = h // group
        qp = _padded_q(qT_ref, h, kv)
        sink = sink_ref[h] * LOG2_E
        ss = [_dot(k_refs[t][0], qp) + biases[t] for t in range(4)]
        m = jnp.maximum(jnp.maximum(ss[0], ss[1]), jnp.maximum(ss[2], ss[3]))
        m = jnp.maximum(jnp.max(m, axis=0, keepdims=True), sink)
        acc = jnp.zeros((HEAD_DIM, LANES), F32)
        denom = jnp.exp2(sink - m)
        for t in range(4):
            p = jnp.exp2(ss[t] - m)
            acc = acc + _dot(v_refs[t][0, 0, kv * HEAD_DIM:(kv + 1) * HEAD_DIM, :], p.astype(BF16))
            denom = denom + jnp.sum(p, axis=0, keepdims=True)
        o_ref[h * HEAD_DIM:(h + 1) * HEAD_DIM, :] = (acc / denom).astype(o_ref.dtype)


def _attn_c(sink, qT, k, vT, start):
    bsz, lp = k.shape[:2]
    nb = lp // LANES
    off = start // LANES
    kernel = functools.partial(_attn_c_kernel, lp=lp)

    def kspec(fn):
        return pl.BlockSpec((1, LANES, C_KV), lambda b, j: (b, fn(j), 0))

    def vspec(fn):
        return pl.BlockSpec((1, 1, C_KV, LANES), lambda b, j: (b, fn(j), 0, 0))

    fns = (lambda j: PAD // LANES, lambda j: jnp.maximum(j - 1, 0), lambda j: j,
           lambda j: jnp.minimum(j + 1, nb - 1))
    return pl.pallas_call(
        kernel, grid=(bsz, nb),
        in_specs=[pl.BlockSpec(memory_space=pltpu.SMEM),
                  pl.BlockSpec((C_Q, LANES), lambda b, j: (0, off + b * nb + j))]
                 + [kspec(f) for f in fns] + [vspec(f) for f in fns],
        out_specs=pl.BlockSpec((C_Q, LANES), lambda b, j: (0, b * nb + j)),
        out_shape=jax.ShapeDtypeStruct((C_Q, bsz * lp), BF16),
        compiler_params=_cparams(("parallel", "parallel")), name="attn_c",
    )(sink, qT, k, k, k, k, vT, vT, vT, vT)


def _fft16(vals):
    n = FFT_L1
    bits = n.bit_length() - 1
    a = [vals[int(format(idx, '0%db' % bits)[::-1], 2)] for idx in range(n)]
    m = 2
    while m <= n:
        half = m // 2
        for base in range(0, n, m):
            for t in range(half):
                ur, ui = a[base + t]
                vr, vi = a[base + t + half]
                if 4 * t == m:
                    tr, ti = vi, -vr
                elif t > 0:
                    wr, wi = math.cos(-2.0 * math.pi * t / m), math.sin(-2.0 * math.pi * t / m)
                    tr, ti = vr * wr - vi * wi, vr * wi + vi * wr
                else:
                    tr, ti = vr, vi
                a[base + t] = (ur + tr, ui + ti)
                a[base + t + half] = (ur - tr, ui - ti)
        m *= 2
    return a


def _fft1_kernel(x_ref, o_ref):
    def strip(s, carry):
        rows = pl.ds(pl.multiple_of(s * 8, 8), 8)
        for q in range(B_W // LANES):
            re_cols = slice(q * LANES, (q + 1) * LANES)
            im_cols = slice(B_W + q * LANES, B_W + (q + 1) * LANES)
            vals = [(x_ref[0, n, rows, re_cols], x_ref[0, n, rows, im_cols]) for n in range(FFT_L1)]
            for kk, (re, im) in enumerate(_fft16(vals)):
                o_ref[0, kk, rows, re_cols] = re
                o_ref[0, kk, rows, im_cols] = im
        return carry

    lax.fori_loop(0, x_ref.shape[2] // 8, strip, 0)


def _fft2_kernel(a_ref, tc_ref, ts_ref, c2_ref, s2_ref, bias_ref, o_ref):
    a = a_ref[0, 0]
    tc, ts = _wide(tc_ref[0], B_W), _wide(ts_ref[0], B_W)
    are, aim = a[:, :B_W], a[:, B_W:]
    bre = (are * tc + aim * ts).astype(BF16)
    bim = (aim * tc - are * ts).astype(BF16)
    o_ref[0] = _dot(c2_ref[...], bre) + _dot(s2_ref[...], bim) + bias_ref[...]


def _dft_tables(l2, l2p, l2o):
    l1 = FFT_L1
    length = l1 * l2
    n1 = np.arange(l1)
    n2 = np.arange(l2)
    angt = 2.0 * np.pi * (n1[:, None] * n2[None, :]) / length
    tc = np.zeros((l1, l2p, LANES), np.float32)
    ts = np.zeros((l1, l2p, LANES), np.float32)
    tc[:, :l2, :] = np.cos(angt)[:, :, None]
    ts[:, :l2, :] = np.sin(angt)[:, :, None]
    ang2 = 2.0 * np.pi * ((n2[:, None] * n2[None, :]) % l2) / l2
    scale = 1.0 / math.sqrt(B_DIM * length)
    c2 = np.zeros((l2o, l2p), np.float32)
    s2 = np.zeros((l2o, l2p), np.float32)
    c2[:l2, :l2] = np.cos(ang2) * scale
    s2[:l2, :l2] = np.sin(ang2) * scale
    as_bf = lambda t: jnp.asarray(t, F32).astype(BF16)
    return jnp.asarray(tc), jnp.asarray(ts), as_bf(c2), as_bf(s2)


def _mixer_b(gseq, bias):
    bsz, length, _ = gseq.shape
    l1 = FFT_L1
    l2 = length // l1
    l2p, l2o = _round_up(l2, LANES), _round_up(l2, 8)
    tc, ts, c2, s2 = _dft_tables(l2, l2p, l2o)
    x = jnp.pad(gseq.reshape(bsz, l1, l2, 2 * B_W), ((0, 0), (0, 0), (0, l2p - l2), (0, 0)))
    blk = (1, l1, LANES, 2 * B_W)
    a = pl.pallas_call(
        _fft1_kernel, grid=(bsz, l2p // LANES),
        in_specs=[pl.BlockSpec(blk, lambda b, i: (b, 0, i, 0))],
        out_specs=pl.BlockSpec(blk, lambda b, i: (b, 0, i, 0)),
        out_shape=jax.ShapeDtypeStruct((bsz, l1, l2p, 2 * B_W), F32),
        compiler_params=_cparams(("parallel", "parallel")), name="fft1",
    )(x)
    y = pl.pallas_call(
        _fft2_kernel, grid=(bsz, l1),
        in_specs=[pl.BlockSpec((1, 1, l2p, 2 * B_W), lambda b, k: (b, k, 0, 0)),
                  pl.BlockSpec((1, l2p, LANES), lambda b, k: (k, 0, 0)),
                  pl.BlockSpec((1, l2p, LANES), lambda b, k: (k, 0, 0)),
                  pl.BlockSpec((l2o, l2p), lambda b, k: (0, 0)),
                  pl.BlockSpec((l2o, l2p), lambda b, k: (0, 0)),
                  pl.BlockSpec((1, B_W), lambda b, k: (0, 0))],
        out_specs=pl.BlockSpec((1, l2o, B_W), lambda b, k: (b, 0, k)),
        out_shape=jax.ShapeDtypeStruct((bsz, l2o, l1 * B_W), F32),
        compiler_params=_cparams(("parallel", "parallel")), name="fft2",
    )(a, tc, ts, c2, s2, bias)
    return y[:, :l2].reshape(bsz, length, B_W)


def _seq_position(rows, segments):
    pos = jnp.full(rows.shape, -1.0, F32)
    seqlen = jnp.full(rows.shape, 1.0, F32)
    for start, bsz, lp in segments:
        rel = rows - float(start)
        q = jnp.floor((rel + 0.5) * (1.0 / lp))
        inside = jnp.where(rel >= 0.0, jnp.where(rel < float(bsz * lp), 1.0, 0.0), 0.0) > 0.5
        pos = jnp.where(inside, rel - q * lp, pos)
        seqlen = jnp.where(inside, float(lp), seqlen)
    return pos, seqlen


def _sigmoid(x):
    return 1.0 / (1.0 + jnp.exp(-x))


def _d_prep_kernel(u_ref, up_ref, un_ref, mup_ref, mun_ref, w2_ref, a2_ref, gup_ref, w0_ref, a0_ref,
                   kk_ref, ka_ref, rk_ref, bs_ref,
                   r_ref, v_ref, a_ref, lwf_ref, lwb_ref, kdf_ref, kdb_ref, bf_ref, bb_ref,
                   bonus_ref, g_ref, *, tm, segments):
    i = pl.program_id(0)
    u = u_ref[...]
    rows = (lax.broadcasted_iota(jnp.int32, (tm, 1), 0) + i * tm).astype(F32)
    pos, seqlen = _seq_position(rows, segments)
    local = lax.broadcasted_iota(jnp.int32, (tm, 1), 0)
    u_prev = jnp.where(local == 0, up_ref[7:8, :], pltpu.roll(u, 1, 0))
    u_next = jnp.where(local == tm - 1, un_ref[0:1, :], pltpu.roll(u, tm - 1, 0))
    u_prev = jnp.where(pos == float(PAD), 0.0, u_prev)
    u_next = jnp.where(pos == seqlen - 1.0, 0.0, u_next)
    u = u + mup_ref[...] * (u_prev - u) + mun_ref[...] * (u_next - u)
    valid = jnp.where(pos >= float(PAD), 1.0, 0.0)

    w = D_WIDTH
    r, k, v = u[:, :w], u[:, w:2 * w], u[:, 2 * w:3 * w]
    c0 = 3 * w
    dec = _dot(jnp.tanh(u[:, c0:c0 + 2 * DECAY_RANK]).astype(BF16), w2_ref[...]) + w0_ref[...]
    c0 += 2 * DECAY_RANK
    icl = _dot(u[:, c0:c0 + 2 * ICLR_RANK].astype(BF16), a2_ref[...]) + a0_ref[...]
    c0 += 2 * ICLR_RANK
    g_ref[...] = _dot(_sigmoid(u[:, c0:c0 + GATE_RANK]).astype(BF16), gup_ref[...]).astype(g_ref.dtype)

    bs = bs_ref[...]

    def head_sum(x):
        return jnp.concatenate(
            [_group_mean(x[:, s * HGROUP:(s + 1) * HGROUP], bs) for s in range(w // HGROUP)], axis=1
        ) * float(HEAD_DIM)

    kk = k * kk_ref[...]
    kk = kk * lax.rsqrt(jnp.maximum(head_sum(kk * kk), 1e-24))
    r_ref[...] = r.astype(r_ref.dtype)
    v_ref[...] = v.astype(v_ref.dtype)
    a_ref[...] = (-kk * valid).astype(a_ref.dtype)
    bonus = jnp.zeros_like(r)
    for d, (lw_ref, kd_ref, b_ref) in enumerate(((lwf_ref, kdf_ref, bf_ref), (lwb_ref, kdb_ref, bb_ref))):
        x = -dec[:, d * w:(d + 1) * w]
        softplus = jnp.maximum(x, 0.0) + jnp.log(1.0 + jnp.exp(-jnp.abs(x)))
        lw_ref[...] = -jnp.exp(-softplus - 0.5)
        gate = _sigmoid(icl[:, d * w:(d + 1) * w])
        kd = k * (1.0 + (gate - 1.0) * ka_ref[...])
        kd_ref[...] = (kd * valid).astype(kd_ref.dtype)
        b_ref[...] = (kk * gate * valid).astype(b_ref.dtype)
        bonus = bonus + head_sum(r * kd * rk_ref[...]) * v
    bonus_ref[...] = bonus


def _scan_kernel(fwd_blk, bwd_blk, first,
                 rf, vf, af, lwf, kdf, bf_, rb, vb, ab, lwb, kdb, bb_,
                 of_ref, ob_ref, s_scr, *, nk):
    del fwd_blk, bwd_blk
    step = pl.program_id(0)
    c = SCAN_CHUNK

    @pl.when(first[step] == 1)
    def _():
        s_scr[...] = jnp.zeros(s_scr.shape, F32)

    t_sq = lax.broadcasted_iota(jnp.int32, (nk * c, nk * c), 0)
    s_sq = lax.broadcasted_iota(jnp.int32, (nk * c, nk * c), 1)
    t_cat = lax.broadcasted_iota(jnp.int32, (c, HGROUP), 0)
    s_cat = lax.broadcasted_iota(jnp.int32, (c, HGROUP), 1) & (c - 1)
    bd_row = lax.broadcasted_iota(jnp.int32, (HGROUP, HGROUP), 0) >> 6
    bd_col = lax.broadcasted_iota(jnp.int32, (HGROUP, HGROUP), 1) >> 6
    bd_mask = bd_row == bd_col
    eye_cat = jnp.where(t_cat == s_cat, 1.0, 0.0).astype(F32)

    def block_diag(x):
        return jnp.where(bd_mask, jnp.concatenate([x] * (HGROUP // c), axis=0), 0.0).astype(BF16)

    chains = []
    dirs = ((0, rf, vf, af, lwf, kdf, bf_), (1, rb, vb, ab, lwb, kdb, bb_))
    for d, r_ref, v_ref, a_ref, lw_ref, kd_ref, b_ref in dirs:
        rev = d == 1
        earlier = (s_sq >= t_sq) if rev else (s_sq <= t_sq)
        tri = jnp.where((t_sq >> 6) == (s_sq >> 6), jnp.where(earlier, 1.0, 0.0), 0.0).astype(BF16)
        lw = lw_ref[...]
        h1 = lw.astype(BF16)
        r1 = lw - h1.astype(F32)
        h2 = r1.astype(BF16)
        h3 = (r1 - h2.astype(F32)).astype(BF16)
        cum = _dot(tri, h1) + _dot(tri, h2) + _dot(tri, h3)
        e_in = jnp.exp(cum)
        e_neg = jnp.exp(-cum)
        a_t = a_ref[...] * jnp.exp(cum - lw)
        r_t = r_ref[...] * e_in
        b_t = b_ref[...] * e_neg
        k_t = kd_ref[...] * e_neg
        v_all = v_ref[...]
        for ck in range(nk):
            rows = slice(ck * c, (ck + 1) * c)
            w_row = ck * c if rev else (ck + 1) * c - 1
            for g in range(D_WIDTH // HGROUP):
                cs = slice(g * HGROUP, (g + 1) * HGROUP)
                chains.append(dict(d=d, ck=ck, g=g, rev=rev, at=a_t[rows, cs], rt=r_t[rows, cs],
                                   bt=b_t[rows, cs], kt=k_t[rows, cs], vc=v_all[rows, cs],
                                   w=e_in[w_row:w_row + 1, cs]))

    def strict(ch):
        return (s_cat > t_cat) if ch['rev'] else (s_cat < t_cat)

    def incl(ch):
        return (s_cat >= t_cat) if ch['rev'] else (s_cat <= t_cat)

    for ch in chains:
        ch['lhs'] = jnp.concatenate([ch['at'], ch['rt']], axis=0).astype(BF16)
        ch['bd_v'] = block_diag(ch['vc'])
    m_b = [_dot_nt(ch['lhs'], block_diag(ch['bt'])) for ch in chains]
    m_k = [_dot_nt(ch['lhs'], block_diag(ch['kt'])) for ch in chains]
    for ch, mb, mk in zip(chains, m_b, m_k):
        ch['a_ab'] = jnp.where(strict(ch), mb[:c], 0.0)
        ch['a_rb'] = jnp.where(incl(ch), mb[c:], 0.0).astype(BF16)
        ch['a_ak'] = jnp.where(strict(ch), mk[:c], 0.0).astype(BF16)
        ch['a_rk'] = jnp.where(incl(ch), mk[c:], 0.0).astype(BF16)
        ch['t'] = eye_cat + jnp.where((t_cat >> 1) == (s_cat >> 1), ch['a_ab'], 0.0)
    m = 2
    while m < c:
        sh = int(math.log2(m))
        xs = []
        for ch in chains:
            a_m = jnp.where((t_cat >> (sh + 1)) == (s_cat >> (sh + 1)),
                            jnp.where((t_cat >> sh) != (s_cat >> sh), ch['a_ab'], 0.0), 0.0)
            xs.append(_dot(a_m.astype(BF16), block_diag(ch['t'])))
        for ch, x in zip(chains, xs):
            ch['t'] = ch['t'] + _dot(ch['t'].astype(BF16), block_diag(x))
        m *= 2
    akv = [_dot(ch['a_ak'], ch['bd_v']) for ch in chains]
    for ch in chains:
        ch['t'] = ch['t'].astype(BF16)
    a_hat = [_dot(ch['t'], block_diag(ch['at'])) for ch in chains]
    u_hat = [_dot(ch['t'], block_diag(x)) for ch, x in zip(chains, akv)]
    r_hat = [(ch['rt'] + _dot(ch['a_rb'], block_diag(x))).astype(BF16) for ch, x in zip(chains, a_hat)]
    o_hat = [_dot(ch['a_rb'], block_diag(x)) + _dot(ch['a_rk'], ch['bd_v']) for ch, x in zip(chains, u_hat)]
    m_c = [jnp.where(bd_mask, _dot_tn(x.astype(BF16), ch['bt'].astype(BF16)), 0.0).astype(BF16)
           for ch, x in zip(chains, a_hat)]
    n_c = [jnp.where(bd_mask, _dot_tn(jnp.concatenate([x, ch['vc']], axis=0).astype(BF16),
                                      jnp.concatenate([ch['bt'], ch['kt']], axis=0).astype(BF16)), 0.0)
           for ch, x in zip(chains, u_hat)]
    pre = {(ch['d'], ch['ck'], ch['g']): (r_hat[i], o_hat[i], m_c[i], n_c[i], ch['w'])
           for i, ch in enumerate(chains)}

    groups = [(d, g) for d in range(2) for g in range(D_WIDTH // HGROUP)]
    states = {dg: s_scr[dg[0], dg[1]] for dg in groups}
    for i in range(nk):
        st_b = {dg: states[dg].astype(BF16) for dg in groups}
        for d, g in groups:
            ck = nk - 1 - i if d == 1 else i
            rh, oh, mc, nc_, w = pre[(d, ck, g)]
            o_ref = ob_ref if d == 1 else of_ref
            o_ref[ck * c:(ck + 1) * c, g * HGROUP:(g + 1) * HGROUP] = _dot_nt(rh, st_b[(d, g)]) + oh
            states[(d, g)] = (states[(d, g)] + _dot(st_b[(d, g)], mc) + nc_) * w
    for d, g in groups:
        s_scr[d, g] = states[(d, g)]


def _scan_tables(segments, rows):
    fwd, bwd, first = [], [], []
    for start, bsz, lp in segments:
        nc = lp // rows
        for b in range(bsz):
            base = (start + b * lp) // rows
            for ci in range(nc):
                fwd.append(base + ci)
                bwd.append(base + nc - 1 - ci)
                first.append(1 if ci == 0 else 0)
    as_i32 = lambda t: jnp.asarray(np.asarray(t, np.int32))
    return as_i32(fwd), as_i32(bwd), as_i32(first)


def _scan(segments, n_rows, r, v, a, lwf, lwb, kdf, kdb, bf_, bb_):
    nk = SCAN_CHUNKS_PER_STEP
    rows = nk * SCAN_CHUNK
    if any(start % rows or lp % rows for start, _, lp in segments):
        nk, rows = 1, SCAN_CHUNK
    fwd, bwd, first = _scan_tables(segments, rows)
    nsteps = fwd.shape[0]
    blk = (rows, D_WIDTH)
    fspec = pl.BlockSpec(blk, lambda s, fw, bw, fi: (fw[s], 0))
    bspec = pl.BlockSpec(blk, lambda s, fw, bw, fi: (bw[s], 0))
    grid_spec = pltpu.PrefetchScalarGridSpec(
        num_scalar_prefetch=3, grid=(nsteps,),
        in_specs=[fspec] * 6 + [bspec] * 6,
        out_specs=[fspec, bspec],
        scratch_shapes=[pltpu.VMEM((2, D_WIDTH // HGROUP, HGROUP, HGROUP), F32)],
    )
    return pl.pallas_call(
        functools.partial(_scan_kernel, nk=nk), grid_spec=grid_spec,
        out_shape=[jax.ShapeDtypeStruct((n_rows, D_WIDTH), F32)] * 2,
        compiler_params=_cparams(("arbitrary",)), name="wkv_scan",
    )(fwd, bwd, first, r, v, a, lwf, kdf, bf_, r, v, a, lwb, kdb, bb_)


def _rope_tables(segments, n_rows, layer_kind):
    outs = []
    for start, bsz, lp in segments:
        p = jnp.arange(lp)
        d = jnp.arange(HEAD_DIM)
        if layer_kind == "axial":
            t = p - FRONT
            row = jnp.where(t >= 0, t // GRID_W, jnp.where(p >= PAD, t, 0)).astype(F32)
            col = jnp.where(t >= 0, t % GRID_W, jnp.where(p >= PAD, t, 0)).astype(F32)
            half = HEAD_DIM // 2
            inv = A_THETA ** (-jnp.arange(0, half, 2, dtype=F32) / half)
            ang = jnp.concatenate([row[:, None] * inv] * 2 + [col[:, None] * inv] * 2, axis=1)
            first = (d % 32) < 16
            cos = jnp.cos(ang)
            sa = jnp.where(first[None, :], -jnp.sin(ang), 0.0)
            sb = jnp.where(first[None, :], 0.0, jnp.sin(ang))
        else:
            pos = jnp.maximum(p - PAD, 0).astype(F32)
            inv = ROPE_THETA ** (-jnp.arange(0, ROPE_DIMS, 2, dtype=F32) / ROPE_DIMS)
            ang8 = pos[:, None] * inv
            ang = jnp.concatenate([ang8, ang8] + [jnp.zeros_like(ang8)] * 6, axis=1)
            cos = jnp.where((d < ROPE_DIMS)[None, :], jnp.cos(ang), 1.0)
            sa = jnp.where((d < 8)[None, :], -jnp.sin(ang), 0.0)
            sb = jnp.where(((d >= 8) & (d < 16))[None, :], jnp.sin(ang), 0.0)
        small = lax.optimization_barrier([jnp.concatenate([t_, t_], axis=1) for t_ in (cos, sa, sb)])
        tabs = [jnp.tile(t_, (bsz, 1)) for t_ in small]
        outs.append(tabs)
    tail = n_rows - sum(b * lp for _, b, lp in segments)
    res = []
    for idx in range(3):
        parts = [o[idx] for o in outs] + [jnp.zeros((tail, LANES), F32)]
        res.append(jnp.concatenate(parts, axis=0))
    return res


def _seq_view(flat, seg, width):
    start, bsz, lp = seg
    return flat[start:start + bsz * lp].reshape(bsz, lp, width)


def _attn_operands(qT, k, vT, seg):
    start, bsz, lp = seg
    chunk = vT.shape[2]
    k_seg = k[start:start + bsz * lp].reshape(bsz, lp, k.shape[1])
    v_seg = vT[start // chunk:(start + bsz * lp) // chunk].reshape(bsz, lp // chunk, vT.shape[1], chunk)
    return qT, k_seg, v_seg, start


def _to_flat(parts, n_rows, width, dtype):
    used = sum(p.shape[0] for p in parts)
    return jnp.concatenate(parts + [jnp.zeros((n_rows - used, width), dtype)], axis=0)


def _to_flat_cols(parts, n_rows):
    used = sum(p.shape[1] for p in parts)
    return jnp.concatenate(parts + [jnp.zeros((parts[0].shape[0], n_rows - used), parts[0].dtype)], axis=1)


def _block_avg(n, group):
    idx = np.arange(n) // group
    return jnp.asarray((idx[:, None] == idx[None, :]).astype(np.float32) / group).astype(BF16)


def _forward(xs, p):
    segments = []
    start = 0
    for x in xs:
        bsz, s, _ = x.shape
        lp = s + FRONT
        segments.append((start, bsz, lp))
        start += bsz * lp
    n_used = start
    tm = 512 if n_used >= 4096 else ATTN_CHUNK
    n_rows = _round_up(n_used, tm)

    meta = p['meta_tokens'].astype(F32)
    parts = []
    for x in xs:
        bsz = x.shape[0]
        lead = jnp.concatenate([jnp.zeros((PAD, D_MODEL), F32), meta], axis=0)
        parts.append(jnp.concatenate([jnp.broadcast_to(lead[None], (bsz, FRONT, D_MODEL)), x], axis=1)
                     .reshape(-1, D_MODEL))
    h = _to_flat(parts, n_rows, D_MODEL, F32)

    bs = _block_avg(HGROUP, HEAD_DIM)
    row = lambda t: t.reshape(1, -1).astype(F32)

    depth = p['pre_mix_g'].shape[0]
    for i in range(depth):
        if i % 2 == 0:
            e = i // 2
            cos, sa, sb = _rope_tables(segments, n_rows, "axial")
            ch = np.arange(B_DIM)
            ang = 2.0 * np.pi * ((ch[:, None] * ch[None, :]) % B_DIM) / B_DIM
            cc, sc = jnp.asarray(np.cos(ang), F32), jnp.asarray(np.sin(ang), F32)
            wl = p['b_w'][e].astype(F32)
            hp = lax.Precision.HIGHEST
            pmat = jnp.einsum('cd,gde->gce', cc, wl, precision=hp)
            qmat = -jnp.einsum('cd,gde->gce', sc, wl, precision=hp)
            bd = lambda m: jax.scipy.linalg.block_diag(*[m[g] for g in range(B_GROUPS)])
            pq_base = jnp.concatenate([bd(pmat), bd(qmat)], axis=1)
            qT, k, vT, gf = _row_call(
                "in_even", _in_even_kernel, n_rows, tm, [h, cos, sa, sb],
                [row(p['pre_mix_g'][i]), p['even_w_in'][e].astype(BF16),
                 row(jnp.tile(p['a_q_gain'][e], A_HEADS) * (HEAD_DIM ** -0.5 * LOG2_E)),
                 row(jnp.tile(p['a_k_gain'][e], A_KV_HEADS)), row(p['b_norm_g'][e]), bs,
                 pq_base.astype(BF16)],
                [("cols", A_Q, BF16), ("rows", A_KV, BF16), ("chunks", A_KV, BF16), ("rows", 2 * B_W, F32)])
            ya_parts, yb_parts = [], []
            for seg in segments:
                _, bsz, lp = seg
                ya_parts.append(_attn_a(*_attn_operands(qT, k, vT, seg)))
                gseq = _seq_view(gf, seg, 2 * B_W)[:, PAD:]
                yb = _mixer_b(gseq, row(p['b_b'][e]))
                yb_parts.append(jnp.pad(yb, ((0, 0), (PAD, 0), (0, 0))).reshape(bsz * lp, B_W).astype(BF16))
            yat = _to_flat_cols(ya_parts, n_rows)
            mix_kernel, mix_rows, mix_consts = _mix_ffn_even_kernel, [_to_flat(yb_parts, n_rows, B_W, BF16)], []
            w_out = p['even_w_out'][e].astype(BF16)
            wa, wb = w_out[:A_Q], w_out[A_Q:]
        else:
            o = i // 2
            cos, sa, sb = _rope_tables(segments, n_rows, "partial")
            qT, k, vT, u = _row_call(
                "in_odd", _in_odd_kernel, n_rows, tm, [h, cos, sa, sb],
                [row(p['pre_mix_g'][i]), p['odd_w_in'][o].astype(BF16)],
                [("cols", C_Q, BF16), ("rows", C_KV, BF16), ("chunks", C_KV, BF16), ("rows", D_IN, F32)])
            sink = p['c_sink'][o].astype(F32)
            yat = _to_flat_cols([_attn_c(sink, *_attn_operands(qT, k, vT, seg)) for seg in segments], n_rows)
            mix_kernel = _mix_ffn_odd_kernel
            mix_rows, mix_consts = _mixer_d(u, p, o, bs, segments, n_rows, tm)
            w_out = p['odd_w_out'][o].astype(BF16)
            wa, wb = w_out[:C_Q], w_out[C_Q:]
        (h,) = _row_call(
            "mix_ffn", functools.partial(mix_kernel, chunk=256), n_rows, tm, [h] + mix_rows,
            mix_consts + [wa, wb, row(p['post_mix_g'][i]),
             row(p['pre_ffn_g'][i]), p['ffn_w_gate'][i].astype(BF16), p['ffn_w_up'][i].astype(BF16),
             p['ffn_w_down'][i].astype(BF16), row(p['post_ffn_g'][i])],
            [("rows", D_MODEL, F32)], scratch=[pltpu.VMEM((tm, D_MODEL), F32)], col_ins=[yat])

    outs = []
    for seg in segments:
        outs.append(_seq_view(h, seg, D_MODEL)[:, FRONT:])
    return tuple(outs)


def _mixer_d(u, p, o, bs, segments, n_rows, tm):
    row = lambda t: t.reshape(1, -1).astype(F32)
    w = D_WIDTH
    zeros = lambda r: jnp.zeros((r, w), F32)
    w2 = jnp.concatenate([jnp.concatenate([p['d_w_up'][o][0], zeros(DECAY_RANK)], axis=1),
                          jnp.concatenate([zeros(DECAY_RANK), p['d_w_up'][o][1]], axis=1)], axis=0)
    a2 = jnp.concatenate([jnp.concatenate([p['d_a_up'][o][0], zeros(ICLR_RANK)], axis=1),
                          jnp.concatenate([zeros(ICLR_RANK), p['d_a_up'][o][1]], axis=1)], axis=0)
    consts = [row(p['d_mu_prev'][o]), row(p['d_mu_next'][o]), w2.astype(BF16), a2.astype(BF16),
              p['d_g_up'][o].astype(BF16), row(p['d_w0'][o]), row(p['d_a0'][o]),
              row(p['d_k_k'][o]), row(p['d_k_a'][o]), row(p['d_r_k'][o]), bs]
    nb8 = n_rows // 8
    t8 = tm // 8
    in_specs = [pl.BlockSpec((tm, D_IN), lambda i: (i, 0)),
                pl.BlockSpec((8, D_IN), lambda i: (jnp.maximum(i * t8 - 1, 0), 0)),
                pl.BlockSpec((8, D_IN), lambda i: (jnp.minimum((i + 1) * t8, nb8 - 1), 0))]
    in_specs += [pl.BlockSpec(a.shape, lambda i, nd=a.ndim: (0,) * nd) for a in consts]
    out_dtypes = [BF16, BF16, BF16, F32, F32, BF16, BF16, BF16, BF16, F32, BF16]
    prep = pl.pallas_call(
        functools.partial(_d_prep_kernel, tm=tm, segments=tuple(segments)),
        grid=(n_rows // tm,), in_specs=in_specs,
        out_specs=[pl.BlockSpec((tm, w), lambda i: (i, 0))] * len(out_dtypes),
        out_shape=[jax.ShapeDtypeStruct((n_rows, w), dt) for dt in out_dtypes],
        compiler_params=_cparams(("parallel",)), name="d_prep",
    )(u, u, u, *consts)
    r, v, a, lwf, lwb, kdf, kdb, bf_, bb_, bonus, g = prep
    of, ob = _scan(segments, n_rows, r, v, a, lwf, lwb, kdf, kdb, bf_, bb_)
    return [of, ob, bonus, g], [row(p['d_ln_g'][o]), row(p['d_ln_b'][o]), bs]


def kernel(x_prompt, x_sample, meta_tokens, pre_mix_g, post_mix_g, pre_ffn_g, post_ffn_g, even_w_in, even_w_out, a_q_gain, a_k_gain, b_norm_g, b_w, b_b, odd_w_in, odd_w_out, c_sink, d_mu_prev, d_mu_next, d_w0, d_w_up, d_a0, d_a_up, d_g_up, d_k_k, d_k_a, d_r_k, d_ln_g, d_ln_b, ffn_w_gate, ffn_w_up, ffn_w_down):
    params = dict(meta_tokens=meta_tokens, pre_mix_g=pre_mix_g, post_mix_g=post_mix_g,
                  pre_ffn_g=pre_ffn_g, post_ffn_g=post_ffn_g,
                  even_w_in=even_w_in, even_w_out=even_w_out, a_q_gain=a_q_gain, a_k_gain=a_k_gain,
                  b_norm_g=b_norm_g, b_w=b_w, b_b=b_b,
                  odd_w_in=odd_w_in, odd_w_out=odd_w_out, c_sink=c_sink,
                  d_mu_prev=d_mu_prev, d_mu_next=d_mu_next, d_w0=d_w0, d_w_up=d_w_up,
                  d_a0=d_a0, d_a_up=d_a_up, d_g_up=d_g_up, d_k_k=d_k_k, d_k_a=d_k_a, d_r_k=d_r_k,
                  d_ln_g=d_ln_g, d_ln_b=d_ln_b,
                  ffn_w_gate=ffn_w_gate, ffn_w_up=ffn_w_up, ffn_w_down=ffn_w_down)
    return _forward([x_prompt, x_sample], params)
```

```python
import functools
import math

import numpy as np
import jax
import jax.numpy as jnp
from jax import lax
from jax.experimental import pallas as pl
from jax.experimental.pallas import tpu as pltpu

F32 = jnp.float32
BF16 = jnp.bfloat16

D_MODEL = 1024
HEAD_DIM = 64
N_META = 16
GRID_W = 64
WINDOW = 128
RMS_EPS = 1e-6
A_HEADS, A_KV_HEADS, A_THETA = 12, 4, 10000.0
B_GROUPS, B_DIM = 4, 64
C_HEADS, C_KV_HEADS = 8, 2
ROPE_THETA = 500000.0
ROPE_DIMS = HEAD_DIM // 4
D_HEADS = 8
D_WIDTH = D_HEADS * HEAD_DIM
DECAY_RANK, ICLR_RANK, GATE_RANK = 64, 64, 128
LNX_EPS = 64e-5
D_FF = 2816
A_Q, A_KV, B_W = A_HEADS * HEAD_DIM, A_KV_HEADS * HEAD_DIM, B_GROUPS * B_DIM
EVEN_IN = A_Q + 2 * A_KV + B_W
C_Q, C_KV = C_HEADS * HEAD_DIM, C_KV_HEADS * HEAD_DIM
D_IN = 3 * D_WIDTH + 2 * DECAY_RANK + 2 * ICLR_RANK + GATE_RANK
ODD_IN = C_Q + 2 * C_KV + D_IN

LANES = 128
FRONT = 256
PAD = FRONT - N_META
NEG = -1e30
SCAN_CHUNK = 64
SCAN_CHUNKS_PER_STEP = 4
HGROUP = 256
FFT_L1 = 16
VMEM_LIMIT = 56 * 1024 * 1024
WIN_QBLOCKS = 1
ROW_SUBTILE = 256
ATTN_CHUNK = 256
ATTN_MAX_CHUNKS = 20
ATTN_UNROLL = 16
LOG2_E = math.log2(math.e)


def _round_up(x, m):
    return (x + m - 1) // m * m


def _cparams(sem):
    return pltpu.CompilerParams(dimension_semantics=sem, vmem_limit_bytes=VMEM_LIMIT)


def _dot(a, b):
    return jnp.dot(a, b, preferred_element_type=F32)


def _dot_nt(a, b):
    return lax.dot_general(a, b, (((1,), (1,)), ((), ())), preferred_element_type=F32)


def _dot_tn(a, b):
    return lax.dot_general(a, b, (((0,), (0,)), ((), ())), preferred_element_type=F32)


def _split2(x):
    hi = x.astype(BF16)
    lo = (x - hi.astype(F32)).astype(BF16)
    return hi, lo


def _group_mean(x, bs):
    hi, lo = _split2(x)
    return _dot(hi, bs) + _dot(lo, bs)


def _rope(x, cos, sin_a, sin_b, shift):
    n = x.shape[1]
    return x * cos + pltpu.roll(x, n - shift, 1) * sin_a + pltpu.roll(x, shift, 1) * sin_b


def _wide(t, n):
    return t if n == LANES else jnp.concatenate([t] * (n // LANES), axis=1)


def _rms_rows(x, g):
    ms = jnp.mean(x * x, axis=-1, keepdims=True)
    return x * lax.rsqrt(ms + RMS_EPS) * g


def _in_even_kernel(h_ref, cos_ref, sa_ref, sb_ref, g_ref, w_ref, qg_ref, kg_ref, bg_ref, bs_ref, pq_ref,
                    q_ref, k_ref, v_ref, gf_ref):
    bs = bs_ref[...]
    for r in range(h_ref.shape[0] // ROW_SUBTILE):
        rows = slice(r * ROW_SUBTILE, (r + 1) * ROW_SUBTILE)
        hn = _rms_rows(h_ref[rows, :], g_ref[...]).astype(BF16)
        proj = _dot(hn, w_ref[...])
        cos, sa, sb = (_wide(t[rows, :], HGROUP) for t in (cos_ref, sa_ref, sb_ref))

        def norm_rope(x, gain):
            xn = x * lax.rsqrt(_group_mean(x * x, bs) + RMS_EPS) * gain
            return _rope(xn, cos, sa, sb, 16)

        for s in range(A_Q // HGROUP):
            cs = slice(s * HGROUP, (s + 1) * HGROUP)
            q_ref[cs, rows] = norm_rope(proj[:, cs], qg_ref[:, cs]).T.astype(BF16)
        k_ref[rows, :] = norm_rope(proj[:, A_Q:A_Q + A_KV], kg_ref[...]).astype(BF16)
        v_ref[r] = proj[:, A_Q + A_KV:A_Q + 2 * A_KV].T.astype(BF16)
        f = proj[:, A_Q + 2 * A_KV:]
        fn = f * lax.rsqrt(_group_mean(f * f, bs) + RMS_EPS) * bg_ref[...]
        gf_ref[rows, :] = _dot(fn.astype(BF16), pq_ref[...])


def _in_odd_kernel(h_ref, cos_ref, sa_ref, sb_ref, g_ref, w_ref, q_ref, k_ref, v_ref, u_ref):
    for r in range(h_ref.shape[0] // ROW_SUBTILE):
        rows = slice(r * ROW_SUBTILE, (r + 1) * ROW_SUBTILE)
        hn = _rms_rows(h_ref[rows, :], g_ref[...]).astype(BF16)
        proj = _dot(hn, w_ref[...])
        cos, sa, sb = cos_ref[rows, :], sa_ref[rows, :], sb_ref[rows, :]
        cos2, sa2, sb2 = (_wide(t, HGROUP) for t in (cos, sa, sb))
        for s in range(C_Q // HGROUP):
            cs = slice(s * HGROUP, (s + 1) * HGROUP)
            q_ref[cs, rows] = (_rope(proj[:, cs], cos2, sa2, sb2, 8)
                               * (HEAD_DIM ** -0.5 * LOG2_E)).T.astype(BF16)
        k_ref[rows, :] = _rope(proj[:, C_Q:C_Q + C_KV], cos, sa, sb, 8).astype(BF16)
        v = proj[:, C_Q + C_KV:C_Q + 2 * C_KV]
        for c in range(ROW_SUBTILE // C_KV):
            v_ref[r * (ROW_SUBTILE // C_KV) + c] = v[c * C_KV:(c + 1) * C_KV, :].T.astype(BF16)
        u_ref[rows, :] = proj[:, C_Q + 2 * C_KV:]


def _mix_ffn_even_kernel(h_ref, yb_ref, *rest, chunk):
    _mix_ffn(h_ref, yb_ref[...], *rest, chunk=chunk)


def _mix_ffn_odd_kernel(h_ref, of_ref, ob_ref, bonus_ref, gate_ref, lng_ref, lnb_ref, bs_ref, *rest, chunk):
    o = of_ref[...] + ob_ref[...]
    bs = bs_ref[...]

    def head_mean(x):
        return jnp.concatenate(
            [_group_mean(x[:, s * HGROUP:(s + 1) * HGROUP], bs) for s in range(D_WIDTH // HGROUP)], axis=1)

    cen = o - head_mean(o)
    var = head_mean(cen * cen)
    y = cen * lax.rsqrt(var + LNX_EPS) * lng_ref[...] + lnb_ref[...] + bonus_ref[...]
    _mix_ffn(h_ref, (y * gate_ref[...]).astype(BF16), *rest, chunk=chunk)


def _mix_ffn(h_ref, yb, wa_ref, wb_ref, gm_ref, g1_ref, wg_ref, wu_ref, wd_ref, g2_ref,
             yat_ref, o_ref, acc_ref, *, chunk):
    mix = _dot_tn(yat_ref[...], wa_ref[...]) + _dot(yb, wb_ref[...])
    h = h_ref[...] + _rms_rows(mix, gm_ref[...])
    hn = _rms_rows(h, g1_ref[...]).astype(BF16)
    for c in range(D_FF // chunk):
        cs = slice(c * chunk, (c + 1) * chunk)
        gate = _dot(hn, wg_ref[:, cs])
        up = _dot(hn, wu_ref[:, cs])
        act = (gate * (1.0 / (1.0 + jnp.exp(-gate))) * up).astype(BF16)
        part = _dot(act, wd_ref[cs, :])
        if c == 0:
            acc_ref[...] = part
        else:
            acc_ref[...] += part
    o_ref[...] = h + _rms_rows(acc_ref[...], g2_ref[...])


def _row_call(name, kernel, n_rows, tm, row_ins, const_ins, outs, scratch=(), col_ins=()):
    grid = (n_rows // tm,)
    in_specs = [pl.BlockSpec((tm, a.shape[1]), lambda i: (i, 0)) for a in row_ins]
    in_specs += [pl.BlockSpec(a.shape, lambda i, nd=a.ndim: (0,) * nd, pipeline_mode=pl.Buffered(1))
                 for a in const_ins]
    in_specs += [pl.BlockSpec((a.shape[0], tm), lambda i: (0, i)) for a in col_ins]
    out_specs, out_shape = [], []
    for kind, n, dt in outs:
        if kind == "rows":
            out_specs.append(pl.BlockSpec((tm, n), lambda i: (i, 0)))
            out_shape.append(jax.ShapeDtypeStruct((n_rows, n), dt))
        elif kind == "cols":
            out_specs.append(pl.BlockSpec((n, tm), lambda i: (0, i)))
            out_shape.append(jax.ShapeDtypeStruct((n, n_rows), dt))
        else:
            out_specs.append(pl.BlockSpec((tm // n, n, n), lambda i: (i, 0, 0)))
            out_shape.append(jax.ShapeDtypeStruct((n_rows // n, n, n), dt))
    return pl.pallas_call(
        kernel, grid=grid, in_specs=in_specs, out_specs=out_specs, out_shape=out_shape,
        scratch_shapes=list(scratch), compiler_params=_cparams(("parallel",)), name=name,
    )(*row_ins, *const_ins, *col_ins)


def _padded_q(q_ref, h, kv):
    qh = q_ref[h * HEAD_DIM:(h + 1) * HEAD_DIM, :]
    zero = jnp.zeros_like(qh)
    return jnp.concatenate([qh, zero] if kv % 2 == 0 else [zero, qh], axis=0)


def _next_step(b, i, j, bsz, nq, nkb):
    j1 = j + 1
    wrap_j = (j1 == nkb).astype(jnp.int32)
    i1 = i + wrap_j
    wrap_i = (i1 == nq).astype(jnp.int32)
    return jnp.minimum(b + wrap_i, bsz - 1), i1 * (1 - wrap_i), j1 * (1 - wrap_j)


def _attn_a_kernel(qT_ref, qn_ref, k_ref, kn_ref, vT_ref, o_ref, m_scr, l_scr, acc_scr, s_scr, mx_scr,
                   *, nchunk, nkb, nq, bsz, tq):
    b, i, j = pl.program_id(0), pl.program_id(1), pl.program_id(2)

    @pl.when(j == 0)
    def _():
        m_scr[...] = jnp.full(m_scr.shape, NEG, F32)
        l_scr[...] = jnp.zeros(l_scr.shape, F32)
        acc_scr[...] = jnp.zeros(acc_scr.shape, F32)

    group = A_HEADS // A_KV_HEADS
    ck = ATTN_CHUNK
    row_iota = lax.broadcasted_iota(jnp.int32, (ck, tq), 0)
    key_row = row_iota + j * (nchunk * ck)
    key_row_next = row_iota + _next_step(b, i, j, bsz, nq, nkb)[2] * (nchunk * ck)

    def fold(s, op):
        return op(s.reshape(ck // 8, 8, tq), axis=0)

    def gang_q(g):
        if g == A_KV_HEADS:
            return [_padded_q(qn_ref, t, 0) for t in range(group)]
        return [_padded_q(qT_ref, g * group + t, g) for t in range(group)]

    def gang_scores(g, qps, c, mxs, first=False):
        rows = pl.ds(pl.multiple_of(c * ck, ck), ck)
        if g == A_KV_HEADS:
            kc, krow = kn_ref[0, rows, :], key_row_next
        else:
            kc, krow = k_ref[0, rows, (g // 2) * LANES:(g // 2 + 1) * LANES], key_row
        out = []
        for t in range(group):
            s = _dot(kc, qps[t])
            if first:
                s = jnp.where(krow >= PAD, s, NEG)
            s_scr[g % 2, t, c] = s
            out.append(fold(s, jnp.max) if mxs is None else jnp.maximum(mxs[t], fold(s, jnp.max)))
        return tuple(out)

    def gang_values(g, c, m_news):
        vc = vT_ref[0, c, g * HEAD_DIM:(g + 1) * HEAD_DIM, :]
        ps = [jnp.exp2(s_scr[g % 2, t, c] - m_news[t]) for t in range(group)]
        return tuple(_dot(vc, p.astype(BF16)) for p in ps), tuple(fold(p, jnp.sum) for p in ps)

    def add(xs, ys):
        return tuple(x + y for x, y in zip(xs, ys))

    unroll = max(u for u in range(1, ATTN_UNROLL + 1) if max(nchunk - 1, 1) % u == 0)

    @pl.when((b == 0) & (i == 0) & (j == 0))
    def _():
        qps = gang_q(0)
        mxs = lax.fori_loop(1, nchunk, lambda c, mxs: gang_scores(0, qps, c, mxs),
                            gang_scores(0, qps, 0, None, first=True), unroll=unroll)
        for t in range(group):
            mx_scr[t] = mxs[t]

    mxs = tuple(mx_scr[t] for t in range(group))
    for g in range(A_KV_HEADS):
        heads = [g * group + t for t in range(group)]
        m_olds = [m_scr[h] for h in heads]
        m_news = [jnp.maximum(mo, jnp.max(mx, axis=0, keepdims=True)) for mo, mx in zip(m_olds, mxs)]
        qps = gang_q(g + 1)

        def body(c, carry, g=g, m_news=m_news, qps=qps):
            accs, lsums, mxn = carry
            mxn = gang_scores(g + 1, qps, c, mxn)
            pvs, pss = gang_values(g, c, m_news)
            return add(accs, pvs), add(lsums, pss), mxn

        mx0 = gang_scores(g + 1, qps, 0, None, first=True)
        accs, lsums, mxs = lax.fori_loop(1, nchunk, body, (*gang_values(g, 0, m_news), mx0), unroll=unroll)
        for t, h in enumerate(heads):
            alpha = jnp.exp2(m_olds[t] - m_news[t])
            acc_scr[h] = acc_scr[h] * alpha + accs[t]
            l_scr[h] = l_scr[h] * alpha + jnp.sum(lsums[t], axis=0, keepdims=True)
            m_scr[h] = m_news[t]
    for t in range(group):
        mx_scr[t] = mxs[t]

    @pl.when(j == nkb - 1)
    def _():
        for h in range(A_HEADS):
            o_ref[h * HEAD_DIM:(h + 1) * HEAD_DIM, :] = (acc_scr[h] / l_scr[h]).astype(o_ref.dtype)


def _attn_a(qT, k, vT, start):
    bsz, lp = k.shape[:2]
    off = start // ATTN_CHUNK
    tq = ATTN_CHUNK
    total = lp // ATTN_CHUNK
    nkb = min(n for n in range(1, total + 1) if total % n == 0 and total // n <= ATTN_MAX_CHUNKS)
    tk = lp // nkb
    nchunk = tk // ATTN_CHUNK
    nq = lp // tq
    group = A_HEADS // A_KV_HEADS
    kernel = functools.partial(_attn_a_kernel, nchunk=nchunk, nkb=nkb, nq=nq, bsz=bsz, tq=tq)

    def next_q(b, i, j):
        b2, i2, _ = _next_step(b, i, j, bsz, nq, nkb)
        return 0, off + b2 * nq + i2

    def next_k(b, i, j):
        b2, _, j2 = _next_step(b, i, j, bsz, nq, nkb)
        return b2, j2, 0

    return pl.pallas_call(
        kernel, grid=(bsz, nq, nkb),
        in_specs=[
            pl.BlockSpec((A_Q, tq), lambda b, i, j: (0, off + b * nq + i)),
            pl.BlockSpec((group * HEAD_DIM, tq), next_q),
            pl.BlockSpec((1, tk, A_KV), lambda b, i, j: (b, j, 0)),
            pl.BlockSpec((1, tk, LANES), next_k),
            pl.BlockSpec((1, nchunk, A_KV, ATTN_CHUNK), lambda b, i, j: (b, j, 0, 0)),
        ],
        out_specs=pl.BlockSpec((A_Q, tq), lambda b, i, j: (0, b * nq + i)),
        out_shape=jax.ShapeDtypeStruct((A_Q, bsz * lp), BF16),
        scratch_shapes=[
            pltpu.VMEM((A_HEADS, 1, tq), F32),
            pltpu.VMEM((A_HEADS, 1, tq), F32),
            pltpu.VMEM((A_HEADS, HEAD_DIM, tq), F32),
            pltpu.VMEM((2, group, nchunk, ATTN_CHUNK, tq), F32),
            pltpu.VMEM((group, 8, tq), F32),
        ],
        compiler_params=_cparams(("arbitrary", "arbitrary", "arbitrary")), name="attn_a",
    )(qT, qT, k, k, vT)


def _attn_c_kernel(sink_ref, qT_ref, *refs, lp, nslot):
    j = pl.program_id(1)
    k_refs, v_refs, o_ref = refs[:nslot], refs[nslot:2 * nslot], refs[2 * nslot]
    tq = WIN_QBLOCKS * LANES
    row = lax.broadcasted_iota(jnp.int32, (LANES, tq), 0)
    pq = lax.broadcasted_iota(jnp.int32, (LANES, tq), 1) + j * tq
    biases = []
    for slot in range(nslot):
        if slot == 0:
            bias = jnp.where(row >= PAD % LANES, 0.0, NEG)
        else:
            pk = row + (j * WIN_QBLOCKS + slot - 2) * LANES
            in_window = jnp.where(jnp.abs(pq - pk) <= WINDOW, 0.0, NEG)
            bias = jnp.where(pk >= FRONT, jnp.where(pk < lp, in_window, NEG), NEG)
        biases.append(bias.astype(F32))
    group = C_HEADS // C_KV_HEADS
    scores = []
    for h in range(C_HEADS):
        qp = _padded_q(qT_ref, h, h // group)
        scores.append([_dot(k_refs[t][0], qp) + biases[t] for t in range(nslot)])
    for h in range(C_HEADS):
        kv = h // group
        sink = sink_ref[h] * LOG2_E
        ss = scores[h]
        m = functools.reduce(jnp.maximum, ss)
        m = jnp.maximum(jnp.max(m, axis=0, keepdims=True), sink)
        acc = jnp.zeros((HEAD_DIM, tq), F32)
        denom = jnp.exp2(sink - m)
        for t in range(nslot):
            p = jnp.exp2(ss[t] - m)
            acc = acc + _dot(v_refs[t][0, 0, kv * HEAD_DIM:(kv + 1) * HEAD_DIM, :], p.astype(BF16))
            denom = denom + jnp.sum(p, axis=0, keepdims=True)
        o_ref[h * HEAD_DIM:(h + 1) * HEAD_DIM, :] = (acc / denom).astype(o_ref.dtype)


def _attn_c(sink, qT, k, vT, start):
    bsz, lp = k.shape[:2]
    nb = lp // LANES
    tq = WIN_QBLOCKS * LANES
    nq = lp // tq
    off = start // tq
    nslot = WIN_QBLOCKS + 3
    kernel = functools.partial(_attn_c_kernel, lp=lp, nslot=nslot)

    def kspec(fn):
        return pl.BlockSpec((1, LANES, C_KV), lambda b, j: (b, fn(j), 0))

    def vspec(fn):
        return pl.BlockSpec((1, 1, C_KV, LANES), lambda b, j: (b, fn(j), 0, 0))

    def neighbour(s):
        return lambda j: jnp.clip(j * WIN_QBLOCKS + s - 2, 0, nb - 1)

    fns = [lambda j: PAD // LANES] + [neighbour(s) for s in range(1, nslot)]
    return pl.pallas_call(
        kernel, grid=(bsz, nq),
        in_specs=[pl.BlockSpec(memory_space=pltpu.SMEM),
                  pl.BlockSpec((C_Q, tq), lambda b, j: (0, off + b * nq + j))]
                 + [kspec(f) for f in fns] + [vspec(f) for f in fns],
        out_specs=pl.BlockSpec((C_Q, tq), lambda b, j: (0, b * nq + j)),
        out_shape=jax.ShapeDtypeStruct((C_Q, bsz * lp), BF16),
        compiler_params=_cparams(("parallel", "parallel")), name="attn_c",
    )(sink, qT, *([k] * nslot), *([vT] * nslot))


def _fft16(vals):
    n = FFT_L1
    bits = n.bit_length() - 1
    a = [vals[int(format(idx, '0%db' % bits)[::-1], 2)] for idx in range(n)]
    m = 2
    while m <= n:
        half = m // 2
        for base in range(0, n, m):
            for t in range(half):
                ur, ui = a[base + t]
                vr, vi = a[base + t + half]
                if 4 * t == m:
                    tr, ti = vi, -vr
                elif t > 0:
                    wr, wi = math.cos(-2.0 * math.pi * t / m), math.sin(-2.0 * math.pi * t / m)
                    tr, ti = vr * wr - vi * wi, vr * wi + vi * wr
                else:
                    tr, ti = vr, vi
                a[base + t] = (ur + tr, ui + ti)
                a[base + t + half] = (ur - tr, ui - ti)
        m *= 2
    return a


def _fft1_kernel(x_ref, o_ref):
    def strip(s, carry):
        rows = pl.ds(pl.multiple_of(s * 8, 8), 8)
        for q in range(B_W // LANES):
            re_cols = slice(q * LANES, (q + 1) * LANES)
            im_cols = slice(B_W + q * LANES, B_W + (q + 1) * LANES)
            vals = [(x_ref[0, n, rows, re_cols], x_ref[0, n, rows, im_cols]) for n in range(FFT_L1)]
            for kk, (re, im) in enumerate(_fft16(vals)):
                o_ref[0, kk, rows, re_cols] = re
                o_ref[0, kk, rows, im_cols] = im
        return carry

    lax.fori_loop(0, x_ref.shape[2] // 8, strip, 0)


def _fft2_kernel(a_ref, tc_ref, ts_ref, c2_ref, s2_ref, bias_ref, o_ref):
    a = a_ref[0, 0]
    tc, ts = _wide(tc_ref[0], B_W), _wide(ts_ref[0], B_W)
    are, aim = a[:, :B_W], a[:, B_W:]
    bre = (are * tc + aim * ts).astype(BF16)
    bim = (aim * tc - are * ts).astype(BF16)
    o_ref[0] = _dot(c2_ref[...], bre) + _dot(s2_ref[...], bim) + bias_ref[...]


def _dft_tables(l2, l2p, l2o):
    l1 = FFT_L1
    length = l1 * l2
    n1 = np.arange(l1)
    n2 = np.arange(l2)
    angt = 2.0 * np.pi * (n1[:, None] * n2[None, :]) / length
    tc = np.zeros((l1, l2p, LANES), np.float32)
    ts = np.zeros((l1, l2p, LANES), np.float32)
    tc[:, :l2, :] = np.cos(angt)[:, :, None]
    ts[:, :l2, :] = np.sin(angt)[:, :, None]
    ang2 = 2.0 * np.pi * ((n2[:, None] * n2[None, :]) % l2) / l2
    scale = 1.0 / math.sqrt(B_DIM * length)
    c2 = np.zeros((l2o, l2p), np.float32)
    s2 = np.zeros((l2o, l2p), np.float32)
    c2[:l2, :l2] = np.cos(ang2) * scale
    s2[:l2, :l2] = np.sin(ang2) * scale
    as_bf = lambda t: jnp.asarray(t, F32).astype(BF16)
    return jnp.asarray(tc), jnp.asarray(ts), as_bf(c2), as_bf(s2)


def _mixer_b(gseq, bias):
    bsz, length, _ = gseq.shape
    l1 = FFT_L1
    l2 = length // l1
    l2p, l2o = _round_up(l2, LANES), _round_up(l2, 8)
    tc, ts, c2, s2 = _dft_tables(l2, l2p, l2o)
    x = jnp.pad(gseq.reshape(bsz, l1, l2, 2 * B_W), ((0, 0), (0, 0), (0, l2p - l2), (0, 0)))
    blk = (1, l1, LANES, 2 * B_W)
    a = pl.pallas_call(
        _fft1_kernel, grid=(bsz, l2p // LANES),
        in_specs=[pl.BlockSpec(blk, lambda b, i: (b, 0, i, 0))],
        out_specs=pl.BlockSpec(blk, lambda b, i: (b, 0, i, 0)),
        out_shape=jax.ShapeDtypeStruct((bsz, l1, l2p, 2 * B_W), F32),
        compiler_params=_cparams(("parallel", "parallel")), name="fft1",
    )(x)
    y = pl.pallas_call(
        _fft2_kernel, grid=(bsz, l1),
        in_specs=[pl.BlockSpec((1, 1, l2p, 2 * B_W), lambda b, k: (b, k, 0, 0)),
                  pl.BlockSpec((1, l2p, LANES), lambda b, k: (k, 0, 0)),
                  pl.BlockSpec((1, l2p, LANES), lambda b, k: (k, 0, 0)),
                  pl.BlockSpec((l2o, l2p), lambda b, k: (0, 0)),
                  pl.BlockSpec((l2o, l2p), lambda b, k: (0, 0)),
                  pl.BlockSpec((1, B_W), lambda b, k: (0, 0))],
        out_specs=pl.BlockSpec((1, l2o, B_W), lambda b, k: (b, 0, k)),
        out_shape=jax.ShapeDtypeStruct((bsz, l2o, l1 * B_W), F32),
        compiler_params=_cparams(("parallel", "parallel")), name="fft2",
    )(a, tc, ts, c2, s2, bias)
    return y[:, :l2].reshape(bsz, length, B_W)


def _seq_position(rows, segments):
    pos = jnp.full(rows.shape, -1.0, F32)
    seqlen = jnp.full(rows.shape, 1.0, F32)
    for start, bsz, lp in segments:
        rel = rows - float(start)
        q = jnp.floor((rel + 0.5) * (1.0 / lp))
        inside = jnp.where(rel >= 0.0, jnp.where(rel < float(bsz * lp), 1.0, 0.0), 0.0) > 0.5
        pos = jnp.where(inside, rel - q * lp, pos)
        seqlen = jnp.where(inside, float(lp), seqlen)
    return pos, seqlen


def _sigmoid(x):
    return 1.0 / (1.0 + jnp.exp(-x))


def _d_prep_kernel(u_ref, up_ref, un_ref, mup_ref, mun_ref, w2_ref, a2_ref, gup_ref, w0_ref, a0_ref,
                   kk_ref, ka_ref, rk_ref, bs_ref,
                   r_ref, v_ref, a_ref, lwf_ref, lwb_ref, kdf_ref, kdb_ref, bf_ref, bb_ref,
                   bonus_ref, g_ref, *, tm, segments):
    i = pl.program_id(0)
    u = u_ref[...]
    rows = (lax.broadcasted_iota(jnp.int32, (tm, 1), 0) + i * tm).astype(F32)
    pos, seqlen = _seq_position(rows, segments)
    edge = lax.broadcasted_iota(jnp.int32, (8, 1), 0)
    u_prev = pltpu.roll(u, 1, 0)
    u_prev = jnp.concatenate([jnp.where(edge == 0, up_ref[7:8, :], u_prev[:8]), u_prev[8:]], axis=0)
    u_next = pltpu.roll(u, tm - 1, 0)
    u_next = jnp.concatenate([u_next[:tm - 8], jnp.where(edge == 7, un_ref[0:1, :], u_next[tm - 8:])], axis=0)
    u_prev = jnp.where(pos == float(PAD), 0.0, u_prev)
    u_next = jnp.where(pos == seqlen - 1.0, 0.0, u_next)
    u = u + mup_ref[...] * (u_prev - u) + mun_ref[...] * (u_next - u)
    valid = jnp.where(pos >= float(PAD), 1.0, 0.0)

    w = D_WIDTH
    r, k, v = u[:, :w], u[:, w:2 * w], u[:, 2 * w:3 * w]
    c0 = 3 * w
    dec = _dot(jnp.tanh(u[:, c0:c0 + 2 * DECAY_RANK]).astype(BF16), w2_ref[...]) + w0_ref[...]
    c0 += 2 * DECAY_RANK
    icl = _dot(u[:, c0:c0 + 2 * ICLR_RANK].astype(BF16), a2_ref[...]) + a0_ref[...]
    c0 += 2 * ICLR_RANK
    g_ref[...] = _dot(_sigmoid(u[:, c0:c0 + GATE_RANK]).astype(BF16), gup_ref[...]).astype(g_ref.dtype)

    bs = bs_ref[...]

    def head_sum(x):
        xb = x.astype(BF16)
        return jnp.concatenate(
            [_dot(xb[:, s * HGROUP:(s + 1) * HGROUP], bs) for s in range(w // HGROUP)], axis=1
        ) * float(HEAD_DIM)

    kk = k * kk_ref[...]
    kk = kk * lax.rsqrt(jnp.maximum(head_sum(kk * kk), 1e-24))
    r_ref[...] = r.astype(r_ref.dtype)
    v_ref[...] = v.astype(v_ref.dtype)
    a_ref[...] = (-kk * valid).astype(a_ref.dtype)
    bonus = jnp.zeros_like(r)
    for d, (lw_ref, kd_ref, b_ref) in enumerate(((lwf_ref, kdf_ref, bf_ref), (lwb_ref, kdb_ref, bb_ref))):
        x = -dec[:, d * w:(d + 1) * w]
        softplus = jnp.maximum(x, 0.0) + jnp.log(1.0 + jnp.exp(-jnp.abs(x)))
        lw_ref[...] = -jnp.exp(-softplus - 0.5)
        gate = _sigmoid(icl[:, d * w:(d + 1) * w])
        kd = k * (1.0 + (gate - 1.0) * ka_ref[...])
        kd_ref[...] = (kd * valid).astype(kd_ref.dtype)
        b_ref[...] = (kk * gate * valid).astype(b_ref.dtype)
        bonus = bonus + head_sum(r * kd * rk_ref[...]) * v
    bonus_ref[...] = bonus


def _scan_kernel(fwd_blk, bwd_blk, first,
                 rf, vf, af, lwf, kdf, bf_, rb, vb, ab, lwb, kdb, bb_,
                 of_ref, ob_ref, s_scr, *, nk):
    del fwd_blk, bwd_blk
    step = pl.program_id(0)
    c = SCAN_CHUNK

    @pl.when(first[step] == 1)
    def _():
        s_scr[...] = jnp.zeros(s_scr.shape, F32)

    t_sq = lax.broadcasted_iota(jnp.int32, (nk * c, nk * c), 0)
    s_sq = lax.broadcasted_iota(jnp.int32, (nk * c, nk * c), 1)
    t_cat = lax.broadcasted_iota(jnp.int32, (c, HGROUP), 0)
    s_cat = lax.broadcasted_iota(jnp.int32, (c, HGROUP), 1) & (c - 1)
    bd_row = lax.broadcasted_iota(jnp.int32, (HGROUP, HGROUP), 0) >> 6
    bd_col = lax.broadcasted_iota(jnp.int32, (HGROUP, HGROUP), 1) >> 6
    bd_mask = bd_row == bd_col
    eye_cat = jnp.where(t_cat == s_cat, 1.0, 0.0).astype(F32)

    def block_diag(x):
        return jnp.where(bd_mask, jnp.concatenate([x] * (HGROUP // c), axis=0), 0.0).astype(BF16)

    chains = []
    dirs = ((0, rf, vf, af, lwf, kdf, bf_), (1, rb, vb, ab, lwb, kdb, bb_))
    for d, r_ref, v_ref, a_ref, lw_ref, kd_ref, b_ref in dirs:
        rev = d == 1
        earlier = (s_sq >= t_sq) if rev else (s_sq <= t_sq)
        tri = jnp.where((t_sq >> 6) == (s_sq >> 6), jnp.where(earlier, 1.0, 0.0), 0.0).astype(BF16)
        lw = lw_ref[...]
        h1 = lw.astype(BF16)
        r1 = lw - h1.astype(F32)
        h2 = r1.astype(BF16)
        h3 = (r1 - h2.astype(F32)).astype(BF16)
        cum = _dot(tri, h1) + _dot(tri, h2) + _dot(tri, h3)
        e_in = jnp.exp(cum)
        e_neg = jnp.exp(-cum)
        a_t = a_ref[...] * jnp.exp(cum - lw)
        r_t = r_ref[...] * e_in
        b_t = b_ref[...] * e_neg
        k_t = kd_ref[...] * e_neg
        v_all = v_ref[...]
        for ck in range(nk):
            rows = slice(ck * c, (ck + 1) * c)
            w_row = ck * c if rev else (ck + 1) * c - 1
            for g in range(D_WIDTH // HGROUP):
                cs = slice(g * HGROUP, (g + 1) * HGROUP)
                chains.append(dict(d=d, ck=ck, g=g, rev=rev, at=a_t[rows, cs], rt=r_t[rows, cs],
                                   bt=b_t[rows, cs], kt=k_t[rows, cs], vc=v_all[rows, cs],
                                   w=e_in[w_row:w_row + 1, cs]))

    def strict(ch):
        return (s_cat > t_cat) if ch['rev'] else (s_cat < t_cat)

    def incl(ch):
        return (s_cat >= t_cat) if ch['rev'] else (s_cat <= t_cat)

    for ch in chains:
        ch['lhs'] = jnp.concatenate([ch['at'], ch['rt']], axis=0).astype(BF16)
        ch['bd_v'] = block_diag(ch['vc'])
    m_b = [_dot_nt(ch['lhs'], block_diag(ch['bt'])) for ch in chains]
    m_k = [_dot_nt(ch['lhs'], block_diag(ch['kt'])) for ch in chains]
    for ch, mb, mk in zip(chains, m_b, m_k):
        ch['a_ab'] = jnp.where(strict(ch), mb[:c], 0.0)
        ch['a_rb'] = jnp.where(incl(ch), mb[c:], 0.0).astype(BF16)
        ch['a_ak'] = jnp.where(strict(ch), mk[:c], 0.0).astype(BF16)
        ch['a_rk'] = jnp.where(incl(ch), mk[c:], 0.0).astype(BF16)
        ch['t'] = eye_cat + jnp.where((t_cat >> 1) == (s_cat >> 1), ch['a_ab'], 0.0)
    m = 2
    while m < c:
        sh = int(math.log2(m))
        xs = []
        for ch in chains:
            a_m = jnp.where((t_cat >> (sh + 1)) == (s_cat >> (sh + 1)),
                            jnp.where((t_cat >> sh) != (s_cat >> sh), ch['a_ab'], 0.0), 0.0)
            xs.append(_dot(a_m.astype(BF16), block_diag(ch['t'])))
        for ch, x in zip(chains, xs):
            ch['t'] = ch['t'] + _dot(ch['t'].astype(BF16), block_diag(x))
        m *= 2
    akv = [_dot(ch['a_ak'], ch['bd_v']) for ch in chains]
    for ch in chains:
        ch['t'] = ch['t'].astype(BF16)
    a_hat = [_dot(ch['t'], block_diag(ch['at'])) for ch in chains]
    u_hat = [_dot(ch['t'], block_diag(x)) for ch, x in zip(chains, akv)]
    r_hat = [(ch['rt'] + _dot(ch['a_rb'], block_diag(x))).astype(BF16) for ch, x in zip(chains, a_hat)]
    o_hat = [_dot(ch['a_rb'], block_diag(x)) + _dot(ch['a_rk'], ch['bd_v']) for ch, x in zip(chains, u_hat)]
    m_c = [jnp.where(bd_mask, _dot_tn(x.astype(BF16), ch['bt'].astype(BF16)), 0.0).astype(BF16)
           for ch, x in zip(chains, a_hat)]
    n_c = [jnp.where(bd_mask, _dot_tn(jnp.concatenate([x, ch['vc']], axis=0).astype(BF16),
                                      jnp.concatenate([ch['bt'], ch['kt']], axis=0).astype(BF16)), 0.0)
           for ch, x in zip(chains, u_hat)]
    pre = {(ch['d'], ch['ck'], ch['g']): (r_hat[i], o_hat[i], m_c[i], n_c[i], ch['w'])
           for i, ch in enumerate(chains)}

    groups = [(d, g) for d in range(2) for g in range(D_WIDTH // HGROUP)]
    states = {dg: s_scr[dg[0], dg[1]] for dg in groups}
    for i in range(nk):
        st_b = {dg: states[dg].astype(BF16) for dg in groups}
        for d, g in groups:
            ck = nk - 1 - i if d == 1 else i
            rh, oh, mc, nc_, w = pre[(d, ck, g)]
            o_ref = ob_ref if d == 1 else of_ref
            o_ref[ck * c:(ck + 1) * c, g * HGROUP:(g + 1) * HGROUP] = _dot_nt(rh, st_b[(d, g)]) + oh
            states[(d, g)] = (states[(d, g)] + _dot(st_b[(d, g)], mc) + nc_) * w
    for d, g in groups:
        s_scr[d, g] = states[(d, g)]


def _scan_tables(segments, rows):
    fwd, bwd, first = [], [], []
    for start, bsz, lp in segments:
        nc = lp // rows
        for b in range(bsz):
            base = (start + b * lp) // rows
            for ci in range(nc):
                fwd.append(base + ci)
                bwd.append(base + nc - 1 - ci)
                first.append(1 if ci == 0 else 0)
    as_i32 = lambda t: jnp.asarray(np.asarray(t, np.int32))
    return as_i32(fwd), as_i32(bwd), as_i32(first)


def _scan(segments, n_rows, r, v, a, lwf, lwb, kdf, kdb, bf_, bb_):
    nk = SCAN_CHUNKS_PER_STEP
    rows = nk * SCAN_CHUNK
    if any(start % rows or lp % rows for start, _, lp in segments):
        nk, rows = 1, SCAN_CHUNK
    fwd, bwd, first = _scan_tables(segments, rows)
    nsteps = fwd.shape[0]
    blk = (rows, D_WIDTH)
    fspec = pl.BlockSpec(blk, lambda s, fw, bw, fi: (fw[s], 0))
    bspec = pl.BlockSpec(blk, lambda s, fw, bw, fi: (bw[s], 0))
    grid_spec = pltpu.PrefetchScalarGridSpec(
        num_scalar_prefetch=3, grid=(nsteps,),
        in_specs=[fspec] * 6 + [bspec] * 6,
        out_specs=[fspec, bspec],
        scratch_shapes=[pltpu.VMEM((2, D_WIDTH // HGROUP, HGROUP, HGROUP), F32)],
    )
    return pl.pallas_call(
        functools.partial(_scan_kernel, nk=nk), grid_spec=grid_spec,
        out_shape=[jax.ShapeDtypeStruct((n_rows, D_WIDTH), F32)] * 2,
        compiler_params=_cparams(("arbitrary",)), name="wkv_scan",
    )(fwd, bwd, first, r, v, a, lwf, kdf, bf_, r, v, a, lwb, kdb, bb_)


def _rope_tables(segments, n_rows, layer_kind):
    outs = []
    for start, bsz, lp in segments:
        p = jnp.arange(lp)
        d = jnp.arange(HEAD_DIM)
        if layer_kind == "axial":
            t = p - FRONT
            row = jnp.where(t >= 0, t // GRID_W, jnp.where(p >= PAD, t, 0)).astype(F32)
            col = jnp.where(t >= 0, t % GRID_W, jnp.where(p >= PAD, t, 0)).astype(F32)
            half = HEAD_DIM // 2
            inv = A_THETA ** (-jnp.arange(0, half, 2, dtype=F32) / half)
            ang = jnp.concatenate([row[:, None] * inv] * 2 + [col[:, None] * inv] * 2, axis=1)
            first = (d % 32) < 16
            cos = jnp.cos(ang)
            sa = jnp.where(first[None, :], -jnp.sin(ang), 0.0)
            sb = jnp.where(first[None, :], 0.0, jnp.sin(ang))
        else:
            pos = jnp.maximum(p - PAD, 0).astype(F32)
            inv = ROPE_THETA ** (-jnp.arange(0, ROPE_DIMS, 2, dtype=F32) / ROPE_DIMS)
            ang8 = pos[:, None] * inv
            ang = jnp.concatenate([ang8, ang8] + [jnp.zeros_like(ang8)] * 6, axis=1)
            cos = jnp.where((d < ROPE_DIMS)[None, :], jnp.cos(ang), 1.0)
            sa = jnp.where((d < 8)[None, :], -jnp.sin(ang), 0.0)
            sb = jnp.where(((d >= 8) & (d < 16))[None, :], jnp.sin(ang), 0.0)
        small = lax.optimization_barrier([jnp.concatenate([t_, t_], axis=1) for t_ in (cos, sa, sb)])
        tabs = [jnp.tile(t_, (bsz, 1)) for t_ in small]
        outs.append(tabs)
    tail = n_rows - sum(b * lp for _, b, lp in segments)
    res = []
    for idx in range(3):
        parts = [o[idx] for o in outs] + [jnp.zeros((tail, LANES), F32)]
        res.append(jnp.concatenate(parts, axis=0))
    return res


def _seq_view(flat, seg, width):
    start, bsz, lp = seg
    return flat[start:start + bsz * lp].reshape(bsz, lp, width)


def _attn_operands(qT, k, vT, seg):
    start, bsz, lp = seg
    chunk = vT.shape[2]
    k_seg = k[start:start + bsz * lp].reshape(bsz, lp, k.shape[1])
    v_seg = vT[start // chunk:(start + bsz * lp) // chunk].reshape(bsz, lp // chunk, vT.shape[1], chunk)
    return qT, k_seg, v_seg, start


def _to_flat(parts, n_rows, width, dtype):
    used = sum(p.shape[0] for p in parts)
    return jnp.concatenate(parts + [jnp.zeros((n_rows - used, width), dtype)], axis=0)


def _to_flat_cols(parts, n_rows):
    used = sum(p.shape[1] for p in parts)
    return jnp.concatenate(parts + [jnp.zeros((parts[0].shape[0], n_rows - used), parts[0].dtype)], axis=1)


def _block_avg(n, group):
    idx = np.arange(n) // group
    return jnp.asarray((idx[:, None] == idx[None, :]).astype(np.float32) / group).astype(BF16)


def _forward(xs, p):
    segments = []
    start = 0
    for x in xs:
        bsz, s, _ = x.shape
        lp = s + FRONT
        segments.append((start, bsz, lp))
        start += bsz * lp
    n_used = start
    tm = 512 if n_used >= 4096 else ATTN_CHUNK
    n_rows = _round_up(n_used, tm)

    meta = p['meta_tokens'].astype(F32)
    parts = []
    for x in xs:
        bsz = x.shape[0]
        lead = jnp.concatenate([jnp.zeros((PAD, D_MODEL), F32), meta], axis=0)
        parts.append(jnp.concatenate([jnp.broadcast_to(lead[None], (bsz, FRONT, D_MODEL)), x], axis=1)
                     .reshape(-1, D_MODEL))
    h = _to_flat(parts, n_rows, D_MODEL, F32)

    bs = _block_avg(HGROUP, HEAD_DIM)
    row = lambda t: t.reshape(1, -1).astype(F32)

    depth = p['pre_mix_g'].shape[0]
    for i in range(depth):
        if i % 2 == 0:
            e = i // 2
            cos, sa, sb = _rope_tables(segments, n_rows, "axial")
            ch = np.arange(B_DIM)
            ang = 2.0 * np.pi * ((ch[:, None] * ch[None, :]) % B_DIM) / B_DIM
            cc, sc = jnp.asarray(np.cos(ang), F32), jnp.asarray(np.sin(ang), F32)
            wl = p['b_w'][e].astype(F32)
            hp = lax.Precision.HIGHEST
            pmat = jnp.einsum('cd,gde->gce', cc, wl, precision=hp)
            qmat = -jnp.einsum('cd,gde->gce', sc, wl, precision=hp)
            bd = lambda m: jax.scipy.linalg.block_diag(*[m[g] for g in range(B_GROUPS)])
            pq_base = jnp.concatenate([bd(pmat), bd(qmat)], axis=1)
            qT, k, vT, gf = _row_call(
                "in_even", _in_even_kernel, n_rows, tm, [h, cos, sa, sb],
                [row(p['pre_mix_g'][i]), p['even_w_in'][e].astype(BF16),
                 row(jnp.tile(p['a_q_gain'][e], A_HEADS) * (HEAD_DIM ** -0.5 * LOG2_E)),
                 row(jnp.tile(p['a_k_gain'][e], A_KV_HEADS)), row(p['b_norm_g'][e]), bs,
                 pq_base.astype(BF16)],
                [("cols", A_Q, BF16), ("rows", A_KV, BF16), ("chunks", A_KV, BF16), ("rows", 2 * B_W, F32)])
            ya_parts, yb_parts = [], []
            for seg in segments:
                _, bsz, lp = seg
                ya_parts.append(_attn_a(*_attn_operands(qT, k, vT, seg)))
                gseq = _seq_view(gf, seg, 2 * B_W)[:, PAD:]
                yb = _mixer_b(gseq, row(p['b_b'][e]))
                yb_parts.append(jnp.pad(yb, ((0, 0), (PAD, 0), (0, 0))).reshape(bsz * lp, B_W).astype(BF16))
            yat = _to_flat_cols(ya_parts, n_rows)
            mix_kernel, mix_rows, mix_consts = _mix_ffn_even_kernel, [_to_flat(yb_parts, n_rows, B_W, BF16)], []
            w_out = p['even_w_out'][e].astype(BF16)
            wa, wb = w_out[:A_Q], w_out[A_Q:]
        else:
            o = i // 2
            cos, sa, sb = _rope_tables(segments, n_rows, "partial")
            qT, k, vT, u = _row_call(
                "in_odd", _in_odd_kernel, n_rows, tm, [h, cos, sa, sb],
                [row(p['pre_mix_g'][i]), p['odd_w_in'][o].astype(BF16)],
                [("cols", C_Q, BF16), ("rows", C_KV, BF16), ("chunks", C_KV, BF16), ("rows", D_IN, F32)])
            sink = p['c_sink'][o].astype(F32)
            yat = _to_flat_cols([_attn_c(sink, *_attn_operands(qT, k, vT, seg)) for seg in segments], n_rows)
            mix_kernel = _mix_ffn_odd_kernel
            mix_rows, mix_consts = _mixer_d(u, p, o, bs, segments, n_rows, tm)
            w_out = p['odd_w_out'][o].astype(BF16)
            wa, wb = w_out[:C_Q], w_out[C_Q:]
        (h,) = _row_call(
            "mix_ffn", functools.partial(mix_kernel, chunk=256), n_rows, tm, [h] + mix_rows,
            mix_consts + [wa, wb, row(p['post_mix_g'][i]),
             row(p['pre_ffn_g'][i]), p['ffn_w_gate'][i].astype(BF16), p['ffn_w_up'][i].astype(BF16),
             p['ffn_w_down'][i].astype(BF16), row(p['post_ffn_g'][i])],
            [("rows", D_MODEL, F32)], scratch=[pltpu.VMEM((tm, D_MODEL), F32)], col_ins=[yat])

    outs = []
    for seg in segments:
        outs.append(_seq_view(h, seg, D_MODEL)[:, FRONT:])
    return tuple(outs)


def _mixer_d(u, p, o, bs, segments, n_rows, tm):
    row = lambda t: t.reshape(1, -1).astype(F32)
    w = D_WIDTH
    zeros = lambda r: jnp.zeros((r, w), F32)
    w2 = jnp.concatenate([jnp.concatenate([p['d_w_up'][o][0], zeros(DECAY_RANK)], axis=1),
                          jnp.concatenate([zeros(DECAY_RANK), p['d_w_up'][o][1]], axis=1)], axis=0)
    a2 = jnp.concatenate([jnp.concatenate([p['d_a_up'][o][0], zeros(ICLR_RANK)], axis=1),
                          jnp.concatenate([zeros(ICLR_RANK), p['d_a_up'][o][1]], axis=1)], axis=0)
    consts = [row(p['d_mu_prev'][o]), row(p['d_mu_next'][o]), w2.astype(BF16), a2.astype(BF16),
              p['d_g_up'][o].astype(BF16), row(p['d_w0'][o]), row(p['d_a0'][o]),
              row(p['d_k_k'][o]), row(p['d_k_a'][o]), row(p['d_r_k'][o]), bs]
    nb8 = n_rows // 8
    t8 = tm // 8
    in_specs = [pl.BlockSpec((tm, D_IN), lambda i: (i, 0)),
                pl.BlockSpec((8, D_IN), lambda i: (jnp.maximum(i * t8 - 1, 0), 0)),
                pl.BlockSpec((8, D_IN), lambda i: (jnp.minimum((i + 1) * t8, nb8 - 1), 0))]
    in_specs += [pl.BlockSpec(a.shape, lambda i, nd=a.ndim: (0,) * nd) for a in consts]
    out_dtypes = [BF16, BF16, BF16, F32, F32, BF16, BF16, BF16, BF16, F32, BF16]
    prep = pl.pallas_call(
        functools.partial(_d_prep_kernel, tm=tm, segments=tuple(segments)),
        grid=(n_rows // tm,), in_specs=in_specs,
        out_specs=[pl.BlockSpec((tm, w), lambda i: (i, 0))] * len(out_dtypes),
        out_shape=[jax.ShapeDtypeStruct((n_rows, w), dt) for dt in out_dtypes],
        compiler_params=_cparams(("parallel",)), name="d_prep",
    )(u, u, u, *consts)
    r, v, a, lwf, lwb, kdf, kdb, bf_, bb_, bonus, g = prep
    of, ob = _scan(segments, n_rows, r, v, a, lwf, lwb, kdf, kdb, bf_, bb_)
    return [of, ob, bonus, g], [row(p['d_ln_g'][o]), row(p['d_ln_b'][o]), bs]


def kernel(x_prompt, x_sample, meta_tokens, pre_mix_g, post_mix_g, pre_ffn_g, post_ffn_g, even_w_in, even_w_out, a_q_gain, a_k_gain, b_norm_g, b_w, b_b, odd_w_in, odd_w_out, c_sink, d_mu_prev, d_mu_next, d_w0, d_w_up, d_a0, d_a_up, d_g_up, d_k_k, d_k_a, d_r_k, d_ln_g, d_ln_b, ffn_w_gate, ffn_w_up, ffn_w_down):
    params = dict(meta_tokens=meta_tokens, pre_mix_g=pre_mix_g, post_mix_g=post_mix_g,
                  pre_ffn_g=pre_ffn_g, post_ffn_g=post_ffn_g,
                  even_w_in=even_w_in, even_w_out=even_w_out, a_q_gain=a_q_gain, a_k_gain=a_k_gain,
                  b_norm_g=b_norm_g, b_w=b_w, b_b=b_b,
                  odd_w_in=odd_w_in, odd_w_out=odd_w_out, c_sink=c_sink,
                  d_mu_prev=d_mu_prev, d_mu_next=d_mu_next, d_w0=d_w0, d_w_up=d_w_up,
                  d_a0=d_a0, d_a_up=d_a_up, d_g_up=d_g_up, d_k_k=d_k_k, d_k_a=d_k_a, d_r_k=d_r_k,
                  d_ln_g=d_ln_g, d_ln_b=d_ln_b,
                  ffn_w_gate=ffn_w_gate, ffn_w_up=ffn_w_up, ffn_w_down=ffn_w_down)
    return _forward([x_prompt, x_sample], params)
```

```python
import functools
import math

import numpy as np
import jax
import jax.numpy as jnp
from jax import lax
from jax.experimental import pallas as pl
from jax.experimental.pallas import tpu as pltpu

F32 = jnp.float32
BF16 = jnp.bfloat16

D_MODEL = 1024
HEAD_DIM = 64
N_META = 16
GRID_W = 64
WINDOW = 128
RMS_EPS = 1e-6
A_HEADS, A_KV_HEADS, A_THETA = 12, 4, 10000.0
B_GROUPS, B_DIM = 4, 64
C_HEADS, C_KV_HEADS = 8, 2
ROPE_THETA = 500000.0
ROPE_DIMS = HEAD_DIM // 4
D_HEADS = 8
D_WIDTH = D_HEADS * HEAD_DIM
DECAY_RANK, ICLR_RANK, GATE_RANK = 64, 64, 128
LNX_EPS = 64e-5
D_FF = 2816
A_Q, A_KV, B_W = A_HEADS * HEAD_DIM, A_KV_HEADS * HEAD_DIM, B_GROUPS * B_DIM
C_Q, C_KV = C_HEADS * HEAD_DIM, C_KV_HEADS * HEAD_DIM
D_IN = 3 * D_WIDTH + 2 * DECAY_RANK + 2 * ICLR_RANK + GATE_RANK

LANES = 128
FRONT = 256
PAD = FRONT - N_META
NEG = -1e30
SCAN_CHUNK = 64
SCAN_CHUNKS_PER_STEP = 4
HGROUP = 256
FFT_L1 = 16
VMEM_LIMIT = 56 * 1024 * 1024
WIN_QBLOCKS = 1
ROW_SUBTILE = 256
ATTN_CHUNK = 256
ATTN_MAX_CHUNKS = 20
ATTN_UNROLL = 16
LOG2_E = math.log2(math.e)


def _round_up(x, m):
    return (x + m - 1) // m * m


def _cparams(sem):
    return pltpu.CompilerParams(dimension_semantics=sem, vmem_limit_bytes=VMEM_LIMIT)


def _dot(a, b):
    return jnp.dot(a, b, preferred_element_type=F32)


def _dot_nt(a, b):
    return lax.dot_general(a, b, (((1,), (1,)), ((), ())), preferred_element_type=F32)


def _dot_tn(a, b):
    return lax.dot_general(a, b, (((0,), (0,)), ((), ())), preferred_element_type=F32)


def _split2(x):
    hi = x.astype(BF16)
    lo = (x - hi.astype(F32)).astype(BF16)
    return hi, lo


def _group_mean(x, bs):
    hi, lo = _split2(x)
    return _dot(hi, bs) + _dot(lo, bs)


def _rope(x, cos, sin_a, sin_b, shift):
    n = x.shape[1]
    return x * cos + pltpu.roll(x, n - shift, 1) * sin_a + pltpu.roll(x, shift, 1) * sin_b


def _wide(t, n):
    return t if n == LANES else jnp.concatenate([t] * (n // LANES), axis=1)


def _rms_rows(x, g):
    ms = jnp.mean(x * x, axis=-1, keepdims=True)
    return x * lax.rsqrt(ms + RMS_EPS) * g


def _in_even_kernel(h_ref, cos_ref, sa_ref, sb_ref, g_ref, w_ref, qg_ref, kg_ref, bg_ref, bs_ref, pq_ref,
                    q_ref, k_ref, v_ref, gf_ref):
    bs = bs_ref[...]
    for r in range(h_ref.shape[0] // ROW_SUBTILE):
        rows = slice(r * ROW_SUBTILE, (r + 1) * ROW_SUBTILE)
        hn = _rms_rows(h_ref[rows, :], g_ref[...]).astype(BF16)
        proj = _dot(hn, w_ref[...])
        cos, sa, sb = (_wide(t[rows, :], HGROUP) for t in (cos_ref, sa_ref, sb_ref))

        def norm_rope(x, gain):
            xn = x * lax.rsqrt(_group_mean(x * x, bs) + RMS_EPS) * gain
            return _rope(xn, cos, sa, sb, 16)

        for s in range(A_Q // HGROUP):
            cs = slice(s * HGROUP, (s + 1) * HGROUP)
            q_ref[cs, rows] = norm_rope(proj[:, cs], qg_ref[:, cs]).T.astype(BF16)
        k_ref[rows, :] = norm_rope(proj[:, A_Q:A_Q + A_KV], kg_ref[...]).astype(BF16)
        v_ref[r] = proj[:, A_Q + A_KV:A_Q + 2 * A_KV].T.astype(BF16)
        f = proj[:, A_Q + 2 * A_KV:]
        fn = f * lax.rsqrt(_group_mean(f * f, bs) + RMS_EPS) * bg_ref[...]
        gf_ref[rows, :] = _dot(fn.astype(BF16), pq_ref[...])


def _in_odd_kernel(h_ref, cos_ref, sa_ref, sb_ref, g_ref, w_ref, q_ref, k_ref, v_ref, u_ref):
    for r in range(h_ref.shape[0] // ROW_SUBTILE):
        rows = slice(r * ROW_SUBTILE, (r + 1) * ROW_SUBTILE)
        hn = _rms_rows(h_ref[rows, :], g_ref[...]).astype(BF16)
        proj = _dot(hn, w_ref[...])
        cos, sa, sb = cos_ref[rows, :], sa_ref[rows, :], sb_ref[rows, :]
        cos2, sa2, sb2 = (_wide(t, HGROUP) for t in (cos, sa, sb))
        for s in range(C_Q // HGROUP):
            cs = slice(s * HGROUP, (s + 1) * HGROUP)
            q_ref[cs, rows] = (_rope(proj[:, cs], cos2, sa2, sb2, 8)
                               * (HEAD_DIM ** -0.5 * LOG2_E)).T.astype(BF16)
        k_ref[rows, :] = _rope(proj[:, C_Q:C_Q + C_KV], cos, sa, sb, 8).astype(BF16)
        v = proj[:, C_Q + C_KV:C_Q + 2 * C_KV]
        for c in range(ROW_SUBTILE // C_KV):
            v_ref[r * (ROW_SUBTILE // C_KV) + c] = v[c * C_KV:(c + 1) * C_KV, :].T.astype(BF16)
        u_ref[rows, :] = proj[:, C_Q + 2 * C_KV:]


def _mix_ffn_even_kernel(h_ref, yb_ref, *rest, **static):
    _mix_ffn(h_ref, yb_ref[...], *rest, **static)


def _mix_ffn_odd_kernel(h_ref, of_ref, ob_ref, bonus_ref, gate_ref, lng_ref, lnb_ref, bs_ref, *rest, **static):
    o = of_ref[...] + ob_ref[...]
    bs = bs_ref[...]

    def head_mean(x):
        return jnp.concatenate(
            [_group_mean(x[:, s * HGROUP:(s + 1) * HGROUP], bs) for s in range(D_WIDTH // HGROUP)], axis=1)

    cen = o - head_mean(o)
    var = head_mean(cen * cen)
    y = cen * lax.rsqrt(var + LNX_EPS) * lng_ref[...] + lnb_ref[...] + bonus_ref[...]
    _mix_ffn(h_ref, (y * gate_ref[...]).astype(BF16), *rest, **static)


def _mix_ffn(h_ref, yb, wa_ref, wb_ref, gm_ref, g1_ref, wg_ref, wu_ref, wd_ref, g2_ref, *rest,
             chunk, group_tiles):
    *yat_refs, o_ref, acc_ref = rest
    yat = yat_refs[0][...]
    for ref, first_tile in zip(yat_refs[1:], group_tiles[1:]):
        yat = jnp.where(pl.program_id(0) >= first_tile, ref[...], yat)
    mix = _dot_tn(yat, wa_ref[...]) + _dot(yb, wb_ref[...])
    h = h_ref[...] + _rms_rows(mix, gm_ref[...])
    hn = _rms_rows(h, g1_ref[...]).astype(BF16)
    for c in range(D_FF // chunk):
        cs = slice(c * chunk, (c + 1) * chunk)
        gate = _dot(hn, wg_ref[:, cs])
        up = _dot(hn, wu_ref[:, cs])
        act = (gate * (1.0 / (1.0 + jnp.exp(-gate))) * up).astype(BF16)
        part = _dot(act, wd_ref[cs, :])
        if c == 0:
            acc_ref[...] = part
        else:
            acc_ref[...] += part
    o_ref[...] = h + _rms_rows(acc_ref[...], g2_ref[...])


def _row_call(name, kernel, n_rows, tm, row_ins, const_ins, outs, scratch=(), col_ins=()):
    grid = (n_rows // tm,)
    in_specs = [pl.BlockSpec((tm, a.shape[1]), lambda i: (i, 0)) for a in row_ins]
    in_specs += [pl.BlockSpec(a.shape, lambda i, nd=a.ndim: (0,) * nd, pipeline_mode=pl.Buffered(1))
                 for a in const_ins]
    first_tile = 0
    for a in col_ins:
        n_tiles = -(-a.shape[1] // tm)
        in_specs.append(pl.BlockSpec(
            (a.shape[0], tm), lambda i, lo=first_tile, n=n_tiles: (0, jnp.clip(i - lo, 0, n - 1))))
        first_tile += a.shape[1] // tm
    out_specs, out_shape = [], []
    for kind, n, dt in outs:
        if kind == "rows":
            out_specs.append(pl.BlockSpec((tm, n), lambda i: (i, 0)))
            out_shape.append(jax.ShapeDtypeStruct((n_rows, n), dt))
        elif kind == "cols":
            out_specs.append(pl.BlockSpec((n, tm), lambda i: (0, i)))
            out_shape.append(jax.ShapeDtypeStruct((n, n_rows), dt))
        else:
            out_specs.append(pl.BlockSpec((tm // n, n, n), lambda i: (i, 0, 0)))
            out_shape.append(jax.ShapeDtypeStruct((n_rows // n, n, n), dt))
    return pl.pallas_call(
        kernel, grid=grid, in_specs=in_specs, out_specs=out_specs, out_shape=out_shape,
        scratch_shapes=list(scratch), compiler_params=_cparams(("parallel",)), name=name,
    )(*row_ins, *const_ins, *col_ins)


def _padded_q(q_ref, h, kv):
    qh = q_ref[h * HEAD_DIM:(h + 1) * HEAD_DIM, :]
    zero = jnp.zeros_like(qh)
    return jnp.concatenate([qh, zero] if kv % 2 == 0 else [zero, qh], axis=0)


def _next_step(b, i, j, bsz, nq, nkb):
    j1 = j + 1
    wrap_j = (j1 == nkb).astype(jnp.int32)
    i1 = i + wrap_j
    wrap_i = (i1 == nq).astype(jnp.int32)
    return jnp.minimum(b + wrap_i, bsz - 1), i1 * (1 - wrap_i), j1 * (1 - wrap_j)


def _attn_a_kernel(qT_ref, qn_ref, k_ref, kn_ref, vT_ref, o_ref, m_scr, l_scr, acc_scr, s_scr, mx_scr,
                   *, nchunk, nkb, nq, bsz, tq):
    b, i, j = pl.program_id(0), pl.program_id(1), pl.program_id(2)

    @pl.when(j == 0)
    def _():
        m_scr[...] = jnp.full(m_scr.shape, NEG, F32)
        l_scr[...] = jnp.zeros(l_scr.shape, F32)
        acc_scr[...] = jnp.zeros(acc_scr.shape, F32)

    group = A_HEADS // A_KV_HEADS
    ck = ATTN_CHUNK
    row_iota = lax.broadcasted_iota(jnp.int32, (ck, tq), 0)
    key_row = row_iota + j * (nchunk * ck)
    key_row_next = row_iota + _next_step(b, i, j, bsz, nq, nkb)[2] * (nchunk * ck)

    def fold(s, op):
        return op(s.reshape(ck // 8, 8, tq), axis=0)

    def gang_q(g):
        if g == A_KV_HEADS:
            return [_padded_q(qn_ref, t, 0) for t in range(group)]
        return [_padded_q(qT_ref, g * group + t, g) for t in range(group)]

    def gang_scores(g, qps, c, mxs, first=False):
        rows = pl.ds(pl.multiple_of(c * ck, ck), ck)
        if g == A_KV_HEADS:
            kc, krow = kn_ref[0, rows, :], key_row_next
        else:
            kc, krow = k_ref[0, rows, (g // 2) * LANES:(g // 2 + 1) * LANES], key_row
        out = []
        for t in range(group):
            s = _dot(kc, qps[t])
            if first:
                s = jnp.where(krow >= PAD, s, NEG)
            s_scr[g % 2, t, c] = s
            out.append(fold(s, jnp.max) if mxs is None else jnp.maximum(mxs[t], fold(s, jnp.max)))
        return tuple(out)

    def gang_values(g, c, m_news):
        vc = vT_ref[0, c, g * HEAD_DIM:(g + 1) * HEAD_DIM, :]
        ps = [jnp.exp2(s_scr[g % 2, t, c] - m_news[t]) for t in range(group)]
        return tuple(_dot(vc, p.astype(BF16)) for p in ps), tuple(fold(p, jnp.sum) for p in ps)

    def add(xs, ys):
        return tuple(x + y for x, y in zip(xs, ys))

    unroll = max(u for u in range(1, ATTN_UNROLL + 1) if max(nchunk - 1, 1) % u == 0)

    @pl.when((b == 0) & (i == 0) & (j == 0))
    def _():
        qps = gang_q(0)
        mxs = lax.fori_loop(1, nchunk, lambda c, mxs: gang_scores(0, qps, c, mxs),
                            gang_scores(0, qps, 0, None, first=True), unroll=unroll)
        for t in range(group):
            mx_scr[t] = mxs[t]

    mxs = tuple(mx_scr[t] for t in range(group))
    for g in range(A_KV_HEADS):
        heads = [g * group + t for t in range(group)]
        m_olds = [m_scr[h] for h in heads]
        m_news = [jnp.maximum(mo, jnp.max(mx, axis=0, keepdims=True)) for mo, mx in zip(m_olds, mxs)]
        qps = gang_q(g + 1)

        def body(c, carry, g=g, m_news=m_news, qps=qps):
            accs, lsums, mxn = carry
            mxn = gang_scores(g + 1, qps, c, mxn)
            pvs, pss = gang_values(g, c, m_news)
            return add(accs, pvs), add(lsums, pss), mxn

        mx0 = gang_scores(g + 1, qps, 0, None, first=True)
        accs, lsums, mxs = lax.fori_loop(1, nchunk, body, (*gang_values(g, 0, m_news), mx0), unroll=unroll)
        for t, h in enumerate(heads):
            alpha = jnp.exp2(m_olds[t] - m_news[t])
            acc_scr[h] = acc_scr[h] * alpha + accs[t]
            l_scr[h] = l_scr[h] * alpha + jnp.sum(lsums[t], axis=0, keepdims=True)
            m_scr[h] = m_news[t]
    for t in range(group):
        mx_scr[t] = mxs[t]

    @pl.when(j == nkb - 1)
    def _():
        for h in range(A_HEADS):
            o_ref[h * HEAD_DIM:(h + 1) * HEAD_DIM, :] = (acc_scr[h] / l_scr[h]).astype(o_ref.dtype)


def _attn_a(qT, k, vT, start):
    bsz, lp = k.shape[:2]
    off = start // ATTN_CHUNK
    tq = ATTN_CHUNK
    total = lp // ATTN_CHUNK
    nkb = min(n for n in range(1, total + 1) if total % n == 0 and total // n <= ATTN_MAX_CHUNKS)
    tk = lp // nkb
    nchunk = tk // ATTN_CHUNK
    nq = lp // tq
    group = A_HEADS // A_KV_HEADS
    kernel = functools.partial(_attn_a_kernel, nchunk=nchunk, nkb=nkb, nq=nq, bsz=bsz, tq=tq)

    def next_q(b, i, j):
        b2, i2, _ = _next_step(b, i, j, bsz, nq, nkb)
        return 0, off + b2 * nq + i2

    def next_k(b, i, j):
        b2, _, j2 = _next_step(b, i, j, bsz, nq, nkb)
        return b2, j2, 0

    return pl.pallas_call(
        kernel, grid=(bsz, nq, nkb),
        in_specs=[
            pl.BlockSpec((A_Q, tq), lambda b, i, j: (0, off + b * nq + i)),
            pl.BlockSpec((group * HEAD_DIM, tq), next_q),
            pl.BlockSpec((1, tk, A_KV), lambda b, i, j: (b, j, 0)),
            pl.BlockSpec((1, tk, LANES), next_k),
            pl.BlockSpec((1, nchunk, A_KV, ATTN_CHUNK), lambda b, i, j: (b, j, 0, 0)),
        ],
        out_specs=pl.BlockSpec((A_Q, tq), lambda b, i, j: (0, b * nq + i)),
        out_shape=jax.ShapeDtypeStruct((A_Q, bsz * lp), BF16),
        scratch_shapes=[
            pltpu.VMEM((A_HEADS, 1, tq), F32),
            pltpu.VMEM((A_HEADS, 1, tq), F32),
            pltpu.VMEM((A_HEADS, HEAD_DIM, tq), F32),
            pltpu.VMEM((2, group, nchunk, ATTN_CHUNK, tq), F32),
            pltpu.VMEM((group, 8, tq), F32),
        ],
        compiler_params=_cparams(("arbitrary", "arbitrary", "arbitrary")), name="attn_a",
    )(qT, qT, k, k, vT)


def _attn_c_kernel(sink_ref, qT_ref, *refs, lp, nslot):
    j = pl.program_id(1)
    k_refs, v_refs, o_ref = refs[:nslot], refs[nslot:2 * nslot], refs[2 * nslot]
    tq = WIN_QBLOCKS * LANES
    row = lax.broadcasted_iota(jnp.int32, (LANES, tq), 0)
    pq = lax.broadcasted_iota(jnp.int32, (LANES, tq), 1) + j * tq
    biases = []
    for slot in range(nslot):
        if slot == 0:
            bias = jnp.where(row >= PAD % LANES, 0.0, NEG)
        else:
            pk = row + (j * WIN_QBLOCKS + slot - 2) * LANES
            in_window = jnp.where(jnp.abs(pq - pk) <= WINDOW, 0.0, NEG)
            bias = jnp.where(pk >= FRONT, jnp.where(pk < lp, in_window, NEG), NEG)
        biases.append(bias.astype(F32))
    group = C_HEADS // C_KV_HEADS
    scores = []
    for h in range(C_HEADS):
        qp = _padded_q(qT_ref, h, h // group)
        scores.append([_dot(k_refs[t][0], qp) + biases[t] for t in range(nslot)])
    for h in range(C_HEADS):
        kv = h // group
        sink = sink_ref[h] * LOG2_E
        ss = scores[h]
        m = functools.reduce(jnp.maximum, ss)
        m = jnp.maximum(jnp.max(m, axis=0, keepdims=True), sink)
        acc = jnp.zeros((HEAD_DIM, tq), F32)
        denom = jnp.exp2(sink - m)
        for t in range(nslot):
            p = jnp.exp2(ss[t] - m)
            acc = acc + _dot(v_refs[t][0, 0, kv * HEAD_DIM:(kv + 1) * HEAD_DIM, :], p.astype(BF16))
            denom = denom + jnp.sum(p, axis=0, keepdims=True)
        o_ref[h * HEAD_DIM:(h + 1) * HEAD_DIM, :] = (acc / denom).astype(o_ref.dtype)


def _attn_c(sink, qT, k, vT, start):
    bsz, lp = k.shape[:2]
    nb = lp // LANES
    tq = WIN_QBLOCKS * LANES
    nq = lp // tq
    off = start // tq
    nslot = WIN_QBLOCKS + 3
    kernel = functools.partial(_attn_c_kernel, lp=lp, nslot=nslot)

    def kspec(fn):
        return pl.BlockSpec((1, LANES, C_KV), lambda b, j: (b, fn(j), 0))

    def vspec(fn):
        return pl.BlockSpec((1, 1, C_KV, LANES), lambda b, j: (b, fn(j), 0, 0))

    def neighbour(s):
        return lambda j: jnp.clip(j * WIN_QBLOCKS + s - 2, 0, nb - 1)

    fns = [lambda j: PAD // LANES] + [neighbour(s) for s in range(1, nslot)]
    return pl.pallas_call(
        kernel, grid=(bsz, nq),
        in_specs=[pl.BlockSpec(memory_space=pltpu.SMEM),
                  pl.BlockSpec((C_Q, tq), lambda b, j: (0, off + b * nq + j))]
                 + [kspec(f) for f in fns] + [vspec(f) for f in fns],
        out_specs=pl.BlockSpec((C_Q, tq), lambda b, j: (0, b * nq + j)),
        out_shape=jax.ShapeDtypeStruct((C_Q, bsz * lp), BF16),
        compiler_params=_cparams(("parallel", "parallel")), name="attn_c",
    )(sink, qT, *([k] * nslot), *([vT] * nslot))


def _fft16(vals):
    n = FFT_L1
    bits = n.bit_length() - 1
    a = [vals[int(format(idx, '0%db' % bits)[::-1], 2)] for idx in range(n)]
    m = 2
    while m <= n:
        half = m // 2
        for base in range(0, n, m):
            for t in range(half):
                ur, ui = a[base + t]
                vr, vi = a[base + t + half]
                if 4 * t == m:
                    tr, ti = vi, -vr
                elif t > 0:
                    wr, wi = math.cos(-2.0 * math.pi * t / m), math.sin(-2.0 * math.pi * t / m)
                    tr, ti = vr * wr - vi * wi, vr * wi + vi * wr
                else:
                    tr, ti = vr, vi
                a[base + t] = (ur + tr, ui + ti)
                a[base + t + half] = (ur - tr, ui - ti)
        m *= 2
    return a


def _fft1_kernel(x_ref, o_ref):
    def strip(s, carry):
        rows = pl.ds(pl.multiple_of(s * 8, 8), 8)
        for q in range(B_W // LANES):
            re_cols = slice(q * LANES, (q + 1) * LANES)
            im_cols = slice(B_W + q * LANES, B_W + (q + 1) * LANES)
            vals = [(x_ref[0, n, rows, re_cols], x_ref[0, n, rows, im_cols]) for n in range(FFT_L1)]
            for kk, (re, im) in enumerate(_fft16(vals)):
                o_ref[0, kk, rows, re_cols] = re
                o_ref[0, kk, rows, im_cols] = im
        return carry

    lax.fori_loop(0, x_ref.shape[2] // 8, strip, 0)


def _fft2_kernel(a_ref, tc_ref, ts_ref, c2_ref, s2_ref, bias_ref, o_ref):
    a = a_ref[0, 0]
    tc, ts = _wide(tc_ref[0], B_W), _wide(ts_ref[0], B_W)
    are, aim = a[:, :B_W], a[:, B_W:]
    bre = (are * tc + aim * ts).astype(BF16)
    bim = (aim * tc - are * ts).astype(BF16)
    o_ref[0] = _dot(c2_ref[...], bre) + _dot(s2_ref[...], bim) + bias_ref[...]


def _dft_tables(l2, l2p, l2o):
    l1 = FFT_L1
    length = l1 * l2
    n1 = np.arange(l1)
    n2 = np.arange(l2)
    angt = 2.0 * np.pi * (n1[:, None] * n2[None, :]) / length
    tc = np.zeros((l1, l2p, LANES), np.float32)
    ts = np.zeros((l1, l2p, LANES), np.float32)
    tc[:, :l2, :] = np.cos(angt)[:, :, None]
    ts[:, :l2, :] = np.sin(angt)[:, :, None]
    ang2 = 2.0 * np.pi * ((n2[:, None] * n2[None, :]) % l2) / l2
    scale = 1.0 / math.sqrt(B_DIM * length)
    c2 = np.zeros((l2o, l2p), np.float32)
    s2 = np.zeros((l2o, l2p), np.float32)
    c2[:l2, :l2] = np.cos(ang2) * scale
    s2[:l2, :l2] = np.sin(ang2) * scale
    as_bf = lambda t: jnp.asarray(t, F32).astype(BF16)
    return jnp.asarray(tc), jnp.asarray(ts), as_bf(c2), as_bf(s2)


def _mixer_b(gseq, bias):
    bsz, length, _ = gseq.shape
    l1 = FFT_L1
    l2 = length // l1
    l2p, l2o = _round_up(l2, LANES), _round_up(l2, 8)
    tc, ts, c2, s2 = _dft_tables(l2, l2p, l2o)
    x = jnp.pad(gseq.reshape(bsz, l1, l2, 2 * B_W), ((0, 0), (0, 0), (0, l2p - l2), (0, 0)))
    blk = (1, l1, LANES, 2 * B_W)
    a = pl.pallas_call(
        _fft1_kernel, grid=(bsz, l2p // LANES),
        in_specs=[pl.BlockSpec(blk, lambda b, i: (b, 0, i, 0))],
        out_specs=pl.BlockSpec(blk, lambda b, i: (b, 0, i, 0)),
        out_shape=jax.ShapeDtypeStruct((bsz, l1, l2p, 2 * B_W), F32),
        compiler_params=_cparams(("parallel", "parallel")), name="fft1",
    )(x)
    y = pl.pallas_call(
        _fft2_kernel, grid=(bsz, l1),
        in_specs=[pl.BlockSpec((1, 1, l2p, 2 * B_W), lambda b, k: (b, k, 0, 0)),
                  pl.BlockSpec((1, l2p, LANES), lambda b, k: (k, 0, 0)),
                  pl.BlockSpec((1, l2p, LANES), lambda b, k: (k, 0, 0)),
                  pl.BlockSpec((l2o, l2p), lambda b, k: (0, 0)),
                  pl.BlockSpec((l2o, l2p), lambda b, k: (0, 0)),
                  pl.BlockSpec((1, B_W), lambda b, k: (0, 0))],
        out_specs=pl.BlockSpec((1, l2o, B_W), lambda b, k: (b, 0, k)),
        out_shape=jax.ShapeDtypeStruct((bsz, l2o, l1 * B_W), F32),
        compiler_params=_cparams(("parallel", "parallel")), name="fft2",
    )(a, tc, ts, c2, s2, bias)
    return y[:, :l2].reshape(bsz, length, B_W)


def _seq_position(rows, segments):
    pos = jnp.full(rows.shape, -1.0, F32)
    seqlen = jnp.full(rows.shape, 1.0, F32)
    for start, bsz, lp in segments:
        rel = rows - float(start)
        q = jnp.floor((rel + 0.5) * (1.0 / lp))
        inside = jnp.where(rel >= 0.0, jnp.where(rel < float(bsz * lp), 1.0, 0.0), 0.0) > 0.5
        pos = jnp.where(inside, rel - q * lp, pos)
        seqlen = jnp.where(inside, float(lp), seqlen)
    return pos, seqlen


def _sigmoid(x):
    return 1.0 / (1.0 + jnp.exp(-x))


def _d_prep_kernel(u_ref, up_ref, un_ref, mup_ref, mun_ref, w2_ref, a2_ref, gup_ref, w0_ref, a0_ref,
                   kk_ref, ka_ref, rk_ref, bs_ref,
                   r_ref, v_ref, a_ref, lwf_ref, lwb_ref, kdf_ref, kdb_ref, bf_ref, bb_ref,
                   bonus_ref, g_ref, *, tm, segments):
    i = pl.program_id(0)
    u = u_ref[...]
    rows = (lax.broadcasted_iota(jnp.int32, (tm, 1), 0) + i * tm).astype(F32)
    pos, seqlen = _seq_position(rows, segments)
    edge = lax.broadcasted_iota(jnp.int32, (8, 1), 0)
    u_prev = pltpu.roll(u, 1, 0)
    u_prev = jnp.concatenate([jnp.where(edge == 0, up_ref[7:8, :], u_prev[:8]), u_prev[8:]], axis=0)
    u_next = pltpu.roll(u, tm - 1, 0)
    u_next = jnp.concatenate([u_next[:tm - 8], jnp.where(edge == 7, un_ref[0:1, :], u_next[tm - 8:])], axis=0)
    u_prev = jnp.where(pos == float(PAD), 0.0, u_prev)
    u_next = jnp.where(pos == seqlen - 1.0, 0.0, u_next)
    u = u + mup_ref[...] * (u_prev - u) + mun_ref[...] * (u_next - u)
    valid = jnp.where(pos >= float(PAD), 1.0, 0.0)

    w = D_WIDTH
    r, k, v = u[:, :w], u[:, w:2 * w], u[:, 2 * w:3 * w]
    c0 = 3 * w
    dec = _dot(jnp.tanh(u[:, c0:c0 + 2 * DECAY_RANK]).astype(BF16), w2_ref[...]) + w0_ref[...]
    c0 += 2 * DECAY_RANK
    icl = _dot(u[:, c0:c0 + 2 * ICLR_RANK].astype(BF16), a2_ref[...]) + a0_ref[...]
    c0 += 2 * ICLR_RANK
    g_ref[...] = _dot(_sigmoid(u[:, c0:c0 + GATE_RANK]).astype(BF16), gup_ref[...]).astype(g_ref.dtype)

    bs = bs_ref[...]

    def head_sum(x):
        xb = x.astype(BF16)
        return jnp.concatenate(
            [_dot(xb[:, s * HGROUP:(s + 1) * HGROUP], bs) for s in range(w // HGROUP)], axis=1
        ) * float(HEAD_DIM)

    kk = k * kk_ref[...]
    kk = kk * lax.rsqrt(jnp.maximum(head_sum(kk * kk), 1e-24))
    r_ref[...] = r.astype(r_ref.dtype)
    v_ref[...] = v.astype(v_ref.dtype)
    a_ref[...] = (-kk * valid).astype(a_ref.dtype)
    bonus = jnp.zeros_like(r)
    for d, (lw_ref, kd_ref, b_ref) in enumerate(((lwf_ref, kdf_ref, bf_ref), (lwb_ref, kdb_ref, bb_ref))):
        x = -dec[:, d * w:(d + 1) * w]
        softplus = jnp.maximum(x, 0.0) + jnp.log(1.0 + jnp.exp(-jnp.abs(x)))
        lw_ref[...] = -jnp.exp(-softplus - 0.5)
        gate = _sigmoid(icl[:, d * w:(d + 1) * w])
        kd = k * (1.0 + (gate - 1.0) * ka_ref[...])
        kd_ref[...] = (kd * valid).astype(kd_ref.dtype)
        b_ref[...] = (kk * gate * valid).astype(b_ref.dtype)
        bonus = bonus + head_sum(r * kd * rk_ref[...]) * v
    bonus_ref[...] = bonus


def _scan_kernel(fwd_blk, bwd_blk, first,
                 rf, vf, af, lwf, kdf, bf_, rb, vb, ab, lwb, kdb, bb_,
                 of_ref, ob_ref, s_scr, *, nk):
    del fwd_blk, bwd_blk
    step = pl.program_id(0)
    c = SCAN_CHUNK

    @pl.when(first[step] == 1)
    def _():
        s_scr[...] = jnp.zeros(s_scr.shape, F32)

    t_sq = lax.broadcasted_iota(jnp.int32, (nk * c, nk * c), 0)
    s_sq = lax.broadcasted_iota(jnp.int32, (nk * c, nk * c), 1)
    t_cat = lax.broadcasted_iota(jnp.int32, (c, HGROUP), 0)
    s_cat = lax.broadcasted_iota(jnp.int32, (c, HGROUP), 1) & (c - 1)
    bd_row = lax.broadcasted_iota(jnp.int32, (HGROUP, HGROUP), 0) >> 6
    bd_col = lax.broadcasted_iota(jnp.int32, (HGROUP, HGROUP), 1) >> 6
    bd_mask = bd_row == bd_col
    eye_cat = jnp.where(t_cat == s_cat, 1.0, 0.0).astype(F32)

    def block_diag(x):
        return jnp.where(bd_mask, jnp.concatenate([x] * (HGROUP // c), axis=0), 0.0).astype(BF16)

    chains = []
    dirs = ((0, rf, vf, af, lwf, kdf, bf_), (1, rb, vb, ab, lwb, kdb, bb_))
    for d, r_ref, v_ref, a_ref, lw_ref, kd_ref, b_ref in dirs:
        rev = d == 1
        earlier = (s_sq >= t_sq) if rev else (s_sq <= t_sq)
        tri = jnp.where((t_sq >> 6) == (s_sq >> 6), jnp.where(earlier, 1.0, 0.0), 0.0).astype(BF16)
        lw = lw_ref[...]
        h1, h2 = _split2(lw)
        cum = _dot(tri, h1) + _dot(tri, h2)
        e_in = jnp.exp(cum)
        e_neg = jnp.exp(-cum)
        a_t = a_ref[...] * jnp.exp(cum - lw)
        r_t = r_ref[...] * e_in
        b_t = b_ref[...] * e_neg
        k_t = kd_ref[...] * e_neg
        v_all = v_ref[...]
        for ck in range(nk):
            rows = slice(ck * c, (ck + 1) * c)
            w_row = ck * c if rev else (ck + 1) * c - 1
            for g in range(D_WIDTH // HGROUP):
                cs = slice(g * HGROUP, (g + 1) * HGROUP)
                chains.append(dict(d=d, ck=ck, g=g, rev=rev, at=a_t[rows, cs], rt=r_t[rows, cs],
                                   bt=b_t[rows, cs], kt=k_t[rows, cs], vc=v_all[rows, cs],
                                   w=e_in[w_row:w_row + 1, cs]))

    def strict(ch):
        return (s_cat > t_cat) if ch['rev'] else (s_cat < t_cat)

    def incl(ch):
        return (s_cat >= t_cat) if ch['rev'] else (s_cat <= t_cat)

    for ch in chains:
        ch['lhs'] = jnp.concatenate([ch['at'], ch['rt']], axis=0).astype(BF16)
        ch['bd_v'] = block_diag(ch['vc'])
    m_b = [_dot_nt(ch['lhs'], block_diag(ch['bt'])) for ch in chains]
    m_k = [_dot_nt(ch['lhs'], block_diag(ch['kt'])) for ch in chains]
    for ch, mb, mk in zip(chains, m_b, m_k):
        ch['a_ab'] = jnp.where(strict(ch), mb[:c], 0.0)
        ch['a_rb'] = jnp.where(incl(ch), mb[c:], 0.0).astype(BF16)
        ch['a_ak'] = jnp.where(strict(ch), mk[:c], 0.0).astype(BF16)
        ch['a_rk'] = jnp.where(incl(ch), mk[c:], 0.0).astype(BF16)
        ch['t'] = eye_cat + jnp.where((t_cat >> 1) == (s_cat >> 1), ch['a_ab'], 0.0)
    m = 2
    while m < c:
        sh = int(math.log2(m))
        xs = []
        for ch in chains:
            a_m = jnp.where((t_cat >> (sh + 1)) == (s_cat >> (sh + 1)),
                            jnp.where((t_cat >> sh) != (s_cat >> sh), ch['a_ab'], 0.0), 0.0)
            xs.append(_dot(a_m.astype(BF16), block_diag(ch['t'])))
        for ch, x in zip(chains, xs):
            ch['t'] = ch['t'] + _dot(ch['t'].astype(BF16), block_diag(x))
        m *= 2
    av = [_dot(jnp.concatenate([ch['a_ak'], ch['a_rk']], axis=0), ch['bd_v']) for ch in chains]
    akv = [x[:c] for x in av]
    for ch in chains:
        ch['t'] = ch['t'].astype(BF16)
    a_hat = [_dot(ch['t'], block_diag(ch['at'])) for ch in chains]
    u_hat = [_dot(ch['t'], block_diag(x)) for ch, x in zip(chains, akv)]
    r_hat = [(ch['rt'] + _dot(ch['a_rb'], block_diag(x))).astype(BF16) for ch, x in zip(chains, a_hat)]
    o_hat = [_dot(ch['a_rb'], block_diag(x)) + y[c:] for ch, x, y in zip(chains, u_hat, av)]
    m_c = [jnp.where(bd_mask, _dot_tn(x.astype(BF16), ch['bt'].astype(BF16)), 0.0).astype(BF16)
           for ch, x in zip(chains, a_hat)]
    n_c = [jnp.where(bd_mask, _dot_tn(jnp.concatenate([x, ch['vc']], axis=0).astype(BF16),
                                      jnp.concatenate([ch['bt'], ch['kt']], axis=0).astype(BF16)), 0.0)
           for ch, x in zip(chains, u_hat)]
    pre = {(ch['d'], ch['ck'], ch['g']): (r_hat[i], o_hat[i], m_c[i], n_c[i], ch['w'])
           for i, ch in enumerate(chains)}

    groups = [(d, g) for d in range(2) for g in range(D_WIDTH // HGROUP)]
    states = {dg: s_scr[dg[0], dg[1]] for dg in groups}
    for i in range(nk):
        st_b = {dg: states[dg].astype(BF16) for dg in groups}
        for d, g in groups:
            ck = nk - 1 - i if d == 1 else i
            rh, oh, mc, nc_, w = pre[(d, ck, g)]
            o_ref = ob_ref if d == 1 else of_ref
            o_ref[ck * c:(ck + 1) * c, g * HGROUP:(g + 1) * HGROUP] = _dot_nt(rh, st_b[(d, g)]) + oh
            states[(d, g)] = (states[(d, g)] + _dot(st_b[(d, g)], mc) + nc_) * w
    for d, g in groups:
        s_scr[d, g] = states[(d, g)]


def _scan_tables(segments, rows):
    fwd, bwd, first = [], [], []
    for start, bsz, lp in segments:
        nc = lp // rows
        for b in range(bsz):
            base = (start + b * lp) // rows
            for ci in range(nc):
                fwd.append(base + ci)
                bwd.append(base + nc - 1 - ci)
                first.append(1 if ci == 0 else 0)
    as_i32 = lambda t: jnp.asarray(np.asarray(t, np.int32))
    return as_i32(fwd), as_i32(bwd), as_i32(first)


def _scan(segments, n_rows, r, v, a, lwf, lwb, kdf, kdb, bf_, bb_):
    nk = SCAN_CHUNKS_PER_STEP
    rows = nk * SCAN_CHUNK
    if any(start % rows or lp % rows for start, _, lp in segments):
        nk, rows = 1, SCAN_CHUNK
    fwd, bwd, first = _scan_tables(segments, rows)
    nsteps = fwd.shape[0]
    blk = (rows, D_WIDTH)
    fspec = pl.BlockSpec(blk, lambda s, fw, bw, fi: (fw[s], 0))
    bspec = pl.BlockSpec(blk, lambda s, fw, bw, fi: (bw[s], 0))
    grid_spec = pltpu.PrefetchScalarGridSpec(
        num_scalar_prefetch=3, grid=(nsteps,),
        in_specs=[fspec] * 6 + [bspec] * 6,
        out_specs=[fspec, bspec],
        scratch_shapes=[pltpu.VMEM((2, D_WIDTH // HGROUP, HGROUP, HGROUP), F32)],
    )
    return pl.pallas_call(
        functools.partial(_scan_kernel, nk=nk), grid_spec=grid_spec,
        out_shape=[jax.ShapeDtypeStruct((n_rows, D_WIDTH), F32)] * 2,
        compiler_params=_cparams(("arbitrary",)), name="wkv_scan",
    )(fwd, bwd, first, r, v, a, lwf, kdf, bf_, r, v, a, lwb, kdb, bb_)


def _rope_tables(segments, n_rows, layer_kind):
    outs = []
    for start, bsz, lp in segments:
        p = jnp.arange(lp)
        d = jnp.arange(HEAD_DIM)
        if layer_kind == "axial":
            t = p - FRONT
            row = jnp.where(t >= 0, t // GRID_W, jnp.where(p >= PAD, t, 0)).astype(F32)
            col = jnp.where(t >= 0, t % GRID_W, jnp.where(p >= PAD, t, 0)).astype(F32)
            half = HEAD_DIM // 2
            inv = A_THETA ** (-jnp.arange(0, half, 2, dtype=F32) / half)
            ang = jnp.concatenate([row[:, None] * inv] * 2 + [col[:, None] * inv] * 2, axis=1)
            first = (d % 32) < 16
            cos = jnp.cos(ang)
            sa = jnp.where(first[None, :], -jnp.sin(ang), 0.0)
            sb = jnp.where(first[None, :], 0.0, jnp.sin(ang))
        else:
            pos = jnp.maximum(p - PAD, 0).astype(F32)
            inv = ROPE_THETA ** (-jnp.arange(0, ROPE_DIMS, 2, dtype=F32) / ROPE_DIMS)
            ang8 = pos[:, None] * inv
            ang = jnp.concatenate([ang8, ang8] + [jnp.zeros_like(ang8)] * 6, axis=1)
            cos = jnp.where((d < ROPE_DIMS)[None, :], jnp.cos(ang), 1.0)
            sa = jnp.where((d < 8)[None, :], -jnp.sin(ang), 0.0)
            sb = jnp.where(((d >= 8) & (d < 16))[None, :], jnp.sin(ang), 0.0)
        small = lax.optimization_barrier([jnp.concatenate([t_, t_], axis=1) for t_ in (cos, sa, sb)])
        tabs = [jnp.tile(t_, (bsz, 1)) for t_ in small]
        outs.append(tabs)
    tail = n_rows - sum(b * lp for _, b, lp in segments)
    res = []
    for idx in range(3):
        parts = [o[idx] for o in outs] + [jnp.zeros((tail, LANES), F32)]
        res.append(jnp.concatenate(parts, axis=0))
    return res


def _seq_view(flat, seg, width):
    start, bsz, lp = seg
    return flat[start:start + bsz * lp].reshape(bsz, lp, width)


def _attn_operands(qT, k, vT, seg):
    start, bsz, lp = seg
    chunk = vT.shape[2]
    k_seg = k[start:start + bsz * lp].reshape(bsz, lp, k.shape[1])
    v_seg = vT[start // chunk:(start + bsz * lp) // chunk].reshape(bsz, lp // chunk, vT.shape[1], chunk)
    return qT, k_seg, v_seg, start


def _to_flat(parts, n_rows, width, dtype):
    used = sum(p.shape[0] for p in parts)
    return jnp.concatenate(parts + [jnp.zeros((n_rows - used, width), dtype)], axis=0)


def _block_avg(n, group):
    idx = np.arange(n) // group
    return jnp.asarray((idx[:, None] == idx[None, :]).astype(np.float32) / group).astype(BF16)


def _forward(xs, p):
    segments = []
    start = 0
    for x in xs:
        bsz, s, _ = x.shape
        lp = s + FRONT
        segments.append((start, bsz, lp))
        start += bsz * lp
    n_used = start
    tm = 512 if n_used >= 4096 else ATTN_CHUNK
    n_rows = _round_up(n_used, tm)

    meta = p['meta_tokens'].astype(F32)
    parts = []
    for x in xs:
        bsz = x.shape[0]
        lead = jnp.concatenate([jnp.zeros((PAD, D_MODEL), F32), meta], axis=0)
        parts.append(jnp.concatenate([jnp.broadcast_to(lead[None], (bsz, FRONT, D_MODEL)), x], axis=1)
                     .reshape(-1, D_MODEL))
    h = _to_flat(parts, n_rows, D_MODEL, F32)

    bs = _block_avg(HGROUP, HEAD_DIM)
    row = lambda t: t.reshape(1, -1).astype(F32)

    depth = p['pre_mix_g'].shape[0]
    for i in range(depth):
        if i % 2 == 0:
            e = i // 2
            cos, sa, sb = _rope_tables(segments, n_rows, "axial")
            ch = np.arange(B_DIM)
            ang = 2.0 * np.pi * ((ch[:, None] * ch[None, :]) % B_DIM) / B_DIM
            cc, sc = jnp.asarray(np.cos(ang), F32), jnp.asarray(np.sin(ang), F32)
            wl = p['b_w'][e].astype(F32)
            hp = lax.Precision.HIGHEST
            pmat = jnp.einsum('cd,gde->gce', cc, wl, precision=hp)
            qmat = -jnp.einsum('cd,gde->gce', sc, wl, precision=hp)
            bd = lambda m: jax.scipy.linalg.block_diag(*[m[g] for g in range(B_GROUPS)])
            pq_base = jnp.concatenate([bd(pmat), bd(qmat)], axis=1)
            qT, k, vT, gf = _row_call(
                "in_even", _in_even_kernel, n_rows, tm, [h, cos, sa, sb],
                [row(p['pre_mix_g'][i]), p['even_w_in'][e].astype(BF16),
                 row(jnp.tile(p['a_q_gain'][e], A_HEADS) * (HEAD_DIM ** -0.5 * LOG2_E)),
                 row(jnp.tile(p['a_k_gain'][e], A_KV_HEADS)), row(p['b_norm_g'][e]), bs,
                 pq_base.astype(BF16)],
                [("cols", A_Q, BF16), ("rows", A_KV, BF16), ("chunks", A_KV, BF16), ("rows", 2 * B_W, F32)])
            ya_parts, yb_parts = [], []
            for seg in segments:
                _, bsz, lp = seg
                ya_parts.append(_attn_a(*_attn_operands(qT, k, vT, seg)))
                gseq = _seq_view(gf, seg, 2 * B_W)[:, PAD:]
                yb = _mixer_b(gseq, row(p['b_b'][e]))
                yb_parts.append(jnp.pad(yb, ((0, 0), (PAD, 0), (0, 0))).reshape(bsz * lp, B_W).astype(BF16))
            yat = ya_parts
            mix_kernel, mix_rows, mix_consts = _mix_ffn_even_kernel, [_to_flat(yb_parts, n_rows, B_W, BF16)], []
            w_out = p['even_w_out'][e].astype(BF16)
            wa, wb = w_out[:A_Q], w_out[A_Q:]
        else:
            o = i // 2
            cos, sa, sb = _rope_tables(segments, n_rows, "partial")
            qT, k, vT, u = _row_call(
                "in_odd", _in_odd_kernel, n_rows, tm, [h, cos, sa, sb],
                [row(p['pre_mix_g'][i]), p['odd_w_in'][o].astype(BF16)],
                [("cols", C_Q, BF16), ("rows", C_KV, BF16), ("chunks", C_KV, BF16), ("rows", D_IN, F32)])
            sink = p['c_sink'][o].astype(F32)
            yat = [_attn_c(sink, *_attn_operands(qT, k, vT, seg)) for seg in segments]
            mix_kernel = _mix_ffn_odd_kernel
            mix_rows, mix_consts = _mixer_d(u, p, o, bs, segments, n_rows, tm)
            w_out = p['odd_w_out'][o].astype(BF16)
            wa, wb = w_out[:C_Q], w_out[C_Q:]
        (h,) = _row_call(
            "mix_ffn",
            functools.partial(mix_kernel, chunk=256, group_tiles=tuple(s[0] // tm for s in segments)),
            n_rows, tm, [h] + mix_rows,
            mix_consts + [wa, wb, row(p['post_mix_g'][i]),
             row(p['pre_ffn_g'][i]), p['ffn_w_gate'][i].astype(BF16), p['ffn_w_up'][i].astype(BF16),
             p['ffn_w_down'][i].astype(BF16), row(p['post_ffn_g'][i])],
            [("rows", D_MODEL, F32)], scratch=[pltpu.VMEM((tm, D_MODEL), F32)], col_ins=yat)

    outs = []
    for seg in segments:
        outs.append(_seq_view(h, seg, D_MODEL)[:, FRONT:])
    return tuple(outs)


def _mixer_d(u, p, o, bs, segments, n_rows, tm):
    row = lambda t: t.reshape(1, -1).astype(F32)
    w = D_WIDTH
    zeros = lambda r: jnp.zeros((r, w), F32)
    w2 = jnp.concatenate([jnp.concatenate([p['d_w_up'][o][0], zeros(DECAY_RANK)], axis=1),
                          jnp.concatenate([zeros(DECAY_RANK), p['d_w_up'][o][1]], axis=1)], axis=0)
    a2 = jnp.concatenate([jnp.concatenate([p['d_a_up'][o][0], zeros(ICLR_RANK)], axis=1),
                          jnp.concatenate([zeros(ICLR_RANK), p['d_a_up'][o][1]], axis=1)], axis=0)
    consts = [row(p['d_mu_prev'][o]), row(p['d_mu_next'][o]), w2.astype(BF16), a2.astype(BF16),
              p['d_g_up'][o].astype(BF16), row(p['d_w0'][o]), row(p['d_a0'][o]),
              row(p['d_k_k'][o]), row(p['d_k_a'][o]), row(p['d_r_k'][o]), bs]
    nb8 = n_rows // 8
    t8 = tm // 8
    in_specs = [pl.BlockSpec((tm, D_IN), lambda i: (i, 0)),
                pl.BlockSpec((8, D_IN), lambda i: (jnp.maximum(i * t8 - 1, 0), 0)),
                pl.BlockSpec((8, D_IN), lambda i: (jnp.minimum((i + 1) * t8, nb8 - 1), 0))]
    in_specs += [pl.BlockSpec(a.shape, lambda i, nd=a.ndim: (0,) * nd) for a in consts]
    out_dtypes = [BF16, BF16, BF16, F32, F32, BF16, BF16, BF16, BF16, F32, BF16]
    prep = pl.pallas_call(
        functools.partial(_d_prep_kernel, tm=tm, segments=tuple(segments)),
        grid=(n_rows // tm,), in_specs=in_specs,
        out_specs=[pl.BlockSpec((tm, w), lambda i: (i, 0))] * len(out_dtypes),
        out_shape=[jax.ShapeDtypeStruct((n_rows, w), dt) for dt in out_dtypes],
        compiler_params=_cparams(("parallel",)), name="d_prep",
    )(u, u, u, *consts)
    r, v, a, lwf, lwb, kdf, kdb, bf_, bb_, bonus, g = prep
    of, ob = _scan(segments, n_rows, r, v, a, lwf, lwb, kdf, kdb, bf_, bb_)
    return [of, ob, bonus, g], [row(p['d_ln_g'][o]), row(p['d_ln_b'][o]), bs]


def kernel(x_prompt, x_sample, meta_tokens, pre_mix_g, post_mix_g, pre_ffn_g, post_ffn_g, even_w_in, even_w_out, a_q_gain, a_k_gain, b_norm_g, b_w, b_b, odd_w_in, odd_w_out, c_sink, d_mu_prev, d_mu_next, d_w0, d_w_up, d_a0, d_a_up, d_g_up, d_k_k, d_k_a, d_r_k, d_ln_g, d_ln_b, ffn_w_gate, ffn_w_up, ffn_w_down):
    params = dict(meta_tokens=meta_tokens, pre_mix_g=pre_mix_g, post_mix_g=post_mix_g,
                  pre_ffn_g=pre_ffn_g, post_ffn_g=post_ffn_g,
                  even_w_in=even_w_in, even_w_out=even_w_out, a_q_gain=a_q_gain, a_k_gain=a_k_gain,
                  b_norm_g=b_norm_g, b_w=b_w, b_b=b_b,
                  odd_w_in=odd_w_in, odd_w_out=odd_w_out, c_sink=c_sink,
                  d_mu_prev=d_mu_prev, d_mu_next=d_mu_next, d_w0=d_w0, d_w_up=d_w_up,
                  d_a0=d_a0, d_a_up=d_a_up, d_g_up=d_g_up, d_k_k=d_k_k, d_k_a=d_k_a, d_r_k=d_r_k,
                  d_ln_g=d_ln_g, d_ln_b=d_ln_b,
                  ffn_w_gate=ffn_w_gate, ffn_w_up=ffn_w_up, ffn_w_down=ffn_w_down)
    return _forward([x_prompt, x_sample], params)
```

```python
import functools
import math

import numpy as np
import jax
import jax.numpy as jnp
from jax import lax
from jax.experimental import pallas as pl
from jax.experimental.pallas import tpu as pltpu

F32 = jnp.float32
BF16 = jnp.bfloat16

D_MODEL = 1024
HEAD_DIM = 64
N_META = 16
GRID_W = 64
WINDOW = 128
RMS_EPS = 1e-6
A_HEADS, A_KV_HEADS, A_THETA = 12, 4, 10000.0
B_GROUPS, B_DIM = 4, 64
C_HEADS, C_KV_HEADS = 8, 2
ROPE_THETA = 500000.0
ROPE_DIMS = HEAD_DIM // 4
D_HEADS = 8
D_WIDTH = D_HEADS * HEAD_DIM
DECAY_RANK, ICLR_RANK, GATE_RANK = 64, 64, 128
LNX_EPS = 64e-5
D_FF = 2816
A_Q, A_KV, B_W = A_HEADS * HEAD_DIM, A_KV_HEADS * HEAD_DIM, B_GROUPS * B_DIM
C_Q, C_KV = C_HEADS * HEAD_DIM, C_KV_HEADS * HEAD_DIM
D_IN = 3 * D_WIDTH + 2 * DECAY_RANK + 2 * ICLR_RANK + GATE_RANK

LANES = 128
FRONT = 256
PAD = FRONT - N_META
NEG = -1e30
SCAN_CHUNK = 64
SCAN_CHUNKS_PER_STEP = 4
HGROUP = 256
FFT_L1 = 16
VMEM_LIMIT = 56 * 1024 * 1024
WIN_QBLOCKS = 1
ROW_SUBTILE = 256
ATTN_CHUNK = 256
ATTN_MAX_CHUNKS = 20
ATTN_UNROLL = 16
LOG2_E = math.log2(math.e)


def _round_up(x, m):
    return (x + m - 1) // m * m


def _cparams(sem):
    return pltpu.CompilerParams(dimension_semantics=sem, vmem_limit_bytes=VMEM_LIMIT)


def _dot(a, b):
    return jnp.dot(a, b, preferred_element_type=F32)


def _dot_nt(a, b):
    return lax.dot_general(a, b, (((1,), (1,)), ((), ())), preferred_element_type=F32)


def _dot_tn(a, b):
    return lax.dot_general(a, b, (((0,), (0,)), ((), ())), preferred_element_type=F32)


def _split2(x):
    hi = x.astype(BF16)
    lo = (x - hi.astype(F32)).astype(BF16)
    return hi, lo


def _group_mean(x, bs):
    hi, lo = _split2(x)
    return _dot(hi, bs) + _dot(lo, bs)


def _rope(x, cos, sin_a, sin_b, shift):
    n = x.shape[1]
    return x * cos + pltpu.roll(x, n - shift, 1) * sin_a + pltpu.roll(x, shift, 1) * sin_b


def _wide(t, n):
    return t if n == LANES else jnp.concatenate([t] * (n // LANES), axis=1)


def _rms_rows(x, g):
    ms = jnp.mean(x * x, axis=-1, keepdims=True)
    return x * lax.rsqrt(ms + RMS_EPS) * g


def _in_even_kernel(h_ref, cos_ref, sa_ref, sb_ref, g_ref, w_ref, qg_ref, kg_ref, bg_ref, bs_ref, pq_ref,
                    q_ref, k_ref, v_ref, gf_ref):
    bs = bs_ref[...]
    for r in range(h_ref.shape[0] // ROW_SUBTILE):
        rows = slice(r * ROW_SUBTILE, (r + 1) * ROW_SUBTILE)
        hn = _rms_rows(h_ref[rows, :], g_ref[...]).astype(BF16)
        proj = _dot(hn, w_ref[...])
        cos, sa, sb = (_wide(t[rows, :], HGROUP) for t in (cos_ref, sa_ref, sb_ref))

        def norm_rope(x, gain):
            xn = x * lax.rsqrt(_group_mean(x * x, bs) + RMS_EPS) * gain
            return _rope(xn, cos, sa, sb, 16)

        for s in range(A_Q // HGROUP):
            cs = slice(s * HGROUP, (s + 1) * HGROUP)
            q_ref[cs, rows] = norm_rope(proj[:, cs], qg_ref[:, cs]).T.astype(BF16)
        k_ref[rows, :] = norm_rope(proj[:, A_Q:A_Q + A_KV], kg_ref[...]).astype(BF16)
        v_ref[r] = proj[:, A_Q + A_KV:A_Q + 2 * A_KV].T.astype(BF16)
        f = proj[:, A_Q + 2 * A_KV:]
        fn = f * lax.rsqrt(_group_mean(f * f, bs) + RMS_EPS) * bg_ref[...]
        gf_ref[rows, :] = _dot(fn.astype(BF16), pq_ref[...])


def _in_odd_kernel(h_ref, cos_ref, sa_ref, sb_ref, g_ref, w_ref, q_ref, k_ref, v_ref, u_ref):
    for r in range(h_ref.shape[0] // ROW_SUBTILE):
        rows = slice(r * ROW_SUBTILE, (r + 1) * ROW_SUBTILE)
        hn = _rms_rows(h_ref[rows, :], g_ref[...]).astype(BF16)
        proj = _dot(hn, w_ref[...])
        cos, sa, sb = cos_ref[rows, :], sa_ref[rows, :], sb_ref[rows, :]
        cos2, sa2, sb2 = (_wide(t, HGROUP) for t in (cos, sa, sb))
        for s in range(C_Q // HGROUP):
            cs = slice(s * HGROUP, (s + 1) * HGROUP)
            q_ref[cs, rows] = (_rope(proj[:, cs], cos2, sa2, sb2, 8)
                               * (HEAD_DIM ** -0.5 * LOG2_E)).T.astype(BF16)
        k_ref[rows, :] = _rope(proj[:, C_Q:C_Q + C_KV], cos, sa, sb, 8).astype(BF16)
        v = proj[:, C_Q + C_KV:C_Q + 2 * C_KV]
        for c in range(ROW_SUBTILE // C_KV):
            v_ref[r * (ROW_SUBTILE // C_KV) + c] = v[c * C_KV:(c + 1) * C_KV, :].T.astype(BF16)
        u_ref[rows, :] = proj[:, C_Q + 2 * C_KV:]


def _mix_ffn_even_kernel(h_ref, yb_ref, *rest, **static):
    _mix_ffn(h_ref, yb_ref[...], *rest, **static)


def _mix_ffn_odd_kernel(h_ref, of_ref, ob_ref, bonus_ref, gate_ref, lng_ref, lnb_ref, bs_ref, *rest, **static):
    o = of_ref[...] + ob_ref[...]
    bs = bs_ref[...]

    def head_mean(x):
        return jnp.concatenate(
            [_group_mean(x[:, s * HGROUP:(s + 1) * HGROUP], bs) for s in range(D_WIDTH // HGROUP)], axis=1)

    cen = o - head_mean(o)
    var = head_mean(cen * cen)
    y = cen * lax.rsqrt(var + LNX_EPS) * lng_ref[...] + lnb_ref[...] + bonus_ref[...]
    _mix_ffn(h_ref, (y * gate_ref[...]).astype(BF16), *rest, **static)


def _mix_ffn(h_ref, yb, wa_ref, wb_ref, gm_ref, g1_ref, wg_ref, wu_ref, wd_ref, g2_ref, *rest,
             chunk, group_tiles):
    *yat_refs, o_ref, acc_ref = rest
    yat = yat_refs[0][...]
    for ref, first_tile in zip(yat_refs[1:], group_tiles[1:]):
        yat = jnp.where(pl.program_id(0) >= first_tile, ref[...], yat)
    mix = _dot_tn(yat, wa_ref[...]) + _dot(yb, wb_ref[...])
    h = h_ref[...] + _rms_rows(mix, gm_ref[...])
    hn = _rms_rows(h, g1_ref[...]).astype(BF16)
    for c in range(D_FF // chunk):
        cs = slice(c * chunk, (c + 1) * chunk)
        gate = _dot(hn, wg_ref[:, cs])
        up = _dot(hn, wu_ref[:, cs])
        act = (gate * (1.0 / (1.0 + jnp.exp(-gate))) * up).astype(BF16)
        part = _dot(act, wd_ref[cs, :])
        if c == 0:
            acc_ref[...] = part
        else:
            acc_ref[...] += part
    o_ref[...] = h + _rms_rows(acc_ref[...], g2_ref[...])


def _row_call(name, kernel, n_rows, tm, row_ins, const_ins, outs, scratch=(), col_ins=()):
    grid = (n_rows // tm,)
    in_specs = [pl.BlockSpec((tm, a.shape[1]), lambda i: (i, 0)) for a in row_ins]
    in_specs += [pl.BlockSpec(a.shape, lambda i, nd=a.ndim: (0,) * nd, pipeline_mode=pl.Buffered(1))
                 for a in const_ins]
    first_tile = 0
    for a in col_ins:
        n_tiles = -(-a.shape[1] // tm)
        in_specs.append(pl.BlockSpec(
            (a.shape[0], tm), lambda i, lo=first_tile, n=n_tiles: (0, jnp.clip(i - lo, 0, n - 1))))
        first_tile += a.shape[1] // tm
    out_specs, out_shape = [], []
    for kind, n, dt in outs:
        if kind == "rows":
            out_specs.append(pl.BlockSpec((tm, n), lambda i: (i, 0)))
            out_shape.append(jax.ShapeDtypeStruct((n_rows, n), dt))
        elif kind == "cols":
            out_specs.append(pl.BlockSpec((n, tm), lambda i: (0, i)))
            out_shape.append(jax.ShapeDtypeStruct((n, n_rows), dt))
        else:
            out_specs.append(pl.BlockSpec((tm // n, n, n), lambda i: (i, 0, 0)))
            out_shape.append(jax.ShapeDtypeStruct((n_rows // n, n, n), dt))
    return pl.pallas_call(
        kernel, grid=grid, in_specs=in_specs, out_specs=out_specs, out_shape=out_shape,
        scratch_shapes=list(scratch), compiler_params=_cparams(("parallel",)), name=name,
    )(*row_ins, *const_ins, *col_ins)


def _padded_q(q_ref, h, kv):
    qh = q_ref[h * HEAD_DIM:(h + 1) * HEAD_DIM, :]
    zero = jnp.zeros_like(qh)
    return jnp.concatenate([qh, zero] if kv % 2 == 0 else [zero, qh], axis=0)


def _next_step(b, i, j, bsz, nq, nkb):
    j1 = j + 1
    wrap_j = (j1 == nkb).astype(jnp.int32)
    i1 = i + wrap_j
    wrap_i = (i1 == nq).astype(jnp.int32)
    return jnp.minimum(b + wrap_i, bsz - 1), i1 * (1 - wrap_i), j1 * (1 - wrap_j)


def _attn_a_kernel(qT_ref, qn_ref, k_ref, kn_ref, vT_ref, o_ref, m_scr, l_scr, acc_scr, s_scr, mx_scr,
                   *, nchunk, nkb, nq, bsz, tq):
    b, i, j = pl.program_id(0), pl.program_id(1), pl.program_id(2)

    @pl.when(j == 0)
    def _():
        m_scr[...] = jnp.full(m_scr.shape, NEG, F32)
        l_scr[...] = jnp.zeros(l_scr.shape, F32)
        acc_scr[...] = jnp.zeros(acc_scr.shape, F32)

    group = A_HEADS // A_KV_HEADS
    ck = ATTN_CHUNK
    row_iota = lax.broadcasted_iota(jnp.int32, (ck, tq), 0)
    key_row = row_iota + j * (nchunk * ck)
    key_row_next = row_iota + _next_step(b, i, j, bsz, nq, nkb)[2] * (nchunk * ck)

    def fold(s, op):
        return op(s.reshape(ck // 8, 8, tq), axis=0)

    def gang_q(g):
        if g == A_KV_HEADS:
            return [_padded_q(qn_ref, t, 0) for t in range(group)]
        return [_padded_q(qT_ref, g * group + t, g) for t in range(group)]

    def gang_scores(g, qps, c, mxs, first=False):
        rows = pl.ds(pl.multiple_of(c * ck, ck), ck)
        if g == A_KV_HEADS:
            kc, krow = kn_ref[0, rows, :], key_row_next
        else:
            kc, krow = k_ref[0, rows, (g // 2) * LANES:(g // 2 + 1) * LANES], key_row
        out = []
        for t in range(group):
            s = _dot(kc, qps[t])
            if first:
                s = jnp.where(krow >= PAD, s, NEG)
            s_scr[g % 2, t, c] = s
            out.append(fold(s, jnp.max) if mxs is None else jnp.maximum(mxs[t], fold(s, jnp.max)))
        return tuple(out)

    def gang_values(g, c, m_news):
        vc = vT_ref[0, c, g * HEAD_DIM:(g + 1) * HEAD_DIM, :]
        ps = [jnp.exp2(s_scr[g % 2, t, c] - m_news[t]) for t in range(group)]
        return tuple(_dot(vc, p.astype(BF16)) for p in ps), tuple(fold(p, jnp.sum) for p in ps)

    def add(xs, ys):
        return tuple(x + y for x, y in zip(xs, ys))

    unroll = max(u for u in range(1, ATTN_UNROLL + 1) if max(nchunk - 1, 1) % u == 0)

    @pl.when((b == 0) & (i == 0) & (j == 0))
    def _():
        qps = gang_q(0)
        mxs = lax.fori_loop(1, nchunk, lambda c, mxs: gang_scores(0, qps, c, mxs),
                            gang_scores(0, qps, 0, None, first=True), unroll=unroll)
        for t in range(group):
            mx_scr[t] = mxs[t]

    mxs = tuple(mx_scr[t] for t in range(group))
    for g in range(A_KV_HEADS):
        heads = [g * group + t for t in range(group)]
        m_olds = [m_scr[h] for h in heads]
        m_news = [jnp.maximum(mo, jnp.max(mx, axis=0, keepdims=True)) for mo, mx in zip(m_olds, mxs)]
        qps = gang_q(g + 1)

        def body(c, carry, g=g, m_news=m_news, qps=qps):
            accs, lsums, mxn = carry
            mxn = gang_scores(g + 1, qps, c, mxn)
            pvs, pss = gang_values(g, c, m_news)
            return add(accs, pvs), add(lsums, pss), mxn

        mx0 = gang_scores(g + 1, qps, 0, None, first=True)
        accs, lsums, mxs = lax.fori_loop(1, nchunk, body, (*gang_values(g, 0, m_news), mx0), unroll=unroll)
        for t, h in enumerate(heads):
            alpha = jnp.exp2(m_olds[t] - m_news[t])
            acc_scr[h] = acc_scr[h] * alpha + accs[t]
            l_scr[h] = l_scr[h] * alpha + jnp.sum(lsums[t], axis=0, keepdims=True)
            m_scr[h] = m_news[t]
    for t in range(group):
        mx_scr[t] = mxs[t]

    @pl.when(j == nkb - 1)
    def _():
        for h in range(A_HEADS):
            o_ref[h * HEAD_DIM:(h + 1) * HEAD_DIM, :] = (acc_scr[h] / l_scr[h]).astype(o_ref.dtype)


def _attn_a(qT, k, vT, start):
    bsz, lp = k.shape[:2]
    off = start // ATTN_CHUNK
    tq = ATTN_CHUNK
    total = lp // ATTN_CHUNK
    nkb = min(n for n in range(1, total + 1) if total % n == 0 and total // n <= ATTN_MAX_CHUNKS)
    tk = lp // nkb
    nchunk = tk // ATTN_CHUNK
    nq = lp // tq
    group = A_HEADS // A_KV_HEADS
    kernel = functools.partial(_attn_a_kernel, nchunk=nchunk, nkb=nkb, nq=nq, bsz=bsz, tq=tq)

    def next_q(b, i, j):
        b2, i2, _ = _next_step(b, i, j, bsz, nq, nkb)
        return 0, off + b2 * nq + i2

    def next_k(b, i, j):
        b2, _, j2 = _next_step(b, i, j, bsz, nq, nkb)
        return b2, j2, 0

    return pl.pallas_call(
        kernel, grid=(bsz, nq, nkb),
        in_specs=[
            pl.BlockSpec((A_Q, tq), lambda b, i, j: (0, off + b * nq + i)),
            pl.BlockSpec((group * HEAD_DIM, tq), next_q),
            pl.BlockSpec((1, tk, A_KV), lambda b, i, j: (b, j, 0)),
            pl.BlockSpec((1, tk, LANES), next_k),
            pl.BlockSpec((1, nchunk, A_KV, ATTN_CHUNK), lambda b, i, j: (b, j, 0, 0)),
        ],
        out_specs=pl.BlockSpec((A_Q, tq), lambda b, i, j: (0, b * nq + i)),
        out_shape=jax.ShapeDtypeStruct((A_Q, bsz * lp), BF16),
        scratch_shapes=[
            pltpu.VMEM((A_HEADS, 1, tq), F32),
            pltpu.VMEM((A_HEADS, 1, tq), F32),
            pltpu.VMEM((A_HEADS, HEAD_DIM, tq), F32),
            pltpu.VMEM((2, group, nchunk, ATTN_CHUNK, tq), F32),
            pltpu.VMEM((group, 8, tq), F32),
        ],
        compiler_params=_cparams(("arbitrary", "arbitrary", "arbitrary")), name="attn_a",
    )(qT, qT, k, k, vT)


def _attn_c_kernel(sink_ref, qT_ref, *refs, lp, nslot):
    j = pl.program_id(1)
    k_refs, v_refs, o_ref = refs[:nslot], refs[nslot:2 * nslot], refs[2 * nslot]
    tq = WIN_QBLOCKS * LANES
    row = lax.broadcasted_iota(jnp.int32, (LANES, tq), 0)
    pq = lax.broadcasted_iota(jnp.int32, (LANES, tq), 1) + j * tq
    biases = []
    for slot in range(nslot):
        if slot == 0:
            bias = jnp.where(row >= PAD % LANES, 0.0, NEG)
        else:
            pk = row + (j * WIN_QBLOCKS + slot - 2) * LANES
            in_window = jnp.where(jnp.abs(pq - pk) <= WINDOW, 0.0, NEG)
            bias = jnp.where(pk >= FRONT, jnp.where(pk < lp, in_window, NEG), NEG)
        biases.append(bias.astype(F32))
    group = C_HEADS // C_KV_HEADS
    k_all = jnp.concatenate([k_refs[t][0] for t in range(nslot)], axis=0)
    bias_all = jnp.concatenate(biases, axis=0)
    v_all = [jnp.concatenate([v_refs[t][0, 0, kv * HEAD_DIM:(kv + 1) * HEAD_DIM, :] for t in range(nslot)],
                             axis=1) for kv in range(C_KV_HEADS)]
    scores = [_dot(k_all, _padded_q(qT_ref, h, h // group)) + bias_all for h in range(C_HEADS)]
    for h in range(C_HEADS):
        sink = sink_ref[h] * LOG2_E
        m = jnp.maximum(jnp.max(scores[h], axis=0, keepdims=True), sink)
        p = jnp.exp2(scores[h] - m)
        denom = jnp.exp2(sink - m) + jnp.sum(p, axis=0, keepdims=True)
        acc = _dot(v_all[h // group], p.astype(BF16))
        o_ref[h * HEAD_DIM:(h + 1) * HEAD_DIM, :] = (acc / denom).astype(o_ref.dtype)


def _attn_c(sink, qT, k, vT, start):
    bsz, lp = k.shape[:2]
    nb = lp // LANES
    tq = WIN_QBLOCKS * LANES
    nq = lp // tq
    off = start // tq
    nslot = WIN_QBLOCKS + 3
    kernel = functools.partial(_attn_c_kernel, lp=lp, nslot=nslot)

    def kspec(fn):
        return pl.BlockSpec((1, LANES, C_KV), lambda b, j: (b, fn(j), 0))

    def vspec(fn):
        return pl.BlockSpec((1, 1, C_KV, LANES), lambda b, j: (b, fn(j), 0, 0))

    def neighbour(s):
        return lambda j: jnp.clip(j * WIN_QBLOCKS + s - 2, 0, nb - 1)

    fns = [lambda j: PAD // LANES] + [neighbour(s) for s in range(1, nslot)]
    return pl.pallas_call(
        kernel, grid=(bsz, nq),
        in_specs=[pl.BlockSpec(memory_space=pltpu.SMEM),
                  pl.BlockSpec((C_Q, tq), lambda b, j: (0, off + b * nq + j))]
                 + [kspec(f) for f in fns] + [vspec(f) for f in fns],
        out_specs=pl.BlockSpec((C_Q, tq), lambda b, j: (0, b * nq + j)),
        out_shape=jax.ShapeDtypeStruct((C_Q, bsz * lp), BF16),
        compiler_params=_cparams(("parallel", "parallel")), name="attn_c",
    )(sink, qT, *([k] * nslot), *([vT] * nslot))


def _fft16(vals):
    n = FFT_L1
    bits = n.bit_length() - 1
    a = [vals[int(format(idx, '0%db' % bits)[::-1], 2)] for idx in range(n)]
    m = 2
    while m <= n:
        half = m // 2
        for base in range(0, n, m):
            for t in range(half):
                ur, ui = a[base + t]
                vr, vi = a[base + t + half]
                if 4 * t == m:
                    tr, ti = vi, -vr
                elif t > 0:
                    wr, wi = math.cos(-2.0 * math.pi * t / m), math.sin(-2.0 * math.pi * t / m)
                    tr, ti = vr * wr - vi * wi, vr * wi + vi * wr
                else:
                    tr, ti = vr, vi
                a[base + t] = (ur + tr, ui + ti)
                a[base + t + half] = (ur - tr, ui - ti)
        m *= 2
    return a


def _fft1_kernel(x_ref, o_ref):
    def strip(s, carry):
        rows = pl.ds(pl.multiple_of(s * 8, 8), 8)
        for q in range(B_W // LANES):
            re_cols = slice(q * LANES, (q + 1) * LANES)
            im_cols = slice(B_W + q * LANES, B_W + (q + 1) * LANES)
            vals = [(x_ref[0, n, rows, re_cols], x_ref[0, n, rows, im_cols]) for n in range(FFT_L1)]
            for kk, (re, im) in enumerate(_fft16(vals)):
                o_ref[0, kk, rows, re_cols] = re
                o_ref[0, kk, rows, im_cols] = im
        return carry

    lax.fori_loop(0, x_ref.shape[2] // 8, strip, 0)


def _fft2_kernel(a_ref, tc_ref, ts_ref, c2_ref, s2_ref, bias_ref, o_ref):
    a = a_ref[0, 0]
    tc, ts = _wide(tc_ref[0], B_W), _wide(ts_ref[0], B_W)
    are, aim = a[:, :B_W], a[:, B_W:]
    bre = (are * tc + aim * ts).astype(BF16)
    bim = (aim * tc - are * ts).astype(BF16)
    o_ref[0] = _dot(c2_ref[...], bre) + _dot(s2_ref[...], bim) + bias_ref[...]


def _dft_tables(l2, l2p, l2o):
    l1 = FFT_L1
    length = l1 * l2
    n1 = np.arange(l1)
    n2 = np.arange(l2)
    angt = 2.0 * np.pi * (n1[:, None] * n2[None, :]) / length
    tc = np.zeros((l1, l2p, LANES), np.float32)
    ts = np.zeros((l1, l2p, LANES), np.float32)
    tc[:, :l2, :] = np.cos(angt)[:, :, None]
    ts[:, :l2, :] = np.sin(angt)[:, :, None]
    ang2 = 2.0 * np.pi * ((n2[:, None] * n2[None, :]) % l2) / l2
    scale = 1.0 / math.sqrt(B_DIM * length)
    c2 = np.zeros((l2o, l2p), np.float32)
    s2 = np.zeros((l2o, l2p), np.float32)
    c2[:l2, :l2] = np.cos(ang2) * scale
    s2[:l2, :l2] = np.sin(ang2) * scale
    as_bf = lambda t: jnp.asarray(t, F32).astype(BF16)
    return jnp.asarray(tc), jnp.asarray(ts), as_bf(c2), as_bf(s2)


def _mixer_b(gseq, bias):
    bsz, length, _ = gseq.shape
    l1 = FFT_L1
    l2 = length // l1
    l2p, l2o = _round_up(l2, LANES), _round_up(l2, 8)
    tc, ts, c2, s2 = _dft_tables(l2, l2p, l2o)
    x = jnp.pad(gseq.reshape(bsz, l1, l2, 2 * B_W), ((0, 0), (0, 0), (0, l2p - l2), (0, 0)))
    blk = (1, l1, LANES, 2 * B_W)
    a = pl.pallas_call(
        _fft1_kernel, grid=(bsz, l2p // LANES),
        in_specs=[pl.BlockSpec(blk, lambda b, i: (b, 0, i, 0))],
        out_specs=pl.BlockSpec(blk, lambda b, i: (b, 0, i, 0)),
        out_shape=jax.ShapeDtypeStruct((bsz, l1, l2p, 2 * B_W), F32),
        compiler_params=_cparams(("parallel", "parallel")), name="fft1",
    )(x)
    y = pl.pallas_call(
        _fft2_kernel, grid=(bsz, l1),
        in_specs=[pl.BlockSpec((1, 1, l2p, 2 * B_W), lambda b, k: (b, k, 0, 0)),
                  pl.BlockSpec((1, l2p, LANES), lambda b, k: (k, 0, 0)),
                  pl.BlockSpec((1, l2p, LANES), lambda b, k: (k, 0, 0)),
                  pl.BlockSpec((l2o, l2p), lambda b, k: (0, 0)),
                  pl.BlockSpec((l2o, l2p), lambda b, k: (0, 0)),
                  pl.BlockSpec((1, B_W), lambda b, k: (0, 0))],
        out_specs=pl.BlockSpec((1, l2o, B_W), lambda b, k: (b, 0, k)),
        out_shape=jax.ShapeDtypeStruct((bsz, l2o, l1 * B_W), F32),
        compiler_params=_cparams(("parallel", "parallel")), name="fft2",
    )(a, tc, ts, c2, s2, bias)
    return y[:, :l2].reshape(bsz, length, B_W)


def _seq_position(rows, segments):
    pos = jnp.full(rows.shape, -1.0, F32)
    seqlen = jnp.full(rows.shape, 1.0, F32)
    for start, bsz, lp in segments:
        rel = rows - float(start)
        q = jnp.floor((rel + 0.5) * (1.0 / lp))
        inside = jnp.where(rel >= 0.0, jnp.where(rel < float(bsz * lp), 1.0, 0.0), 0.0) > 0.5
        pos = jnp.where(inside, rel - q * lp, pos)
        seqlen = jnp.where(inside, float(lp), seqlen)
    return pos, seqlen


def _sigmoid(x):
    return 1.0 / (1.0 + jnp.exp(-x))


def _d_prep_kernel(u_ref, up_ref, un_ref, mup_ref, mun_ref, w2_ref, a2_ref, gup_ref, w0_ref, a0_ref,
                   kk_ref, ka_ref, rk_ref, bs_ref,
                   r_ref, v_ref, a_ref, lwf_ref, lwb_ref, kdf_ref, kdb_ref, bf_ref, bb_ref,
                   bonus_ref, g_ref, *, tm, segments):
    i = pl.program_id(0)
    u = u_ref[...]
    rows = (lax.broadcasted_iota(jnp.int32, (tm, 1), 0) + i * tm).astype(F32)
    pos, seqlen = _seq_position(rows, segments)
    edge = lax.broadcasted_iota(jnp.int32, (8, 1), 0)
    u_prev = pltpu.roll(u, 1, 0)
    u_prev = jnp.concatenate([jnp.where(edge == 0, up_ref[7:8, :], u_prev[:8]), u_prev[8:]], axis=0)
    u_next = pltpu.roll(u, tm - 1, 0)
    u_next = jnp.concatenate([u_next[:tm - 8], jnp.where(edge == 7, un_ref[0:1, :], u_next[tm - 8:])], axis=0)
    u_prev = jnp.where(pos == float(PAD), 0.0, u_prev)
    u_next = jnp.where(pos == seqlen - 1.0, 0.0, u_next)
    u = u + mup_ref[...] * (u_prev - u) + mun_ref[...] * (u_next - u)
    valid = jnp.where(pos >= float(PAD), 1.0, 0.0)

    w = D_WIDTH
    r, k, v = u[:, :w], u[:, w:2 * w], u[:, 2 * w:3 * w]
    c0 = 3 * w
    dec = _dot(jnp.tanh(u[:, c0:c0 + 2 * DECAY_RANK]).astype(BF16), w2_ref[...]) + w0_ref[...]
    c0 += 2 * DECAY_RANK
    icl = _dot(u[:, c0:c0 + 2 * ICLR_RANK].astype(BF16), a2_ref[...]) + a0_ref[...]
    c0 += 2 * ICLR_RANK
    g_ref[...] = _dot(_sigmoid(u[:, c0:c0 + GATE_RANK]).astype(BF16), gup_ref[...]).astype(g_ref.dtype)

    bs = bs_ref[...]

    def head_sum(x):
        xb = x.astype(BF16)
        return jnp.concatenate(
            [_dot(xb[:, s * HGROUP:(s + 1) * HGROUP], bs) for s in range(w // HGROUP)], axis=1
        ) * float(HEAD_DIM)

    kk = k * kk_ref[...]
    kk = kk * lax.rsqrt(jnp.maximum(head_sum(kk * kk), 1e-24))
    r_ref[...] = r.astype(r_ref.dtype)
    v_ref[...] = v.astype(v_ref.dtype)
    a_ref[...] = (-kk * valid).astype(a_ref.dtype)
    bonus = jnp.zeros_like(r)
    for d, (lw_ref, kd_ref, b_ref) in enumerate(((lwf_ref, kdf_ref, bf_ref), (lwb_ref, kdb_ref, bb_ref))):
        x = -dec[:, d * w:(d + 1) * w]
        softplus = jnp.maximum(x, 0.0) + jnp.log(1.0 + jnp.exp(-jnp.abs(x)))
        lw_ref[...] = -jnp.exp(-softplus - 0.5)
        gate = _sigmoid(icl[:, d * w:(d + 1) * w])
        kd = k * (1.0 + (gate - 1.0) * ka_ref[...])
        kd_ref[...] = (kd * valid).astype(kd_ref.dtype)
        b_ref[...] = (kk * gate * valid).astype(b_ref.dtype)
        bonus = bonus + head_sum(r * kd * rk_ref[...]) * v
    bonus_ref[...] = bonus


def _scan_kernel(fwd_blk, bwd_blk, first,
                 rf, vf, af, lwf, kdf, bf_, rb, vb, ab, lwb, kdb, bb_,
                 of_ref, ob_ref, s_scr, *, nk):
    del fwd_blk, bwd_blk
    step = pl.program_id(0)
    c = SCAN_CHUNK

    @pl.when(first[step] == 1)
    def _():
        s_scr[...] = jnp.zeros(s_scr.shape, F32)

    t_sq = lax.broadcasted_iota(jnp.int32, (nk * c, nk * c), 0)
    s_sq = lax.broadcasted_iota(jnp.int32, (nk * c, nk * c), 1)
    t_cat = lax.broadcasted_iota(jnp.int32, (c, HGROUP), 0)
    s_cat = lax.broadcasted_iota(jnp.int32, (c, HGROUP), 1) & (c - 1)
    bd_row = lax.broadcasted_iota(jnp.int32, (HGROUP, HGROUP), 0) >> 6
    bd_col = lax.broadcasted_iota(jnp.int32, (HGROUP, HGROUP), 1) >> 6
    bd_mask = bd_row == bd_col
    eye_cat = jnp.where(t_cat == s_cat, 1.0, 0.0).astype(F32)

    def block_diag(x):
        return jnp.where(bd_mask, jnp.concatenate([x] * (HGROUP // c), axis=0), 0.0).astype(BF16)

    chains = []
    dirs = ((0, rf, vf, af, lwf, kdf, bf_), (1, rb, vb, ab, lwb, kdb, bb_))
    for d, r_ref, v_ref, a_ref, lw_ref, kd_ref, b_ref in dirs:
        rev = d == 1
        earlier = (s_sq >= t_sq) if rev else (s_sq <= t_sq)
        tri = jnp.where((t_sq >> 6) == (s_sq >> 6), jnp.where(earlier, 1.0, 0.0), 0.0).astype(BF16)
        lw = lw_ref[...]
        h1, h2 = _split2(lw)
        cum = _dot(tri, h1) + _dot(tri, h2)
        e_in = jnp.exp(cum)
        e_neg = jnp.exp(-cum)
        a_t = a_ref[...] * jnp.exp(cum - lw)
        r_t = r_ref[...] * e_in
        b_t = b_ref[...] * e_neg
        k_t = kd_ref[...] * e_neg
        v_all = v_ref[...]
        for ck in range(nk):
            rows = slice(ck * c, (ck + 1) * c)
            w_row = ck * c if rev else (ck + 1) * c - 1
            for g in range(D_WIDTH // HGROUP):
                cs = slice(g * HGROUP, (g + 1) * HGROUP)
                chains.append(dict(d=d, ck=ck, g=g, rev=rev, at=a_t[rows, cs], rt=r_t[rows, cs],
                                   bt=b_t[rows, cs], kt=k_t[rows, cs], vc=v_all[rows, cs],
                                   w=e_in[w_row:w_row + 1, cs]))

    def strict(ch):
        return (s_cat > t_cat) if ch['rev'] else (s_cat < t_cat)

    def incl(ch):
        return (s_cat >= t_cat) if ch['rev'] else (s_cat <= t_cat)

    for ch in chains:
        ch['lhs'] = jnp.concatenate([ch['at'], ch['rt']], axis=0).astype(BF16)
        ch['bd_v'] = block_diag(ch['vc'])
    m_b = [_dot_nt(ch['lhs'], block_diag(ch['bt'])) for ch in chains]
    m_k = [_dot_nt(ch['lhs'], block_diag(ch['kt'])) for ch in chains]
    for ch, mb, mk in zip(chains, m_b, m_k):
        ch['a_ab'] = jnp.where(strict(ch), mb[:c], 0.0)
        ch['a_rb'] = jnp.where(incl(ch), mb[c:], 0.0).astype(BF16)
        ch['a_ak'] = jnp.where(strict(ch), mk[:c], 0.0).astype(BF16)
        ch['a_rk'] = jnp.where(incl(ch), mk[c:], 0.0).astype(BF16)
        ch['t'] = eye_cat + jnp.where((t_cat >> 1) == (s_cat >> 1), ch['a_ab'], 0.0)
    m = 2
    while m < c:
        sh = int(math.log2(m))
        xs = []
        for ch in chains:
            a_m = jnp.where((t_cat >> (sh + 1)) == (s_cat >> (sh + 1)),
                            jnp.where((t_cat >> sh) != (s_cat >> sh), ch['a_ab'], 0.0), 0.0)
            xs.append(_dot(a_m.astype(BF16), block_diag(ch['t'])))
        for ch, x in zip(chains, xs):
            ch['t'] = ch['t'] + _dot(ch['t'].astype(BF16), block_diag(x))
        m *= 2
    av = [_dot(jnp.concatenate([ch['a_ak'], ch['a_rk']], axis=0), ch['bd_v']) for ch in chains]
    akv = [x[:c] for x in av]
    for ch in chains:
        ch['t'] = ch['t'].astype(BF16)
    t_both = [jnp.concatenate([ch['t'], _dot(ch['a_rb'], block_diag(ch['t'])).astype(BF16)], axis=0)
              for ch in chains]
    p_a = [_dot(tb, block_diag(ch['at'])) for ch, tb in zip(chains, t_both)]
    p_u = [_dot(tb, block_diag(x)) for tb, x in zip(t_both, akv)]
    a_hat = [x[:c] for x in p_a]
    u_hat = [x[:c] for x in p_u]
    r_hat = [(ch['rt'] + x[c:]).astype(BF16) for ch, x in zip(chains, p_a)]
    o_hat = [x[c:] + y[c:] for x, y in zip(p_u, av)]
    m_c = [jnp.where(bd_mask, _dot_tn(x.astype(BF16), ch['bt'].astype(BF16)), 0.0).astype(BF16)
           for ch, x in zip(chains, a_hat)]
    n_c = [jnp.where(bd_mask, _dot_tn(jnp.concatenate([x, ch['vc']], axis=0).astype(BF16),
                                      jnp.concatenate([ch['bt'], ch['kt']], axis=0).astype(BF16)), 0.0)
           for ch, x in zip(chains, u_hat)]
    pre = {(ch['d'], ch['ck'], ch['g']): (r_hat[i], o_hat[i], m_c[i], n_c[i], ch['w'])
           for i, ch in enumerate(chains)}

    groups = [(d, g) for d in range(2) for g in range(D_WIDTH // HGROUP)]
    states = {dg: s_scr[dg[0], dg[1]] for dg in groups}
    for i in range(nk):
        st_b = {dg: states[dg].astype(BF16) for dg in groups}
        for d, g in groups:
            ck = nk - 1 - i if d == 1 else i
            rh, oh, mc, nc_, w = pre[(d, ck, g)]
            o_ref = ob_ref if d == 1 else of_ref
            o_ref[ck * c:(ck + 1) * c, g * HGROUP:(g + 1) * HGROUP] = _dot_nt(rh, st_b[(d, g)]) + oh
            states[(d, g)] = (states[(d, g)] + _dot(st_b[(d, g)], mc) + nc_) * w
    for d, g in groups:
        s_scr[d, g] = states[(d, g)]


def _scan_tables(segments, rows):
    fwd, bwd, first = [], [], []
    for start, bsz, lp in segments:
        nc = lp // rows
        for b in range(bsz):
            base = (start + b * lp) // rows
            for ci in range(nc):
                fwd.append(base + ci)
                bwd.append(base + nc - 1 - ci)
                first.append(1 if ci == 0 else 0)
    as_i32 = lambda t: jnp.asarray(np.asarray(t, np.int32))
    return as_i32(fwd), as_i32(bwd), as_i32(first)


def _scan(segments, n_rows, r, v, a, lwf, lwb, kdf, kdb, bf_, bb_):
    nk = SCAN_CHUNKS_PER_STEP
    rows = nk * SCAN_CHUNK
    if any(start % rows or lp % rows for start, _, lp in segments):
        nk, rows = 1, SCAN_CHUNK
    fwd, bwd, first = _scan_tables(segments, rows)
    nsteps = fwd.shape[0]
    blk = (rows, D_WIDTH)
    fspec = pl.BlockSpec(blk, lambda s, fw, bw, fi: (fw[s], 0))
    bspec = pl.BlockSpec(blk, lambda s, fw, bw, fi: (bw[s], 0))
    grid_spec = pltpu.PrefetchScalarGridSpec(
        num_scalar_prefetch=3, grid=(nsteps,),
        in_specs=[fspec] * 6 + [bspec] * 6,
        out_specs=[fspec, bspec],
        scratch_shapes=[pltpu.VMEM((2, D_WIDTH // HGROUP, HGROUP, HGROUP), F32)],
    )
    return pl.pallas_call(
        functools.partial(_scan_kernel, nk=nk), grid_spec=grid_spec,
        out_shape=[jax.ShapeDtypeStruct((n_rows, D_WIDTH), F32)] * 2,
        compiler_params=_cparams(("arbitrary",)), name="wkv_scan",
    )(fwd, bwd, first, r, v, a, lwf, kdf, bf_, r, v, a, lwb, kdb, bb_)


def _rope_tables(segments, n_rows, layer_kind):
    outs = []
    for start, bsz, lp in segments:
        p = jnp.arange(lp)
        d = jnp.arange(HEAD_DIM)
        if layer_kind == "axial":
            t = p - FRONT
            row = jnp.where(t >= 0, t // GRID_W, jnp.where(p >= PAD, t, 0)).astype(F32)
            col = jnp.where(t >= 0, t % GRID_W, jnp.where(p >= PAD, t, 0)).astype(F32)
            half = HEAD_DIM // 2
            inv = A_THETA ** (-jnp.arange(0, half, 2, dtype=F32) / half)
            ang = jnp.concatenate([row[:, None] * inv] * 2 + [col[:, None] * inv] * 2, axis=1)
            first = (d % 32) < 16
            cos = jnp.cos(ang)
            sa = jnp.where(first[None, :], -jnp.sin(ang), 0.0)
            sb = jnp.where(first[None, :], 0.0, jnp.sin(ang))
        else:
            pos = jnp.maximum(p - PAD, 0).astype(F32)
            inv = ROPE_THETA ** (-jnp.arange(0, ROPE_DIMS, 2, dtype=F32) / ROPE_DIMS)
            ang8 = pos[:, None] * inv
            ang = jnp.concatenate([ang8, ang8] + [jnp.zeros_like(ang8)] * 6, axis=1)
            cos = jnp.where((d < ROPE_DIMS)[None, :], jnp.cos(ang), 1.0)
            sa = jnp.where((d < 8)[None, :], -jnp.sin(ang), 0.0)
            sb = jnp.where(((d >= 8) & (d < 16))[None, :], jnp.sin(ang), 0.0)
        small = lax.optimization_barrier([jnp.concatenate([t_, t_], axis=1) for t_ in (cos, sa, sb)])
        tabs = [jnp.tile(t_, (bsz, 1)) for t_ in small]
        outs.append(tabs)
    tail = n_rows - sum(b * lp for _, b, lp in segments)
    res = []
    for idx in range(3):
        parts = [o[idx] for o in outs] + [jnp.zeros((tail, LANES), F32)]
        res.append(jnp.concatenate(parts, axis=0))
    return res


def _seq_view(flat, seg, width):
    start, bsz, lp = seg
    return flat[start:start + bsz * lp].reshape(bsz, lp, width)


def _attn_operands(qT, k, vT, seg):
    start, bsz, lp = seg
    chunk = vT.shape[2]
    k_seg = k[start:start + bsz * lp].reshape(bsz, lp, k.shape[1])
    v_seg = vT[start // chunk:(start + bsz * lp) // chunk].reshape(bsz, lp // chunk, vT.shape[1], chunk)
    return qT, k_seg, v_seg, start


def _to_flat(parts, n_rows, width, dtype):
    used = sum(p.shape[0] for p in parts)
    return jnp.concatenate(parts + [jnp.zeros((n_rows - used, width), dtype)], axis=0)


def _block_avg(n, group):
    idx = np.arange(n) // group
    return jnp.asarray((idx[:, None] == idx[None, :]).astype(np.float32) / group).astype(BF16)


def _forward(xs, p):
    segments = []
    start = 0
    for x in xs:
        bsz, s, _ = x.shape
        lp = s + FRONT
        segments.append((start, bsz, lp))
        start += bsz * lp
    n_used = start
    tm = 512 if n_used >= 4096 else ATTN_CHUNK
    n_rows = _round_up(n_used, tm)

    meta = p['meta_tokens'].astype(F32)
    parts = []
    for x in xs:
        bsz = x.shape[0]
        lead = jnp.concatenate([jnp.zeros((PAD, D_MODEL), F32), meta], axis=0)
        parts.append(jnp.concatenate([jnp.broadcast_to(lead[None], (bsz, FRONT, D_MODEL)), x], axis=1)
                     .reshape(-1, D_MODEL))
    h = _to_flat(parts, n_rows, D_MODEL, F32)

    bs = _block_avg(HGROUP, HEAD_DIM)
    row = lambda t: t.reshape(1, -1).astype(F32)

    depth = p['pre_mix_g'].shape[0]
    for i in range(depth):
        if i % 2 == 0:
            e = i // 2
            cos, sa, sb = _rope_tables(segments, n_rows, "axial")
            ch = np.arange(B_DIM)
            ang = 2.0 * np.pi * ((ch[:, None] * ch[None, :]) % B_DIM) / B_DIM
            cc, sc = jnp.asarray(np.cos(ang), F32), jnp.asarray(np.sin(ang), F32)
            wl = p['b_w'][e].astype(F32)
            hp = lax.Precision.HIGHEST
            pmat = jnp.einsum('cd,gde->gce', cc, wl, precision=hp)
            qmat = -jnp.einsum('cd,gde->gce', sc, wl, precision=hp)
            bd = lambda m: jax.scipy.linalg.block_diag(*[m[g] for g in range(B_GROUPS)])
            pq_base = jnp.concatenate([bd(pmat), bd(qmat)], axis=1)
            qT, k, vT, gf = _row_call(
                "in_even", _in_even_kernel, n_rows, tm, [h, cos, sa, sb],
                [row(p['pre_mix_g'][i]), p['even_w_in'][e].astype(BF16),
                 row(jnp.tile(p['a_q_gain'][e], A_HEADS) * (HEAD_DIM ** -0.5 * LOG2_E)),
                 row(jnp.tile(p['a_k_gain'][e], A_KV_HEADS)), row(p['b_norm_g'][e]), bs,
                 pq_base.astype(BF16)],
                [("cols", A_Q, BF16), ("rows", A_KV, BF16), ("chunks", A_KV, BF16), ("rows", 2 * B_W, F32)])
            ya_parts, yb_parts = [], []
            for seg in segments:
                _, bsz, lp = seg
                ya_parts.append(_attn_a(*_attn_operands(qT, k, vT, seg)))
                gseq = _seq_view(gf, seg, 2 * B_W)[:, PAD:]
                yb = _mixer_b(gseq, row(p['b_b'][e]))
                yb_parts.append(jnp.pad(yb, ((0, 0), (PAD, 0), (0, 0))).reshape(bsz * lp, B_W).astype(BF16))
            yat = ya_parts
            mix_kernel, mix_rows, mix_consts = _mix_ffn_even_kernel, [_to_flat(yb_parts, n_rows, B_W, BF16)], []
            w_out = p['even_w_out'][e].astype(BF16)
            wa, wb = w_out[:A_Q], w_out[A_Q:]
        else:
            o = i // 2
            cos, sa, sb = _rope_tables(segments, n_rows, "partial")
            qT, k, vT, u = _row_call(
                "in_odd", _in_odd_kernel, n_rows, tm, [h, cos, sa, sb],
                [row(p['pre_mix_g'][i]), p['odd_w_in'][o].astype(BF16)],
                [("cols", C_Q, BF16), ("rows", C_KV, BF16), ("chunks", C_KV, BF16), ("rows", D_IN, F32)])
            sink = p['c_sink'][o].astype(F32)
            yat = [_attn_c(sink, *_attn_operands(qT, k, vT, seg)) for seg in segments]
            mix_kernel = _mix_ffn_odd_kernel
            mix_rows, mix_consts = _mixer_d(u, p, o, bs, segments, n_rows, tm)
            w_out = p['odd_w_out'][o].astype(BF16)
            wa, wb = w_out[:C_Q], w_out[C_Q:]
        (h,) = _row_call(
            "mix_ffn",
            functools.partial(mix_kernel, chunk=256, group_tiles=tuple(s[0] // tm for s in segments)),
            n_rows, tm, [h] + mix_rows,
            mix_consts + [wa, wb, row(p['post_mix_g'][i]),
             row(p['pre_ffn_g'][i]), p['ffn_w_gate'][i].astype(BF16), p['ffn_w_up'][i].astype(BF16),
             p['ffn_w_down'][i].astype(BF16), row(p['post_ffn_g'][i])],
            [("rows", D_MODEL, F32)], scratch=[pltpu.VMEM((tm, D_MODEL), F32)], col_ins=yat)

    outs = []
    for seg in segments:
        outs.append(_seq_view(h, seg, D_MODEL)[:, FRONT:])
    return tuple(outs)


def _mixer_d(u, p, o, bs, segments, n_rows, tm):
    row = lambda t: t.reshape(1, -1).astype(F32)
    w = D_WIDTH
    zeros = lambda r: jnp.zeros((r, w), F32)
    w2 = jnp.concatenate([jnp.concatenate([p['d_w_up'][o][0], zeros(DECAY_RANK)], axis=1),
                          jnp.concatenate([zeros(DECAY_RANK), p['d_w_up'][o][1]], axis=1)], axis=0)
    a2 = jnp.concatenate([jnp.concatenate([p['d_a_up'][o][0], zeros(ICLR_RANK)], axis=1),
                          jnp.concatenate([zeros(ICLR_RANK), p['d_a_up'][o][1]], axis=1)], axis=0)
    consts = [row(p['d_mu_prev'][o]), row(p['d_mu_next'][o]), w2.astype(BF16), a2.astype(BF16),
              p['d_g_up'][o].astype(BF16), row(p['d_w0'][o]), row(p['d_a0'][o]),
              row(p['d_k_k'][o]), row(p['d_k_a'][o]), row(p['d_r_k'][o]), bs]
    nb8 = n_rows // 8
    t8 = tm // 8
    in_specs = [pl.BlockSpec((tm, D_IN), lambda i: (i, 0)),
                pl.BlockSpec((8, D_IN), lambda i: (jnp.maximum(i * t8 - 1, 0), 0)),
                pl.BlockSpec((8, D_IN), lambda i: (jnp.minimum((i + 1) * t8, nb8 - 1), 0))]
    in_specs += [pl.BlockSpec(a.shape, lambda i, nd=a.ndim: (0,) * nd) for a in consts]
    out_dtypes = [BF16, BF16, BF16, F32, F32, BF16, BF16, BF16, BF16, F32, BF16]
    prep = pl.pallas_call(
        functools.partial(_d_prep_kernel, tm=tm, segments=tuple(segments)),
        grid=(n_rows // tm,), in_specs=in_specs,
        out_specs=[pl.BlockSpec((tm, w), lambda i: (i, 0))] * len(out_dtypes),
        out_shape=[jax.ShapeDtypeStruct((n_rows, w), dt) for dt in out_dtypes],
        compiler_params=_cparams(("parallel",)), name="d_prep",
    )(u, u, u, *consts)
    r, v, a, lwf, lwb, kdf, kdb, bf_, bb_, bonus, g = prep
    of, ob = _scan(segments, n_rows, r, v, a, lwf, lwb, kdf, kdb, bf_, bb_)
    return [of, ob, bonus, g], [row(p['d_ln_g'][o]), row(p['d_ln_b'][o]), bs]


def kernel(x_prompt, x_sample, meta_tokens, pre_mix_g, post_mix_g, pre_ffn_g, post_ffn_g, even_w_in, even_w_out, a_q_gain, a_k_gain, b_norm_g, b_w, b_b, odd_w_in, odd_w_out, c_sink, d_mu_prev, d_mu_next, d_w0, d_w_up, d_a0, d_a_up, d_g_up, d_k_k, d_k_a, d_r_k, d_ln_g, d_ln_b, ffn_w_gate, ffn_w_up, ffn_w_down):
    params = dict(meta_tokens=meta_tokens, pre_mix_g=pre_mix_g, post_mix_g=post_mix_g,
                  pre_ffn_g=pre_ffn_g, post_ffn_g=post_ffn_g,
                  even_w_in=even_w_in, even_w_out=even_w_out, a_q_gain=a_q_gain, a_k_gain=a_k_gain,
                  b_norm_g=b_norm_g, b_w=b_w, b_b=b_b,
                  odd_w_in=odd_w_in, odd_w_out=odd_w_out, c_sink=c_sink,
                  d_mu_prev=d_mu_prev, d_mu_next=d_mu_next, d_w0=d_w0, d_w_up=d_w_up,
                  d_a0=d_a0, d_a_up=d_a_up, d_g_up=d_g_up, d_k_k=d_k_k, d_k_a=d_k_a, d_r_k=d_r_k,
                  d_ln_g=d_ln_g, d_ln_b=d_ln_b,
                  ffn_w_gate=ffn_w_gate, ffn_w_up=ffn_w_up, ffn_w_down=ffn_w_down)
    return _forward([x_prompt, x_sample], params)
```
